```python
import jax, jax.numpy as jnp
from jax import lax
import numpy as np

D_MODEL = 2048
BATCH = 4
SEQ = 2048
DEPTH = 4
DEC_BATCH = 128
DEC_SEQ = 8
PAST_LEN = 16384
PAGE_SIZE = 128

D_A = D_MODEL // 2
HEAD_A = 64
N_HEADS_A = D_A // HEAD_A
LORA_W = 64
LORA_A = 64
D_B = D_MODEL // 4
POOL_WINDOWS = (2, 4, 8, 16)
N_POOL = len(POOL_WINDOWS)
POOL_GC = D_B // N_POOL
POOL_BUF = max(POOL_WINDOWS) - 1
D_C = D_MODEL - D_A - D_B
N_GROUPS_C = 4
GC = D_C // N_GROUPS_C
CHUNK = 128
SHIFT_W = 3 * D_A + LORA_W + LORA_A
SPLIT_SIZES = (SHIFT_W, D_A, D_B, D_B, D_C, D_C, D_C)
SPLIT_IDX = tuple(int(i) for i in np.cumsum(SPLIT_SIZES)[:-1])
D_IN = sum(SPLIT_SIZES)
EPS = 1e-6
GN_EPS = HEAD_A * 1e-5
LN_EPS = 1e-5

kernel_name = "hybrid_rwkv7_pool_chunkgate_decode_step"

F32 = jnp.float32


def rms_norm(x, g):
    xf = x.astype(F32)
    y = xf * lax.rsqrt(jnp.mean(xf * xf, axis=-1, keepdims=True) + EPS)
    return (y * g.astype(F32)).astype(x.dtype)


def wkv7_scan(r, w, k, v, kk, a, s0):
    def step(S, inp):
        r_t, w_t, k_t, v_t, kk_t, a_t = inp
        sa = jnp.einsum('bhij,bhj->bhi', S, -kk_t)
        S = (S * w_t[:, :, None, :] + sa[..., None] * (kk_t * a_t)[:, :, None, :]
             + v_t[..., None] * k_t[:, :, None, :])
        return S, jnp.einsum('bhij,bhj->bhi', S, r_t)
    xs = tuple(jnp.moveaxis(t, 1, 0) for t in (r, w, k, v, kk, a))
    s_T, ys = lax.scan(step, s0, xs)
    return jnp.moveaxis(ys, 0, 1), s_T


def rwkv7_mix(ps, prev, s0, mu, w0, w_up, a0, a_up, k_k, k_a, r_k, lnx_g, lnx_b):
    B, T, _ = ps.shape
    psf = ps.astype(F32)
    shifted = jnp.concatenate([prev.astype(F32)[:, None], psf[:, :-1]], axis=1)
    xs = psf + (shifted - psf) * mu.astype(F32)
    r, k, v, xw, xa = jnp.split(xs, [D_A, 2 * D_A, 3 * D_A, 3 * D_A + LORA_W], axis=-1)
    w_log = -jax.nn.softplus(-(w0 + jnp.tanh(xw) @ w_up)) - 0.5
    w = jnp.exp(-jnp.exp(w_log))
    a = jax.nn.sigmoid(a0 + xa @ a_up)
    hs = lambda t: t.reshape(B, T, N_HEADS_A, HEAD_A)
    kk = hs(k * k_k)
    kk = kk * lax.rsqrt(jnp.maximum(jnp.sum(kk * kk, axis=-1, keepdims=True), 1e-12))
    k = k * (1.0 + (a - 1.0) * k_a)
    r, w, k, v, a = hs(r), hs(w), hs(k), hs(v), hs(a)
    y, s_T = wkv7_scan(r, w, k, v, kk, a, s0.astype(F32))
    ym = jnp.mean(y, axis=-1, keepdims=True)
    yv = jnp.mean(jnp.square(y - ym), axis=-1, keepdims=True)
    yn = ((y - ym) * lax.rsqrt(yv + GN_EPS)).reshape(B, T, D_A) * lnx_g + lnx_b
    bonus = (jnp.sum(r * k * r_k, axis=-1, keepdims=True) * v).reshape(B, T, D_A)
    return yn + bonus, ps[:, -1], s_T


def pool_mix(u, buf, pos0, pool_w, pool_scale):
    B, T, _ = u.shape
    full = jnp.concatenate([buf.astype(F32), u.astype(F32)], axis=1)
    cs = jnp.concatenate([jnp.zeros((B, 1, D_B), F32), jnp.cumsum(full, axis=1)], axis=1)
    hi = cs[:, POOL_BUF + 1:]
    pos = pos0 + jnp.arange(T)
    groups = []
    for g, win in enumerate(POOL_WINDOWS):
        sl = slice(g * POOL_GC, (g + 1) * POOL_GC)
        lo = cs[:, POOL_BUF + 1 - win:POOL_BUF + 1 - win + T, sl]
        cnt = jnp.minimum(pos + 1, win).astype(F32)[None, :, None]
        groups.append((hi[..., sl] - lo) / cnt)
    pooled = jnp.concatenate(groups, axis=-1) - full[:, POOL_BUF:]
    mixed = jnp.einsum('btgc,gcd->btgd', pooled.reshape(B, T, N_POOL, POOL_GC), pool_w)
    mixed = mixed.reshape(B, T, D_B) * pool_scale
    return mixed, full[:, -POOL_BUF:].astype(u.dtype)


def chunk_gate(u, v, ln_g, w_s, b_s):
    B, T, _ = v.shape
    vf = v.astype(F32)
    vm = jnp.mean(vf, axis=-1, keepdims=True)
    vv = jnp.mean(jnp.square(vf - vm), axis=-1, keepdims=True)
    vn = (vf - vm) * lax.rsqrt(vv + LN_EPS) * ln_g
    L = min(T, CHUNK)
    n = -(-T // L)
    pad = n * L - T
    ws = w_s[:, :L, :L] * jnp.tril(jnp.ones((L, L), F32))
    vp = jnp.pad(vn, ((0, 0), (0, pad), (0, 0))).reshape(B, n, L, N_GROUPS_C, GC)
    mix = jnp.einsum('gts,bnsgc->bntgc', ws, vp) + jnp.transpose(b_s[:, :L])[None, None, :, :, None]
    mix = mix.reshape(B, n * L, D_C)[:, :T]
    return u.astype(F32) * mix, vn.astype(v.dtype)


def hybrid_layer(x, st_shift, st_wkv, st_pool, pos0, p):
    h = rms_norm(x, p['norm_g'])
    proj = h @ p['w_in']
    ps, g_a, u_b, g_b, u_c, v_c, g_c = jnp.split(proj, SPLIT_IDX, axis=-1)
    y_a, new_shift, new_wkv = rwkv7_mix(ps, st_shift, st_wkv, p['mu'], p['w0'], p['w_up'], p['a0'],
                                        p['a_up'], p['k_k'], p['k_a'], p['r_k'], p['lnx_g'], p['lnx_b'])
    y_b, new_pool = pool_mix(u_b, st_pool, pos0, p['pool_w'], p['pool_scale'])
    y_c, vn_c = chunk_gate(u_c, v_c, p['gmlp_ln_g'], p['gmlp_ws'], p['gmlp_b'])
    cat = jnp.concatenate([y_a * jax.nn.silu(g_a.astype(F32)),
                           y_b * jax.nn.silu(g_b.astype(F32)),
                           y_c * jax.nn.silu(g_c.astype(F32))], axis=-1)
    out = x + (cat @ p['w_out'].astype(F32)).astype(x.dtype)
    return out, new_shift, new_wkv, new_pool, vn_c


def setup_inputs(seed: int = 0) -> dict:
    key = jax.random.key(seed)
    ks = jax.random.split(key, 26)
    nrm = lambda k, s: jax.random.normal(k, s, F32)
    return {
        "x_prompt": nrm(ks[0], (BATCH, SEQ, D_MODEL)),
        "x_sample": nrm(ks[1], (DEC_BATCH, DEC_SEQ, D_MODEL)),
        "state_shift": nrm(ks[2], (DEPTH, DEC_BATCH, SHIFT_W)),
        "state_wkv": 0.3 * nrm(ks[3], (DEPTH, DEC_BATCH, N_HEADS_A, HEAD_A, HEAD_A)),
        "state_pool": nrm(ks[4], (DEPTH, DEC_BATCH, POOL_BUF, D_B)),
        "norm_g": 1.0 + 0.02 * nrm(ks[5], (DEPTH, D_MODEL)),
        "final_norm_g": 1.0 + 0.02 * nrm(ks[6], (D_MODEL,)),
        "w_in": nrm(ks[7], (DEPTH, D_MODEL, D_IN)) * D_MODEL ** -0.5,
        "shift_mu": jax.random.uniform(ks[8], (DEPTH, SHIFT_W), F32),
        "w0": -0.5 + 0.5 * nrm(ks[9], (DEPTH, D_A)),
        "w_up": 0.1 * nrm(ks[10], (DEPTH, LORA_W, D_A)) * LORA_W ** -0.5,
        "a0": 0.1 * nrm(ks[11], (DEPTH, D_A)),
        "a_up": 0.1 * nrm(ks[12], (DEPTH, LORA_A, D_A)) * LORA_A ** -0.5,
        "k_k": 0.85 + 0.05 * nrm(ks[13], (DEPTH, D_A)),
        "k_a": 1.0 + 0.05 * nrm(ks[14], (DEPTH, D_A)),
        "r_k": 0.1 * nrm(ks[15], (DEPTH, N_HEADS_A, HEAD_A)),
        "lnx_g": 1.0 + 0.02 * nrm(ks[16], (DEPTH, D_A)),
        "lnx_b": 0.02 * nrm(ks[17], (DEPTH, D_A)),
        "pool_w": nrm(ks[18], (DEPTH, N_POOL, POOL_GC, POOL_GC)) * POOL_GC ** -0.5,
        "pool_scale": 0.5 + 0.1 * nrm(ks[19], (DEPTH, D_B)),
        "gmlp_ln_g": 1.0 + 0.02 * nrm(ks[20], (DEPTH, D_C)),
        "gmlp_ws": nrm(ks[21], (DEPTH, N_GROUPS_C, CHUNK, CHUNK)) * CHUNK ** -0.5,
        "gmlp_b": 1.0 + 0.1 * nrm(ks[22], (DEPTH, N_GROUPS_C, CHUNK)),
        "w_out": 0.5 * nrm(ks[23], (DEPTH, D_MODEL, D_MODEL)) * D_MODEL ** -0.5,
    }


def reference(x_prompt, x_sample, state_shift, state_wkv, state_pool, norm_g, final_norm_g, w_in,
              shift_mu, w0, w_up, a0, a_up, k_k, k_a, r_k, lnx_g, lnx_b, pool_w, pool_scale,
              gmlp_ln_g, gmlp_ws, gmlp_b, w_out):
    bp = x_prompt.shape[0]
    xp, xs = x_prompt, x_sample
    p_shift, p_wkv, p_pool = [], [], []
    s_shift, s_wkv, s_pool, s_v = [], [], [], []
    for l in range(DEPTH):
        prm = dict(norm_g=norm_g[l], w_in=w_in[l], mu=shift_mu[l], w0=w0[l], w_up=w_up[l], a0=a0[l],
                   a_up=a_up[l], k_k=k_k[l], k_a=k_a[l], r_k=r_k[l], lnx_g=lnx_g[l], lnx_b=lnx_b[l],
                   pool_w=pool_w[l], pool_scale=pool_scale[l], gmlp_ln_g=gmlp_ln_g[l],
                   gmlp_ws=gmlp_ws[l], gmlp_b=gmlp_b[l], w_out=w_out[l])
        xp, sh, wk, po, _ = hybrid_layer(
            xp, jnp.zeros((bp, SHIFT_W), xp.dtype), jnp.zeros((bp, N_HEADS_A, HEAD_A, HEAD_A), F32),
            jnp.zeros((bp, POOL_BUF, D_B), xp.dtype), 0, prm)
        p_shift.append(sh); p_wkv.append(wk); p_pool.append(po)
        xs, sh, wk, po, vc = hybrid_layer(xs, state_shift[l], state_wkv[l], state_pool[l], PAST_LEN, prm)
        s_shift.append(sh); s_wkv.append(wk); s_pool.append(po); s_v.append(vc)
    y_prompt = rms_norm(xp, final_norm_g)
    y_sample = rms_norm(xs, final_norm_g)
    return (y_prompt, y_sample, jnp.stack(p_shift), jnp.stack(p_wkv), jnp.stack(p_pool),
            jnp.stack(s_shift), jnp.stack(s_wkv), jnp.stack(s_pool), jnp.stack(s_v))
```

```python
import functools

import jax
import jax.numpy as jnp
import numpy as np
from jax import lax
from jax.experimental import pallas as pl
from jax.experimental.pallas import tpu as pltpu

F32 = jnp.float32
BF16 = jnp.bfloat16

D_MODEL = 2048
DEPTH = 4
PAST_LEN = 16384
D_A = 1024
HEAD_A = 64
N_HEADS_A = 16
LORA = 64
D_B = 512
POOL_WINDOWS = (2, 4, 8, 16)
POOL_GC = 128
POOL_BUF = 15
D_C = 512
N_GROUPS_C = 4
GC = 128
CHUNK = 128
SHIFT_W = 3 * D_A + 2 * LORA
EPS = 1e-6
GN_EPS = HEAD_A * 1e-5
LN_EPS = 1e-5

P_R, P_K, P_V, P_GA = 0, 1024, 2048, 3072
P_UB, P_GB, P_UC, P_VC, P_GC = 4096, 4608, 5120, 5632, 6144
P_LORA = 6656
D_INP = 6784

PAIR = 128
N_PAIRS = D_A // PAIR
ROWS = 64

VMEM_LIMIT = 52 * 1024 * 1024
HI = lax.Precision.HIGHEST


def _dot(a, b, prec=None):
    return jnp.dot(a, b, precision=prec, preferred_element_type=F32)


def _dot_nt(a, b, prec=None):
    return lax.dot_general(a, b, (((1,), (1,)), ((), ())), precision=prec,
                           preferred_element_type=F32)


def _dot_tn(a, b, prec=None):
    return lax.dot_general(a, b, (((0,), (0,)), ((), ())), precision=prec,
                           preferred_element_type=F32)


def _inproj_kernel(x_ref, g_ref, w_ref, o_ref, h_ref):
    @pl.when(pl.program_id(1) == 0)
    def _():
        x = x_ref[...]
        ms = jnp.mean(x * x, axis=-1, keepdims=True)
        h_ref[...] = ((x * lax.rsqrt(ms + EPS)) * g_ref[...]).astype(BF16)

    o_ref[...] = _dot(h_ref[...], w_ref[...])


def _inproj(x, g, w, tm=1024, tn=512):
    m = x.shape[0]
    return pl.pallas_call(
        _inproj_kernel,
        grid=(m // tm, pl.cdiv(D_INP, tn)),
        in_specs=[pl.BlockSpec((tm, D_MODEL), lambda i, j: (i, 0)),
                  pl.BlockSpec((1, D_MODEL), lambda i, j: (0, 0)),
                  pl.BlockSpec((D_MODEL, tn), lambda i, j: (0, j))],
        out_specs=pl.BlockSpec((tm, tn), lambda i, j: (i, j)),
        out_shape=jax.ShapeDtypeStruct((m, D_INP), F32),
        scratch_shapes=[pltpu.VMEM((tm, D_MODEL), BF16)],
        compiler_params=pltpu.CompilerParams(
            dimension_semantics=("parallel", "arbitrary"), vmem_limit_bytes=VMEM_LIMIT),
        name="inproj",
    )(x, g, w)


def _outproj_kernel(ca_ref, cb_ref, wa_ref, wb_ref, x_ref, g_ref, o_ref, *, final):
    y = _dot(ca_ref[...], wa_ref[...]) + _dot(cb_ref[...], wb_ref[...])
    out = x_ref[...] + y
    if final:
        ms = jnp.mean(out * out, axis=-1, keepdims=True)
        out = (out * lax.rsqrt(ms + EPS)) * g_ref[...]
    o_ref[...] = out


def _outproj(cat_a, cat_b, w_a, w_b, x, g, final, tm=512):
    m = x.shape[0]
    half = D_MODEL // 2
    return pl.pallas_call(
        functools.partial(_outproj_kernel, final=final),
        grid=(m // tm,),
        in_specs=[pl.BlockSpec((tm, half), lambda i: (i, 0)),
                  pl.BlockSpec((tm, half), lambda i: (i, 0)),
                  pl.BlockSpec((half, D_MODEL), lambda i: (0, 0)),
                  pl.BlockSpec((half, D_MODEL), lambda i: (0, 0)),
                  pl.BlockSpec((tm, D_MODEL), lambda i: (i, 0)),
                  pl.BlockSpec((1, D_MODEL), lambda i: (0, 0))],
        out_specs=pl.BlockSpec((tm, D_MODEL), lambda i: (i, 0)),
        out_shape=jax.ShapeDtypeStruct((m, D_MODEL), F32),
        compiler_params=pltpu.CompilerParams(
            dimension_semantics=("parallel",), vmem_limit_bytes=VMEM_LIMIT),
        name="outproj",
    )(cat_a, cat_b, w_a, w_b, x, g)


PR_MU_R, PR_MU_K, PR_MU_V, PR_W0, PR_A0, PR_KK, PR_KA, PR_RK, PR_LNG, PR_LNB, PR_MU_L = range(11)


def _softplus(z):
    return jnp.maximum(z, 0.0) + jnp.log(1.0 + jnp.exp(-jnp.abs(z)))


def _sigmoid(z):
    return 1.0 / (1.0 + jnp.exp(-z))


def _silu(z):
    return z * _sigmoid(z)


def _wkv_kernel(r_ref, k_ref, v_ref, g_ref, l_ref, sh_ref, shl_ref, s0_ref, prm_ref, w2_ref,
                o_ref, sout_ref,
                st_ref, prev_ref, prevl_ref,
                *, C, NB, n_chunks, has_state):
    c_idx = pl.program_id(1)
    rows = ROWS

    @pl.when(c_idx == 0)
    def _():
        prev_ref[...] = sh_ref[...]
        prevl_ref[...] = shl_ref[...]
        if has_state:
            lane = lax.broadcasted_iota(jnp.int32, (PAIR, PAIR), 1)
            row = lax.broadcasted_iota(jnp.int32, (PAIR, PAIR), 0)
            same = (lane // HEAD_A) == (row // HEAD_A)
            for b in range(NB):
                for p in range(N_PAIRS):
                    s = s0_ref[b, p]
                    s2 = jnp.concatenate([s, s], axis=1)
                    st_ref[p * NB + b] = jnp.where(same, s2, 0.0)
        else:
            st_ref[...] = jnp.zeros_like(st_ref)

    prm = prm_ref[...]

    def prow(i):
        return prm[i:i + 1, :]

    row_id = lax.broadcasted_iota(jnp.int32, (rows, 1), 0)
    first = (row_id % C) == 0

    def shift_mix(x, prev, mu):
        rolled = pltpu.roll(x, 1, 0)
        prev_rows = jnp.concatenate(
            [jnp.broadcast_to(prev[b], (C, x.shape[1])) for b in range(NB)], axis=0)
        shifted = jnp.where(first, prev_rows, rolled)
        return x + (shifted - x) * mu

    r_raw = r_ref[...]
    k_raw = k_ref[...]
    v_raw = v_ref[...]
    l_raw = l_ref[...]
    r = shift_mix(r_raw, prev_ref[0], prow(PR_MU_R))
    k = shift_mix(k_raw, prev_ref[1], prow(PR_MU_K))
    v = shift_mix(v_raw, prev_ref[2], prow(PR_MU_V))
    xl = shift_mix(l_raw, prevl_ref[...], prow(PR_MU_L)[:, :PAIR])

    if n_chunks > 1:
        prev_ref[0, 0] = r_raw[rows - 1:rows, :]
        prev_ref[1, 0] = k_raw[rows - 1:rows, :]
        prev_ref[2, 0] = v_raw[rows - 1:rows, :]
        prevl_ref[0] = l_raw[rows - 1:rows, :]

    lane128 = lax.broadcasted_iota(jnp.int32, (rows, PAIR), 1)
    lo_half = lane128 < HEAD_A
    w2 = w2_ref[...]
    lora_w = _dot(jnp.where(lo_half, jnp.tanh(xl), 0.0), w2, HI)
    lora_a = _dot(jnp.where(lo_half, 0.0, xl), w2, HI)
    w_log = -_softplus(-(prow(PR_W0) + lora_w)) - 0.5
    dec = jnp.exp(w_log)
    a = _sigmoid(prow(PR_A0) + lora_a)
    kk_raw = k * prow(PR_KK)
    k2 = k * (1.0 + (a - 1.0) * prow(PR_KA))
    rkr = r * k2 * prow(PR_RK)

    ri = lax.broadcasted_iota(jnp.int32, (rows, rows), 0)
    ci = lax.broadcasted_iota(jnp.int32, (rows, rows), 1)
    tri = jnp.where(((ri // C) == (ci // C)) & (ri >= ci), 1.0, 0.0)
    lastsel = jnp.where((ri // C) == (ci // C), 1.0, 0.0)
    cum = _dot(tri, -dec, HI)
    tot = _dot(lastsel, -dec, HI)
    g_incl = jnp.exp(cum)
    g_excl = jnp.exp(cum + dec)
    g_inv = jnp.exp(-cum)
    g_rest = jnp.exp(tot - cum)
    g_tot = jnp.exp(tot)

    sg = _silu(g_ref[...])

    pr = lax.broadcasted_iota(jnp.int32, (PAIR, PAIR), 0)
    pc = lax.broadcasted_iota(jnp.int32, (PAIR, PAIR), 1)
    seg = jnp.where((pr // HEAD_A) == (pc // HEAD_A), 1.0, 0.0)
    r2 = lax.broadcasted_iota(jnp.int32, (2 * rows, 2 * rows), 0)
    c2 = lax.broadcasted_iota(jnp.int32, (2 * rows, 2 * rows), 1)
    same_blk = (r2 // C) == (c2 // C)
    m_strict = same_blk & (r2 > c2)
    m_incl = same_blk & (r2 >= c2)
    eye = jnp.where(r2 == c2, 1.0, 0.0)
    same_head = (pr // HEAD_A) == (pc // HEAD_A)
    n_dbl = int(np.log2(C)) - 1

    def stack(x):
        return jnp.concatenate([jnp.where(lo_half, x, 0.0), jnp.where(lo_half, 0.0, x)], axis=0)

    def seq_rows(x, b):
        if NB == 1:
            return x
        return jnp.concatenate([x[b * C:(b + 1) * C], x[rows + b * C:rows + (b + 1) * C]], axis=0)

    def unseq_rows(pieces):
        if NB == 1:
            return pieces[0]
        return jnp.concatenate([pc_[:C] for pc_ in pieces] + [pc_[C:] for pc_ in pieces], axis=0)

    for p in range(N_PAIRS):
        sl = slice(p * PAIR, (p + 1) * PAIR)
        kkp = kk_raw[:, sl]
        nrm = _dot(kkp * kkp, seg, HI)
        kkp = kkp * lax.rsqrt(jnp.maximum(nrm, 1e-12))
        ap = a[:, sl]
        vp = v[:, sl]
        at_s = stack(-kkp * g_excl[:, sl])
        rt_s = stack(r[:, sl] * g_incl[:, sl])
        kh_s = stack(k2[:, sl] * g_inv[:, sl])
        bh_s = stack(kkp * ap * g_inv[:, sl])
        kg_s = stack(k2[:, sl] * g_rest[:, sl])
        bg_s = stack(kkp * ap * g_rest[:, sl])
        v_s = stack(vp)

        gm = _dot_nt(jnp.concatenate([at_s, rt_s], axis=0),
                     jnp.concatenate([kh_s, bh_s], axis=0), HI)
        n2 = 2 * rows
        a_ak = jnp.where(m_strict, gm[:n2, :n2], 0.0)
        a_ab = jnp.where(m_strict, gm[:n2, n2:], 0.0)
        a_rk = jnp.where(m_incl, gm[n2:, :n2], 0.0)
        a_rb = jnp.where(m_incl, gm[n2:, n2:], 0.0)

        t_inv = eye + a_ab
        pw = a_ab
        for _ in range(n_dbl):
            pw = _dot(pw, pw, HI)
            t_inv = t_inv + _dot(t_inv, pw, HI)

        x2 = _dot(a_ak, v_s, HI)
        wu = _dot(t_inv, jnp.concatenate([at_s, x2], axis=1), HI)
        w_t = wu[:, :PAIR]
        u_t = wu[:, PAIR:]

        u_parts = []
        rs_parts = []
        for b in range(NB):
            s_b = st_ref[p * NB + b]
            lhs = jnp.concatenate([seq_rows(w_t, b), seq_rows(rt_s, b)], axis=0)
            us = _dot_nt(lhs, s_b, HI)
            nb2 = 2 * C
            u_b = us[:nb2] + seq_rows(u_t, b)
            u_parts.append(u_b)
            rs_parts.append(us[nb2:])
            g_c = g_tot[b * C:b * C + 1, sl]
            upd = _dot_tn(jnp.concatenate([seq_rows(v_s, b), u_b], axis=0),
                          jnp.concatenate([seq_rows(kg_s, b), seq_rows(bg_s, b)], axis=0), HI)
            s_new = s_b * g_c + upd
            st_ref[p * NB + b] = s_new

            @pl.when(c_idx == n_chunks - 1)
            def _(s_new=s_new, b=b, p=p):
                sm = jnp.where(same_head, s_new, 0.0)
                sout_ref[b, p] = sm[:, :HEAD_A] + sm[:, HEAD_A:]
        u_full = unseq_rows(u_parts)
        rs_full = unseq_rows(rs_parts)
        y2 = rs_full + _dot(jnp.concatenate([a_rk, a_rb], axis=1),
                            jnp.concatenate([v_s, u_full], axis=0), HI)
        y = y2[:rows] + y2[rows:]

        ym = _dot(y, seg, HI) * (1.0 / HEAD_A)
        yc = y - ym
        yv = _dot(yc * yc, seg, HI) * (1.0 / HEAD_A)
        yn = yc * lax.rsqrt(yv + GN_EPS) * prow(PR_LNG)[:, sl] + prow(PR_LNB)[:, sl]
        bonus = _dot(rkr[:, sl], seg, HI) * vp
        o_ref[:, sl] = ((yn + bonus) * sg[:, sl]).astype(o_ref.dtype)


def _wkv(proj, row0, n_seq, seq_len, C, NB, sh_main, sh_lora, s0, prm, w2, has_state):
    n_chunks = seq_len // C
    n_groups = n_seq // NB
    rb0 = row0 // ROWS
    assert NB * C == ROWS and row0 % ROWS == 0

    def rmap(col):
        return lambda i, c: (rb0 + i * n_chunks + c, col)

    kern = functools.partial(_wkv_kernel, C=C, NB=NB, n_chunks=n_chunks, has_state=has_state)
    out, s_out = pl.pallas_call(
        kern,
        grid=(n_groups, n_chunks),
        in_specs=[pl.BlockSpec((ROWS, D_A), rmap(P_R // D_A)),
                  pl.BlockSpec((ROWS, D_A), rmap(P_K // D_A)),
                  pl.BlockSpec((ROWS, D_A), rmap(P_V // D_A)),
                  pl.BlockSpec((ROWS, D_A), rmap(P_GA // D_A)),
                  pl.BlockSpec((ROWS, PAIR), rmap(P_LORA // PAIR)),
                  pl.BlockSpec((3, NB, 1, D_A), lambda i, c: (0, i, 0, 0)),
                  pl.BlockSpec((NB, 1, PAIR), lambda i, c: (i, 0, 0)),
                  pl.BlockSpec((NB, N_PAIRS, PAIR, HEAD_A), lambda i, c: (i, 0, 0, 0)),
                  pl.BlockSpec((16, D_A), lambda i, c: (0, 0)),
                  pl.BlockSpec((PAIR, D_A), lambda i, c: (0, 0))],
        out_specs=[pl.BlockSpec((ROWS, D_A), lambda i, c: (i * n_chunks + c, 0)),
                   pl.BlockSpec((NB, N_PAIRS, PAIR, HEAD_A), lambda i, c: (i, 0, 0, 0))],
        out_shape=[jax.ShapeDtypeStruct((n_seq * seq_len, D_A), BF16),
                   jax.ShapeDtypeStruct((n_seq, N_PAIRS, PAIR, HEAD_A), F32)],
        scratch_shapes=[pltpu.VMEM((N_PAIRS * NB, PAIR, PAIR), F32),
                        pltpu.VMEM((3, NB, 1, D_A), F32),
                        pltpu.VMEM((NB, 1, PAIR), F32)],
        compiler_params=pltpu.CompilerParams(
            dimension_semantics=("parallel", "arbitrary"), vmem_limit_bytes=VMEM_LIMIT),
        name="wkv_c%d" % C,
    )(proj, proj, proj, proj, proj, sh_main, sh_lora, s0, prm, w2)
    return out, s_out


HALO = 16


def _window_sums(ext):
    w2 = ext + pltpu.roll(ext, 1, 0)
    w4 = w2 + pltpu.roll(w2, 2, 0)
    w8 = w4 + pltpu.roll(w4, 4, 0)
    w16 = w8 + pltpu.roll(w8, 8, 0)
    return (w2, w4, w8, w16)


def _pool_gate(pooled_groups, gb, pw_ref, pscale):
    mixed = [_dot(pg.astype(BF16), pw_ref[g]) for g, pg in enumerate(pooled_groups)]
    yb = jnp.concatenate(mixed, axis=1) * pscale
    return yb * _silu(gb)


def _layer_norm_v(vc, ln_g):
    vm = jnp.mean(vc, axis=-1, keepdims=True)
    d = vc - vm
    vv = jnp.mean(d * d, axis=-1, keepdims=True)
    return d * lax.rsqrt(vv + LN_EPS) * ln_g


def _chunk_gate(vn, uc, gc, wm_ref, bm_ref):
    n_rows = vn.shape[0]
    vnb = vn.astype(BF16)
    outs = []
    for j in range(n_rows // CHUNK):
        rs = slice(j * CHUNK, (j + 1) * CHUNK)
        mix = [_dot(wm_ref[g], vnb[rs, g * GC:(g + 1) * GC]) + bm_ref[g] for g in range(N_GROUPS_C)]
        outs.append(jnp.concatenate(mix, axis=1))
    mix = outs[0] if len(outs) == 1 else jnp.concatenate(outs, axis=0)
    return uc * mix * _silu(gc)


def _bc_prompt_kernel(ub_ref, gb_ref, uc_ref, vc_ref, gc_ref, pw_ref, ps_ref, lng_ref, wm_ref, bm_ref,
                      o_ref, halo_ref, *, tt):
    j = pl.program_id(1)

    @pl.when(j == 0)
    def _():
        halo_ref[...] = jnp.zeros_like(halo_ref)

    u = ub_ref[...]
    ext = jnp.concatenate([halo_ref[...], u], axis=0)
    halo_ref[...] = u[tt - HALO:, :]
    sums = _window_sums(ext)
    pos = j * tt + lax.broadcasted_iota(jnp.int32, (tt, 1), 0)
    pooled = []
    for g, win in enumerate(POOL_WINDOWS):
        ls = slice(g * POOL_GC, (g + 1) * POOL_GC)
        cnt = jnp.minimum(pos + 1, win).astype(F32)
        pooled.append(sums[g][HALO:, ls] / cnt - u[:, ls])
    o_ref[:, :D_B] = _pool_gate(pooled, gb_ref[...], pw_ref, ps_ref[...]).astype(o_ref.dtype)
    vn = _layer_norm_v(vc_ref[...], lng_ref[...])
    o_ref[:, D_B:] = _chunk_gate(vn, uc_ref[...], gc_ref[...], wm_ref, bm_ref).astype(o_ref.dtype)


def _bc_prompt(proj, n_seq, seq_len, pw, pscale, ln_g, wm, bm, tt=256):
    n_t = seq_len // tt

    def rmap(col):
        return lambda i, j: (i * n_t + j, col)

    wspec = pl.BlockSpec((4, 128, 128), lambda i, j: (0, 0, 0))
    vspec = pl.BlockSpec((1, D_B), lambda i, j: (0, 0))
    return pl.pallas_call(
        functools.partial(_bc_prompt_kernel, tt=tt),
        grid=(n_seq, n_t),
        in_specs=[pl.BlockSpec((tt, D_B), rmap(P_UB // D_B)),
                  pl.BlockSpec((tt, D_B), rmap(P_GB // D_B)),
                  pl.BlockSpec((tt, D_B), rmap(P_UC // D_B)),
                  pl.BlockSpec((tt, D_B), rmap(P_VC // D_B)),
                  pl.BlockSpec((tt, D_B), rmap(P_GC // D_B)),
                  wspec, vspec, vspec, wspec, wspec],
        out_specs=pl.BlockSpec((tt, D_B + D_C), rmap(0)),
        out_shape=jax.ShapeDtypeStruct((n_seq * seq_len, D_B + D_C), BF16),
        scratch_shapes=[pltpu.VMEM((HALO, D_B), F32)],
        compiler_params=pltpu.CompilerParams(
            dimension_semantics=("parallel", "arbitrary"), vmem_limit_bytes=VMEM_LIMIT),
        name="bc_prompt",
    )(proj, proj, proj, proj, proj, pw, pscale, ln_g, wm, bm)


def _bc_sample_kernel(buf_ref, ub_ref, gb_ref, uc_ref, vc_ref, gc_ref, pw_ref, ps_ref, lng_ref,
                      wm_ref, bm_ref, o_ref, vn_ref, *, nb, t_len):
    u3 = ub_ref[...]
    ext = jnp.concatenate([buf_ref[...], u3], axis=1)
    per = HALO + t_len
    sums = _window_sums(ext.reshape(nb * per, D_B))
    u = u3.reshape(nb * t_len, D_B)
    pooled = []
    for g, win in enumerate(POOL_WINDOWS):
        ls = slice(g * POOL_GC, (g + 1) * POOL_GC)
        s3 = sums[g].reshape(nb, per, D_B)[:, HALO:, ls].reshape(nb * t_len, POOL_GC)
        cnt = float(min(PAST_LEN + 1, win))
        pooled.append(s3 / cnt - u[:, ls])
    rows = nb * t_len
    gb = gb_ref[...].reshape(rows, D_B)
    o_ref[:, :D_B] = _pool_gate(pooled, gb, pw_ref, ps_ref[...]).astype(o_ref.dtype)
    vn = _layer_norm_v(vc_ref[...].reshape(rows, D_C), lng_ref[...])
    vn_ref[...] = vn
    uc = uc_ref[...].reshape(rows, D_C)
    gc = gc_ref[...].reshape(rows, D_C)
    o_ref[:, D_B:] = _chunk_gate(vn, uc, gc, wm_ref, bm_ref).astype(o_ref.dtype)


def _bc_sample(proj3, seq0, n_seq, t_len, buf16, pw, pscale, ln_g, wm, bm):
    nb = CHUNK // t_len
    sb0 = seq0 // nb

    def rmap(col):
        return lambda i: (sb0 + i, 0, col)

    wspec = pl.BlockSpec((4, 128, 128), lambda i: (0, 0, 0))
    vspec = pl.BlockSpec((1, D_B), lambda i: (0, 0))
    rows = nb * t_len
    return pl.pallas_call(
        functools.partial(_bc_sample_kernel, nb=nb, t_len=t_len),
        grid=(n_seq // nb,),
        in_specs=[pl.BlockSpec((nb, HALO, D_B), lambda i: (i, 0, 0)),
                  pl.BlockSpec((nb, t_len, D_B), rmap(P_UB // D_B)),
                  pl.BlockSpec((nb, t_len, D_B), rmap(P_GB // D_B)),
                  pl.BlockSpec((nb, t_len, D_B), rmap(P_UC // D_B)),
                  pl.BlockSpec((nb, t_len, D_B), rmap(P_VC // D_B)),
                  pl.BlockSpec((nb, t_len, D_B), rmap(P_GC // D_B)),
                  wspec, vspec, vspec, wspec, wspec],
        out_specs=[pl.BlockSpec((rows, D_B + D_C), lambda i: (i, 0)),
                   pl.BlockSpec((rows, D_C), lambda i: (i, 0))],
        out_shape=[jax.ShapeDtypeStruct((n_seq * t_len, D_B + D_C), BF16),
                   jax.ShapeDtypeStruct((n_seq * t_len, D_C), F32)],
        compiler_params=pltpu.CompilerParams(
            dimension_semantics=("parallel",), vmem_limit_bytes=VMEM_LIMIT),
        name="bc_sample",
    )(buf16, proj3, proj3, proj3, proj3, proj3, pw, pscale, ln_g, wm, bm)


def _permute_cols(a):
    return jnp.concatenate([a[..., :3 * D_A], a[..., SHIFT_W:], a[..., 3 * D_A:SHIFT_W]], axis=-1)


def kernel(x_prompt, x_sample, state_shift, state_wkv, state_pool, norm_g, final_norm_g, w_in,
           shift_mu, w0, w_up, a0, a_up, k_k, k_a, r_k, lnx_g, lnx_b, pool_w, pool_scale,
           gmlp_ln_g, gmlp_ws, gmlp_b, w_out):
    bp, seq, _ = x_prompt.shape
    bs, dseq, _ = x_sample.shape
    n_p = bp * seq
    n_s = bs * dseq
    x_all = jnp.concatenate([x_prompt.reshape(n_p, D_MODEL), x_sample.reshape(n_s, D_MODEL)], axis=0)

    w_in_p = _permute_cols(w_in).astype(BF16)
    w_out_a = w_out[:, :D_A].astype(BF16)
    w_out_b = w_out[:, D_A:].astype(BF16)
    mu_l = jnp.pad(shift_mu[:, 3 * D_A:], ((0, 0), (0, D_A - 2 * LORA)))
    prm = jnp.stack([shift_mu[:, :D_A], shift_mu[:, D_A:2 * D_A], shift_mu[:, 2 * D_A:3 * D_A],
                     w0, a0, k_k, k_a, r_k.reshape(DEPTH, D_A), lnx_g, lnx_b, mu_l], axis=1)
    prm = jnp.pad(prm, ((0, 0), (0, 16 - prm.shape[1]), (0, 0)))
    w2 = jnp.concatenate([w_up, a_up], axis=1)
    pw = pool_w.astype(BF16)
    tril = jnp.tril(jnp.ones((CHUNK, CHUNK), F32))
    wm_p = (gmlp_ws * tril).astype(BF16)
    bm_p = jnp.broadcast_to(gmlp_b[:, :, :, None], (DEPTH, N_GROUPS_C, CHUNK, GC))
    nb_s = CHUNK // dseq
    eye_b = jnp.eye(nb_s, dtype=F32)
    ws_small = gmlp_ws[:, :, :dseq, :dseq] * tril[:dseq, :dseq]
    wm_s = jnp.einsum('ab,lgts->lgatbs', eye_b, ws_small).reshape(DEPTH, N_GROUPS_C, CHUNK, CHUNK)
    wm_s = wm_s.astype(BF16)
    bm_s = jnp.broadcast_to(jnp.tile(gmlp_b[:, :, :dseq], (1, 1, nb_s))[:, :, :, None],
                            (DEPTH, N_GROUPS_C, CHUNK, GC))

    zsh_main = jnp.zeros((3, bp, 1, D_A), F32)
    zsh_lora = jnp.zeros((bp, 1, PAIR), F32)
    zs0 = jnp.zeros((bp, N_PAIRS, PAIR, HEAD_A), F32)
    ssh_main = state_shift[:, :, :3 * D_A].reshape(DEPTH, bs, 3, 1, D_A).transpose(0, 2, 1, 3, 4)
    ssh_lora = state_shift[:, :, 3 * D_A:].reshape(DEPTH, bs, 1, PAIR)
    s0_all = state_wkv.reshape(DEPTH, bs, N_PAIRS, PAIR, HEAD_A)
    buf16 = jnp.pad(state_pool, ((0, 0), (0, 0), (HALO - POOL_BUF, 0), (0, 0)))

    p_shift, p_wkv, p_pool, s_shift, s_wkv, s_pool, s_v = [], [], [], [], [], [], []
    for l in range(DEPTH):
        proj = _inproj(x_all, norm_g[l][None], w_in_p[l])
        ya_p, wk_p = _wkv(proj, 0, bp, seq, ROWS, 1, zsh_main, zsh_lora, zs0, prm[l], w2[l], False)
        ya_s, wk_s = _wkv(proj, n_p, bs, dseq, dseq, ROWS // dseq, ssh_main[l], ssh_lora[l],
                          s0_all[l], prm[l], w2[l], True)
        cb_p = _bc_prompt(proj, bp, seq, pw[l], pool_scale[l][None], gmlp_ln_g[l][None], wm_p[l], bm_p[l])
        proj3 = proj.reshape((n_p + n_s) // dseq, dseq, D_INP)
        cb_s, vn_s = _bc_sample(proj3, n_p // dseq, bs, dseq, buf16[l], pw[l], pool_scale[l][None],
                                gmlp_ln_g[l][None], wm_s[l], bm_s[l])
        cat_a = jnp.concatenate([ya_p, ya_s], axis=0)
        cat_b = jnp.concatenate([cb_p, cb_s], axis=0)
        final = l == DEPTH - 1
        x_all = _outproj(cat_a, cat_b, w_out_a[l], w_out_b[l], x_all, final_norm_g[None], final)

        pp = proj[:n_p].reshape(bp, seq, D_INP)
        sp = proj[n_p:].reshape(bs, dseq, D_INP)
        unperm = lambda a: jnp.concatenate([a[..., :3 * D_A], a[..., P_LORA:]], axis=-1)
        p_shift.append(unperm(pp[:, -1]))
        s_shift.append(unperm(sp[:, -1]))
        p_wkv.append(wk_p.reshape(bp, N_HEADS_A, HEAD_A, HEAD_A))
        s_wkv.append(wk_s.reshape(bs, N_HEADS_A, HEAD_A, HEAD_A))
        p_pool.append(pp[:, seq - POOL_BUF:, P_UB:P_UB + D_B])
        s_pool.append(jnp.concatenate([state_pool[l], sp[:, :, P_UB:P_UB + D_B]], axis=1)[:, -POOL_BUF:])
        s_v.append(vn_s.reshape(bs, dseq, D_C))

    y_prompt = x_all[:n_p].reshape(bp, seq, D_MODEL)
    y_sample = x_all[n_p:].reshape(bs, dseq, D_MODEL)
    return (y_prompt, y_sample, jnp.stack(p_shift), jnp.stack(p_wkv), jnp.stack(p_pool),
            jnp.stack(s_shift), jnp.stack(s_wkv), jnp.stack(s_pool), jnp.stack(s_v))
```

```python
import functools

import jax
import jax.numpy as jnp
import numpy as np
from jax import lax
from jax.experimental import pallas as pl
from jax.experimental.pallas import tpu as pltpu

F32 = jnp.float32
BF16 = jnp.bfloat16

D_MODEL = 2048
DEPTH = 4
PAST_LEN = 16384
D_A = 1024
HEAD_A = 64
N_HEADS_A = 16
LORA = 64
D_B = 512
POOL_WINDOWS = (2, 4, 8, 16)
POOL_GC = 128
POOL_BUF = 15
D_C = 512
N_GROUPS_C = 4
GC = 128
CHUNK = 128
SHIFT_W = 3 * D_A + 2 * LORA
EPS = 1e-6
GN_EPS = HEAD_A * 1e-5
LN_EPS = 1e-5

P_R, P_K, P_V, P_GA = 0, 1024, 2048, 3072
P_UB, P_GB, P_UC, P_VC, P_GC = 4096, 4608, 5120, 5632, 6144
P_LORA = 6656
D_INP = 6784

PAIR = 128
N_PAIRS = D_A // PAIR
ROWS = 64
PAIR_GROUP = 8

VMEM_LIMIT = 52 * 1024 * 1024
HI = lax.Precision.HIGHEST


def _dot(a, b, prec=None):
    return jnp.dot(a, b, precision=prec, preferred_element_type=F32)


def _dot_nt(a, b, prec=None):
    return lax.dot_general(a, b, (((1,), (1,)), ((), ())), precision=prec,
                           preferred_element_type=F32)


def _dot_tn(a, b, prec=None):
    return lax.dot_general(a, b, (((0,), (0,)), ((), ())), precision=prec,
                           preferred_element_type=F32)


def _split(x, n):
    pieces = []
    rem = x
    for i in range(n):
        hi = rem.astype(BF16)
        pieces.append(hi)
        if i + 1 < n:
            rem = rem - hi.astype(F32)
    return pieces


_CONTRACT = {"nn": (1, 0), "nt": (1, 1), "tn": (0, 0)}


def _mm(a, b, mode="nn", pa=1, pb=1):
    ca, cb = _CONTRACT[mode]
    sa, sb = _split(a, pa), _split(b, pb)
    terms = [(i, j) for i in range(pa) for j in range(pb) if i + j < max(pa, pb)]
    lhs = jnp.concatenate([sa[i] for i, _ in terms], axis=ca) if len(terms) > 1 else sa[0]
    rhs = jnp.concatenate([sb[j] for _, j in terms], axis=cb) if len(terms) > 1 else sb[0]
    return lax.dot_general(lhs, rhs, (((ca,), (cb,)), ((), ())), preferred_element_type=F32)


PREC_LORA = (1, 1)
PREC_CUM = (1, 3)
PREC_SEG = (2, 1)
PREC_G = (1, 1)
PREC_INV = (1, 1)
PREC_X2 = (1, 1)
PREC_WU = (1, 1)
PREC_US = (1, 1)
PREC_UPD = (1, 1)
PREC_Y = (1, 1)


def _inproj_kernel(x_ref, g_ref, w_ref, o_ref, h_ref):
    @pl.when(pl.program_id(1) == 0)
    def _():
        x = x_ref[...]
        ms = jnp.mean(x * x, axis=-1, keepdims=True)
        h_ref[...] = ((x * lax.rsqrt(ms + EPS)) * g_ref[...]).astype(BF16)

    o_ref[...] = _dot(h_ref[...], w_ref[...])


def _inproj(x, g, w, tm=1024, tn=512):
    m = x.shape[0]
    return pl.pallas_call(
        _inproj_kernel,
        grid=(m // tm, pl.cdiv(D_INP, tn)),
        in_specs=[pl.BlockSpec((tm, D_MODEL), lambda i, j: (i, 0)),
                  pl.BlockSpec((1, D_MODEL), lambda i, j: (0, 0)),
                  pl.BlockSpec((D_MODEL, tn), lambda i, j: (0, j))],
        out_specs=pl.BlockSpec((tm, tn), lambda i, j: (i, j)),
        out_shape=jax.ShapeDtypeStruct((m, D_INP), F32),
        scratch_shapes=[pltpu.VMEM((tm, D_MODEL), BF16)],
        compiler_params=pltpu.CompilerParams(
            dimension_semantics=("parallel", "arbitrary"), vmem_limit_bytes=VMEM_LIMIT),
        name="inproj",
    )(x, g, w)


def _outproj_kernel(ca_ref, cb_ref, wa_ref, wb_ref, x_ref, g_ref, o_ref, *, final):
    y = _dot(ca_ref[...], wa_ref[...]) + _dot(cb_ref[...], wb_ref[...])
    out = x_ref[...] + y
    if final:
        ms = jnp.mean(out * out, axis=-1, keepdims=True)
        out = (out * lax.rsqrt(ms + EPS)) * g_ref[...]
    o_ref[...] = out


def _outproj(cat_a, cat_b, w_a, w_b, x, g, final, tm=512):
    m = x.shape[0]
    half = D_MODEL // 2
    return pl.pallas_call(
        functools.partial(_outproj_kernel, final=final),
        grid=(m // tm,),
        in_specs=[pl.BlockSpec((tm, half), lambda i: (i, 0)),
                  pl.BlockSpec((tm, half), lambda i: (i, 0)),
                  pl.BlockSpec((half, D_MODEL), lambda i: (0, 0)),
                  pl.BlockSpec((half, D_MODEL), lambda i: (0, 0)),
                  pl.BlockSpec((tm, D_MODEL), lambda i: (i, 0)),
                  pl.BlockSpec((1, D_MODEL), lambda i: (0, 0))],
        out_specs=pl.BlockSpec((tm, D_MODEL), lambda i: (i, 0)),
        out_shape=jax.ShapeDtypeStruct((m, D_MODEL), F32),
        compiler_params=pltpu.CompilerParams(
            dimension_semantics=("parallel",), vmem_limit_bytes=VMEM_LIMIT),
        name="outproj",
    )(cat_a, cat_b, w_a, w_b, x, g)


PR_MU_R, PR_MU_K, PR_MU_V, PR_W0, PR_A0, PR_KK, PR_KA, PR_RK, PR_LNG, PR_LNB, PR_MU_L = range(11)


def _softplus(z):
    return jnp.maximum(z, 0.0) + jnp.log(1.0 + jnp.exp(-jnp.abs(z)))


def _sigmoid(z):
    return 1.0 / (1.0 + jnp.exp(-z))


def _silu(z):
    return z * _sigmoid(z)


def _wkv_kernel(r_ref, k_ref, v_ref, g_ref, l_ref, sh_ref, shl_ref, s0_ref, prm_ref, w2_ref,
                o_ref, sout_ref,
                st_ref, prev_ref, prevl_ref,
                *, C, NB, n_chunks, has_state):
    c_idx = pl.program_id(1)
    rows = ROWS

    @pl.when(c_idx == 0)
    def _():
        prev_ref[...] = sh_ref[...]
        prevl_ref[...] = shl_ref[...]
        if has_state:
            lane = lax.broadcasted_iota(jnp.int32, (PAIR, PAIR), 1)
            row = lax.broadcasted_iota(jnp.int32, (PAIR, PAIR), 0)
            same = (lane // HEAD_A) == (row // HEAD_A)
            for b in range(NB):
                for p in range(N_PAIRS):
                    s = s0_ref[b, p]
                    s2 = jnp.concatenate([s, s], axis=1)
                    st_ref[p * NB + b] = jnp.where(same, s2, 0.0)
        else:
            st_ref[...] = jnp.zeros_like(st_ref)

    prm = prm_ref[...]

    def prow(i):
        return prm[i:i + 1, :]

    row_id = lax.broadcasted_iota(jnp.int32, (rows, 1), 0)
    first = (row_id % C) == 0

    def shift_mix(x, prev, mu):
        rolled = pltpu.roll(x, 1, 0)
        prev_rows = jnp.concatenate(
            [jnp.broadcast_to(prev[b], (C, x.shape[1])) for b in range(NB)], axis=0)
        shifted = jnp.where(first, prev_rows, rolled)
        return x + (shifted - x) * mu

    r_raw = r_ref[...]
    k_raw = k_ref[...]
    v_raw = v_ref[...]
    l_raw = l_ref[...]
    r = shift_mix(r_raw, prev_ref[0], prow(PR_MU_R))
    k = shift_mix(k_raw, prev_ref[1], prow(PR_MU_K))
    v = shift_mix(v_raw, prev_ref[2], prow(PR_MU_V))
    xl = shift_mix(l_raw, prevl_ref[...], prow(PR_MU_L)[:, :PAIR])

    if n_chunks > 1:
        prev_ref[0, 0] = r_raw[rows - 1:rows, :]
        prev_ref[1, 0] = k_raw[rows - 1:rows, :]
        prev_ref[2, 0] = v_raw[rows - 1:rows, :]
        prevl_ref[0] = l_raw[rows - 1:rows, :]

    lane128 = lax.broadcasted_iota(jnp.int32, (rows, PAIR), 1)
    lo_half = lane128 < HEAD_A
    w2 = w2_ref[...]
    lora_w = _mm(jnp.where(lo_half, jnp.tanh(xl), 0.0), w2, "nn", *PREC_LORA)
    lora_a = _mm(jnp.where(lo_half, 0.0, xl), w2, "nn", *PREC_LORA)
    w_log = -_softplus(-(prow(PR_W0) + lora_w)) - 0.5
    dec = jnp.exp(w_log)
    a = _sigmoid(prow(PR_A0) + lora_a)
    kk_raw = k * prow(PR_KK)
    k2 = k * (1.0 + (a - 1.0) * prow(PR_KA))
    rkr = r * k2 * prow(PR_RK)

    ri = lax.broadcasted_iota(jnp.int32, (rows, rows), 0)
    ci = lax.broadcasted_iota(jnp.int32, (rows, rows), 1)
    tri = jnp.where(((ri // C) == (ci // C)) & (ri >= ci), 1.0, 0.0)
    cum = _mm(tri, -dec, "nn", *PREC_CUM)
    tot = jnp.concatenate(
        [jnp.broadcast_to(cum[(b + 1) * C - 1:(b + 1) * C, :], (C, D_A)) for b in range(NB)], axis=0)
    g_incl = jnp.exp(cum)
    g_excl = jnp.exp(cum + dec)
    g_inv = jnp.exp(-cum)
    g_rest = jnp.exp(tot - cum)
    g_tot = jnp.exp(tot)

    sg = _silu(g_ref[...])

    pr = lax.broadcasted_iota(jnp.int32, (PAIR, PAIR), 0)
    pc = lax.broadcasted_iota(jnp.int32, (PAIR, PAIR), 1)
    seg = jnp.where((pr // HEAD_A) == (pc // HEAD_A), 1.0, 0.0)
    r2 = lax.broadcasted_iota(jnp.int32, (2 * rows, 2 * rows), 0)
    c2 = lax.broadcasted_iota(jnp.int32, (2 * rows, 2 * rows), 1)
    same_blk = (r2 // C) == (c2 // C)
    m_strict = same_blk & (r2 > c2)
    m_incl = same_blk & (r2 >= c2)
    m_incl2 = jnp.concatenate([m_incl, m_incl], axis=1)
    eye = jnp.where(r2 == c2, 1.0, 0.0)
    same_head = (pr // HEAD_A) == (pc // HEAD_A)

    def stack(x):
        return jnp.concatenate([jnp.where(lo_half, x, 0.0), jnp.where(lo_half, 0.0, x)], axis=0)

    def seq_rows(x, b):
        if NB == 1:
            return x
        return jnp.concatenate([x[b * C:(b + 1) * C], x[rows + b * C:rows + (b + 1) * C]], axis=0)

    def unseq_rows(pieces):
        if NB == 1:
            return pieces[0]
        return jnp.concatenate([pc_[:C] for pc_ in pieces] + [pc_[C:] for pc_ in pieces], axis=0)

    n_rounds = int(np.log2(C))
    n2 = 2 * rows
    nb2 = 2 * C

    def run_group(pairs):
        sls = [slice(p * PAIR, (p + 1) * PAIR) for p in pairs]
        s0 = [_mm(jnp.concatenate([kk_raw[:, sl] * kk_raw[:, sl], rkr[:, sl]], axis=0), seg, "nn", *PREC_SEG)
              for sl in sls]
        kkp = [kk_raw[:, sl] * lax.rsqrt(jnp.maximum(q[:rows], 1e-12)) for sl, q in zip(sls, s0)]
        bon = [q[rows:] for q in s0]
        kka = [kq * a[:, sl] for kq, sl in zip(kkp, sls)]
        at_s = [stack(-kq * g_excl[:, sl]) for kq, sl in zip(kkp, sls)]
        rt_s = [stack(r[:, sl] * g_incl[:, sl]) for sl in sls]
        kh_s = [stack(k2[:, sl] * g_inv[:, sl]) for sl in sls]
        bh_s = [stack(q * g_inv[:, sl]) for q, sl in zip(kka, sls)]
        kg_s = [stack(k2[:, sl] * g_rest[:, sl]) for sl in sls]
        bg_s = [stack(q * g_rest[:, sl]) for q, sl in zip(kka, sls)]
        v_s = [stack(v[:, sl]) for sl in sls]

        gm = [_mm(jnp.concatenate([x, y_], axis=0), jnp.concatenate([z, w_], axis=0), "nt", *PREC_G)
              for x, y_, z, w_ in zip(at_s, rt_s, kh_s, bh_s)]
        a_ak = [jnp.where(m_strict, q[:n2, :n2], 0.0) for q in gm]
        a_ab = [jnp.where(m_strict, q[:n2, n2:], 0.0) for q in gm]
        a_r = [jnp.where(m_incl2, q[n2:, :], 0.0) for q in gm]

        x2 = [_mm(q, vs, "nn", *PREC_X2) for q, vs in zip(a_ak, v_s)]
        pw = [_mm(q, q, "nn", *PREC_INV) for q in a_ab]
        t_inv = [eye + q for q in a_ab]
        for rd in range(1, n_rounds):
            if rd < n_rounds - 1:
                res = [_mm(jnp.concatenate([q, t], axis=0), q, "nn", *PREC_INV) for q, t in zip(pw, t_inv)]
                t_inv = [t + q[n2:] for t, q in zip(t_inv, res)]
                pw = [q[:n2] for q in res]
            else:
                t_inv = [t + _mm(t, q, "nn", *PREC_INV) for q, t in zip(pw, t_inv)]

        wu = [_mm(t, jnp.concatenate([x, y_], axis=1), "nn", *PREC_WU)
              for t, x, y_ in zip(t_inv, at_s, x2)]
        w_t = [q[:, :PAIR] for q in wu]
        u_t = [q[:, PAIR:] for q in wu]

        s_old = [[st_ref[p * NB + b] for b in range(NB)] for p in pairs]
        us = [[_mm(jnp.concatenate([seq_rows(w_t[i], b), seq_rows(rt_s[i], b)], axis=0),
                   s_old[i][b], "nt", *PREC_US) for b in range(NB)] for i in range(len(pairs))]
        u_b = [[us[i][b][:nb2] + seq_rows(u_t[i], b) for b in range(NB)] for i in range(len(pairs))]
        for i, p in enumerate(pairs):
            for b in range(NB):
                upd = _mm(jnp.concatenate([seq_rows(v_s[i], b), u_b[i][b]], axis=0),
                          jnp.concatenate([seq_rows(kg_s[i], b), seq_rows(bg_s[i], b)], axis=0),
                          "tn", *PREC_UPD)
                st_ref[p * NB + b] = s_old[i][b] * g_tot[b * C:b * C + 1, sls[i]] + upd
        y2 = [unseq_rows([us[i][b][nb2:] for b in range(NB)])
              + _mm(a_r[i], jnp.concatenate([v_s[i], unseq_rows(u_b[i])], axis=0), "nn", *PREC_Y)
              for i in range(len(pairs))]
        y = [q[:rows] + q[rows:] for q in y2]

        ym = [_mm(q, seg, "nn", *PREC_SEG) * (1.0 / HEAD_A) for q in y]
        yc = [q - m for q, m in zip(y, ym)]
        yv = [_mm(q * q, seg, "nn", *PREC_SEG) * (1.0 / HEAD_A) for q in yc]
        for i, sl in enumerate(sls):
            yn = yc[i] * lax.rsqrt(yv[i] + GN_EPS) * prow(PR_LNG)[:, sl] + prow(PR_LNB)[:, sl]
            o_ref[:, sl] = ((yn + bon[i] * v[:, sl]) * sg[:, sl]).astype(o_ref.dtype)

    for g0 in range(0, N_PAIRS, PAIR_GROUP):
        run_group(list(range(g0, g0 + PAIR_GROUP)))

    @pl.when(c_idx == n_chunks - 1)
    def _():
        for b in range(NB):
            for p in range(N_PAIRS):
                sm = jnp.where(same_head, st_ref[p * NB + b], 0.0)
                sout_ref[b, p] = sm[:, :HEAD_A] + sm[:, HEAD_A:]


def _wkv(proj, row0, n_seq, seq_len, C, NB, sh_main, sh_lora, s0, prm, w2, has_state):
    n_chunks = seq_len // C
    n_groups = n_seq // NB
    rb0 = row0 // ROWS
    assert NB * C == ROWS and row0 % ROWS == 0

    def rmap(col):
        return lambda i, c: (rb0 + i * n_chunks + c, col)

    kern = functools.partial(_wkv_kernel, C=C, NB=NB, n_chunks=n_chunks, has_state=has_state)
    out, s_out = pl.pallas_call(
        kern,
        grid=(n_groups, n_chunks),
        in_specs=[pl.BlockSpec((ROWS, D_A), rmap(P_R // D_A)),
                  pl.BlockSpec((ROWS, D_A), rmap(P_K // D_A)),
                  pl.BlockSpec((ROWS, D_A), rmap(P_V // D_A)),
                  pl.BlockSpec((ROWS, D_A), rmap(P_GA // D_A)),
                  pl.BlockSpec((ROWS, PAIR), rmap(P_LORA // PAIR)),
                  pl.BlockSpec((3, NB, 1, D_A), lambda i, c: (0, i, 0, 0)),
                  pl.BlockSpec((NB, 1, PAIR), lambda i, c: (i, 0, 0)),
                  pl.BlockSpec((NB, N_PAIRS, PAIR, HEAD_A), lambda i, c: (i, 0, 0, 0)),
                  pl.BlockSpec((16, D_A), lambda i, c: (0, 0)),
                  pl.BlockSpec((PAIR, D_A), lambda i, c: (0, 0))],
        out_specs=[pl.BlockSpec((ROWS, D_A), lambda i, c: (i * n_chunks + c, 0)),
                   pl.BlockSpec((NB, N_PAIRS, PAIR, HEAD_A), lambda i, c: (i, 0, 0, 0))],
        out_shape=[jax.ShapeDtypeStruct((n_seq * seq_len, D_A), BF16),
                   jax.ShapeDtypeStruct((n_seq, N_PAIRS, PAIR, HEAD_A), F32)],
        scratch_shapes=[pltpu.VMEM((N_PAIRS * NB, PAIR, PAIR), F32),
                        pltpu.VMEM((3, NB, 1, D_A), F32),
                        pltpu.VMEM((NB, 1, PAIR), F32)],
        compiler_params=pltpu.CompilerParams(
            dimension_semantics=("parallel", "arbitrary"), vmem_limit_bytes=VMEM_LIMIT),
        name="wkv_c%d" % C,
    )(proj, proj, proj, proj, proj, sh_main, sh_lora, s0, prm, w2)
    return out, s_out


HALO = 16


def _window_sums(ext):
    w2 = ext + pltpu.roll(ext, 1, 0)
    w4 = w2 + pltpu.roll(w2, 2, 0)
    w8 = w4 + pltpu.roll(w4, 4, 0)
    w16 = w8 + pltpu.roll(w8, 8, 0)
    return (w2, w4, w8, w16)


def _pool_gate(pooled_groups, gb, pw_ref, pscale):
    mixed = [_dot(pg.astype(BF16), pw_ref[g]) for g, pg in enumerate(pooled_groups)]
    yb = jnp.concatenate(mixed, axis=1) * pscale
    return yb * _silu(gb)


def _layer_norm_v(vc, ln_g):
    vm = jnp.mean(vc, axis=-1, keepdims=True)
    d = vc - vm
    vv = jnp.mean(d * d, axis=-1, keepdims=True)
    return d * lax.rsqrt(vv + LN_EPS) * ln_g


def _chunk_gate(vn, uc, gc, wm_ref, bm_ref):
    n_rows = vn.shape[0]
    vnb = vn.astype(BF16)
    outs = []
    for j in range(n_rows // CHUNK):
        rs = slice(j * CHUNK, (j + 1) * CHUNK)
        mix = [_dot(wm_ref[g], vnb[rs, g * GC:(g + 1) * GC]) + bm_ref[g] for g in range(N_GROUPS_C)]
        outs.append(jnp.concatenate(mix, axis=1))
    mix = outs[0] if len(outs) == 1 else jnp.concatenate(outs, axis=0)
    return uc * mix * _silu(gc)


def _bc_prompt_kernel(ub_ref, gb_ref, uc_ref, vc_ref, gc_ref, pw_ref, ps_ref, lng_ref, wm_ref, bm_ref,
                      o_ref, halo_ref, *, tt):
    j = pl.program_id(1)

    @pl.when(j == 0)
    def _():
        halo_ref[...] = jnp.zeros_like(halo_ref)

    u = ub_ref[...]
    ext = jnp.concatenate([halo_ref[...], u], axis=0)
    halo_ref[...] = u[tt - HALO:, :]
    sums = _window_sums(ext)
    pos = j * tt + lax.broadcasted_iota(jnp.int32, (tt, 1), 0)
    pooled = []
    for g, win in enumerate(POOL_WINDOWS):
        ls = slice(g * POOL_GC, (g + 1) * POOL_GC)
        cnt = jnp.minimum(pos + 1, win).astype(F32)
        pooled.append(sums[g][HALO:, ls] / cnt - u[:, ls])
    o_ref[:, :D_B] = _pool_gate(pooled, gb_ref[...], pw_ref, ps_ref[...]).astype(o_ref.dtype)
    vn = _layer_norm_v(vc_ref[...], lng_ref[...])
    o_ref[:, D_B:] = _chunk_gate(vn, uc_ref[...], gc_ref[...], wm_ref, bm_ref).astype(o_ref.dtype)


def _bc_prompt(proj, n_seq, seq_len, pw, pscale, ln_g, wm, bm, tt=256):
    n_t = seq_len // tt

    def rmap(col):
        return lambda i, j: (i * n_t + j, col)

    wspec = pl.BlockSpec((4, 128, 128), lambda i, j: (0, 0, 0))
    vspec = pl.BlockSpec((1, D_B), lambda i, j: (0, 0))
    return pl.pallas_call(
        functools.partial(_bc_prompt_kernel, tt=tt),
        grid=(n_seq, n_t),
        in_specs=[pl.BlockSpec((tt, D_B), rmap(P_UB // D_B)),
                  pl.BlockSpec((tt, D_B), rmap(P_GB // D_B)),
                  pl.BlockSpec((tt, D_B), rmap(P_UC // D_B)),
                  pl.BlockSpec((tt, D_B), rmap(P_VC // D_B)),
                  pl.BlockSpec((tt, D_B), rmap(P_GC // D_B)),
                  wspec, vspec, vspec, wspec, wspec],
        out_specs=pl.BlockSpec((tt, D_B + D_C), rmap(0)),
        out_shape=jax.ShapeDtypeStruct((n_seq * seq_len, D_B + D_C), BF16),
        scratch_shapes=[pltpu.VMEM((HALO, D_B), F32)],
        compiler_params=pltpu.CompilerParams(
            dimension_semantics=("parallel", "arbitrary"), vmem_limit_bytes=VMEM_LIMIT),
        name="bc_prompt",
    )(proj, proj, proj, proj, proj, pw, pscale, ln_g, wm, bm)


def _bc_sample_kernel(buf_ref, ub_ref, gb_ref, uc_ref, vc_ref, gc_ref, pw_ref, ps_ref, lng_ref,
                      wm_ref, bm_ref, o_ref, vn_ref, *, nb, t_len):
    u3 = ub_ref[...]
    ext = jnp.concatenate([buf_ref[...], u3], axis=1)
    per = HALO + t_len
    sums = _window_sums(ext.reshape(nb * per, D_B))
    u = u3.reshape(nb * t_len, D_B)
    pooled = []
    for g, win in enumerate(POOL_WINDOWS):
        ls = slice(g * POOL_GC, (g + 1) * POOL_GC)
        s3 = sums[g].reshape(nb, per, D_B)[:, HALO:, ls].reshape(nb * t_len, POOL_GC)
        cnt = float(min(PAST_LEN + 1, win))
        pooled.append(s3 / cnt - u[:, ls])
    rows = nb * t_len
    gb = gb_ref[...].reshape(rows, D_B)
    o_ref[:, :D_B] = _pool_gate(pooled, gb, pw_ref, ps_ref[...]).astype(o_ref.dtype)
    vn = _layer_norm_v(vc_ref[...].reshape(rows, D_C), lng_ref[...])
    vn_ref[...] = vn
    uc = uc_ref[...].reshape(rows, D_C)
    gc = gc_ref[...].reshape(rows, D_C)
    o_ref[:, D_B:] = _chunk_gate(vn, uc, gc, wm_ref, bm_ref).astype(o_ref.dtype)


def _bc_sample(proj3, seq0, n_seq, t_len, buf16, pw, pscale, ln_g, wm, bm):
    nb = CHUNK // t_len
    sb0 = seq0 // nb

    def rmap(col):
        return lambda i: (sb0 + i, 0, col)

    wspec = pl.BlockSpec((4, 128, 128), lambda i: (0, 0, 0))
    vspec = pl.BlockSpec((1, D_B), lambda i: (0, 0))
    rows = nb * t_len
    return pl.pallas_call(
        functools.partial(_bc_sample_kernel, nb=nb, t_len=t_len),
        grid=(n_seq // nb,),
        in_specs=[pl.BlockSpec((nb, HALO, D_B), lambda i: (i, 0, 0)),
                  pl.BlockSpec((nb, t_len, D_B), rmap(P_UB // D_B)),
                  pl.BlockSpec((nb, t_len, D_B), rmap(P_GB // D_B)),
                  pl.BlockSpec((nb, t_len, D_B), rmap(P_UC // D_B)),
                  pl.BlockSpec((nb, t_len, D_B), rmap(P_VC // D_B)),
                  pl.BlockSpec((nb, t_len, D_B), rmap(P_GC // D_B)),
                  wspec, vspec, vspec, wspec, wspec],
        out_specs=[pl.BlockSpec((rows, D_B + D_C), lambda i: (i, 0)),
                   pl.BlockSpec((rows, D_C), lambda i: (i, 0))],
        out_shape=[jax.ShapeDtypeStruct((n_seq * t_len, D_B + D_C), BF16),
                   jax.ShapeDtypeStruct((n_seq * t_len, D_C), F32)],
        compiler_params=pltpu.CompilerParams(
            dimension_semantics=("parallel",), vmem_limit_bytes=VMEM_LIMIT),
        name="bc_sample",
    )(buf16, proj3, proj3, proj3, proj3, proj3, pw, pscale, ln_g, wm, bm)


N_COLBLK = D_INP // PAIR
LORA_BLK = 3 * D_A // PAIR


def _wprep_kernel(w_ref, o_ref):
    o_ref[...] = w_ref[...].astype(BF16)


def _prep_w_in(w_in):
    def omap(l, j):
        return (l, 0, jnp.where(j < LORA_BLK, j, jnp.where(j == LORA_BLK, N_COLBLK - 1, j - 1)))

    return pl.pallas_call(
        _wprep_kernel,
        grid=(DEPTH, N_COLBLK),
        in_specs=[pl.BlockSpec((1, D_MODEL, PAIR), lambda l, j: (l, 0, j))],
        out_specs=pl.BlockSpec((1, D_MODEL, PAIR), omap),
        out_shape=jax.ShapeDtypeStruct((DEPTH, D_MODEL, D_INP), BF16),
        compiler_params=pltpu.CompilerParams(dimension_semantics=("parallel", "parallel")),
        name="wprep",
    )(w_in)


def _unperm(a):
    return jnp.concatenate([a[..., :3 * D_A], a[..., P_LORA:]], axis=-1)


def kernel(x_prompt, x_sample, state_shift, state_wkv, state_pool, norm_g, final_norm_g, w_in,
           shift_mu, w0, w_up, a0, a_up, k_k, k_a, r_k, lnx_g, lnx_b, pool_w, pool_scale,
           gmlp_ln_g, gmlp_ws, gmlp_b, w_out):
    bp, seq, _ = x_prompt.shape
    bs, dseq, _ = x_sample.shape
    n_p = bp * seq
    n_s = bs * dseq
    xp = x_prompt.reshape(n_p, D_MODEL)
    xs = x_sample.reshape(n_s, D_MODEL)

    w_in_p = _prep_w_in(w_in)
    w_out_a = w_out[:, :D_A].astype(BF16)
    w_out_b = w_out[:, D_A:].astype(BF16)
    mu_l = jnp.pad(shift_mu[:, 3 * D_A:], ((0, 0), (0, D_A - 2 * LORA)))
    prm = jnp.stack([shift_mu[:, :D_A], shift_mu[:, D_A:2 * D_A], shift_mu[:, 2 * D_A:3 * D_A],
                     w0, a0, k_k, k_a, r_k.reshape(DEPTH, D_A), lnx_g, lnx_b, mu_l], axis=1)
    prm = jnp.pad(prm, ((0, 0), (0, 16 - prm.shape[1]), (0, 0)))
    w2 = jnp.concatenate([w_up, a_up], axis=1)
    pw = pool_w.astype(BF16)
    tril = jnp.tril(jnp.ones((CHUNK, CHUNK), F32))
    wm_p = (gmlp_ws * tril).astype(BF16)
    bm_p = jnp.broadcast_to(gmlp_b[:, :, :, None], (DEPTH, N_GROUPS_C, CHUNK, GC))
    nb_s = CHUNK // dseq
    eye_b = jnp.eye(nb_s, dtype=F32)
    ws_small = gmlp_ws[:, :, :dseq, :dseq] * tril[:dseq, :dseq]
    wm_s = jnp.einsum('ab,lgts->lgatbs', eye_b, ws_small).reshape(DEPTH, N_GROUPS_C, CHUNK, CHUNK)
    wm_s = wm_s.astype(BF16)
    bm_s = jnp.broadcast_to(jnp.tile(gmlp_b[:, :, :dseq], (1, 1, nb_s))[:, :, :, None],
                            (DEPTH, N_GROUPS_C, CHUNK, GC))

    zsh_main = jnp.zeros((3, bp, 1, D_A), F32)
    zsh_lora = jnp.zeros((bp, 1, PAIR), F32)
    zs0 = jnp.zeros((bp, N_PAIRS, PAIR, HEAD_A), F32)
    ssh_main = state_shift[:, :, :3 * D_A].reshape(DEPTH, bs, 3, 1, D_A).transpose(0, 2, 1, 3, 4)
    ssh_lora = state_shift[:, :, 3 * D_A:].reshape(DEPTH, bs, 1, PAIR)
    s0_all = state_wkv.reshape(DEPTH, bs, N_PAIRS, PAIR, HEAD_A)
    buf16 = jnp.pad(state_pool, ((0, 0), (0, 0), (HALO - POOL_BUF, 0), (0, 0)))

    p_shift, p_wkv, p_pool, s_shift, s_wkv, s_pool, s_v = [], [], [], [], [], [], []
    for l in range(DEPTH):
        final = l == DEPTH - 1
        g_l = norm_g[l][None]
        bc_w = (pw[l], pool_scale[l][None], gmlp_ln_g[l][None])

        proj_p = _inproj(xp, g_l, w_in_p[l])
        ya_p, wk_p = _wkv(proj_p, 0, bp, seq, ROWS, 1, zsh_main, zsh_lora, zs0, prm[l], w2[l], False)
        cb_p = _bc_prompt(proj_p, bp, seq, *bc_w, wm_p[l], bm_p[l])
        xp = _outproj(ya_p, cb_p, w_out_a[l], w_out_b[l], xp, final_norm_g[None], final)

        proj_s = _inproj(xs, g_l, w_in_p[l])
        ya_s, wk_s = _wkv(proj_s, 0, bs, dseq, dseq, ROWS // dseq, ssh_main[l], ssh_lora[l],
                          s0_all[l], prm[l], w2[l], True)
        cb_s, vn_s = _bc_sample(proj_s.reshape(bs, dseq, D_INP), 0, bs, dseq, buf16[l], *bc_w,
                                wm_s[l], bm_s[l])
        xs = _outproj(ya_s, cb_s, w_out_a[l], w_out_b[l], xs, final_norm_g[None], final)

        p_shift.append(_unperm(proj_p[seq - 1::seq]))
        s_shift.append(_unperm(proj_s[dseq - 1::dseq]))
        p_wkv.append(wk_p.reshape(bp, N_HEADS_A, HEAD_A, HEAD_A))
        s_wkv.append(wk_s.reshape(bs, N_HEADS_A, HEAD_A, HEAD_A))
        p_pool.append(jnp.stack([proj_p[(b + 1) * seq - POOL_BUF:(b + 1) * seq, P_UB:P_UB + D_B]
                                 for b in range(bp)]))
        ub_s = proj_s[:, P_UB:P_UB + D_B].reshape(bs, dseq, D_B)
        s_pool.append(jnp.concatenate([state_pool[l], ub_s], axis=1)[:, -POOL_BUF:])
        s_v.append(vn_s.reshape(bs, dseq, D_C))

    y_prompt = xp.reshape(bp, seq, D_MODEL)
    y_sample = xs.reshape(bs, dseq, D_MODEL)
    return (y_prompt, y_sample, jnp.stack(p_shift), jnp.stack(p_wkv), jnp.stack(p_pool),
            jnp.stack(s_shift), jnp.stack(s_wkv), jnp.stack(s_pool), jnp.stack(s_v))
```

```python
import functools

import jax
import jax.numpy as jnp
import numpy as np
from jax import lax
from jax.experimental import pallas as pl
from jax.experimental.pallas import tpu as pltpu

F32 = jnp.float32
BF16 = jnp.bfloat16

D_MODEL = 2048
DEPTH = 4
PAST_LEN = 16384
D_A = 1024
HEAD_A = 64
N_HEADS_A = 16
LORA = 64
D_B = 512
POOL_WINDOWS = (2, 4, 8, 16)
POOL_GC = 128
POOL_BUF = 15
D_C = 512
N_GROUPS_C = 4
GC = 128
CHUNK = 128
SHIFT_W = 3 * D_A + 2 * LORA
EPS = 1e-6
GN_EPS = HEAD_A * 1e-5
LN_EPS = 1e-5

P_R, P_K, P_V, P_GA = 0, 1024, 2048, 3072
P_UB, P_GB, P_UC, P_VC, P_GC = 4096, 4608, 5120, 5632, 6144
P_LORA = 6656
D_INP = 6784

PAIR = 128
N_PAIRS = D_A // PAIR
ROWS = 64
PAIR_GROUP = 8

VMEM_LIMIT = 52 * 1024 * 1024
HI = lax.Precision.HIGHEST


def _dot(a, b, prec=None):
    return jnp.dot(a, b, precision=prec, preferred_element_type=F32)


def _dot_nt(a, b, prec=None):
    return lax.dot_general(a, b, (((1,), (1,)), ((), ())), precision=prec,
                           preferred_element_type=F32)


def _dot_tn(a, b, prec=None):
    return lax.dot_general(a, b, (((0,), (0,)), ((), ())), precision=prec,
                           preferred_element_type=F32)


def _split(x, n):
    pieces = []
    rem = x
    for i in range(n):
        hi = rem.astype(BF16)
        pieces.append(hi)
        if i + 1 < n:
            rem = rem - hi.astype(F32)
    return pieces


_CONTRACT = {"nn": (1, 0), "nt": (1, 1), "tn": (0, 0)}


def _mm(a, b, mode="nn", pa=1, pb=1):
    ca, cb = _CONTRACT[mode]
    sa, sb = _split(a, pa), _split(b, pb)
    terms = [(i, j) for i in range(pa) for j in range(pb) if i + j < max(pa, pb)]
    lhs = jnp.concatenate([sa[i] for i, _ in terms], axis=ca) if len(terms) > 1 else sa[0]
    rhs = jnp.concatenate([sb[j] for _, j in terms], axis=cb) if len(terms) > 1 else sb[0]
    return lax.dot_general(lhs, rhs, (((ca,), (cb,)), ((), ())), preferred_element_type=F32)


PREC_LORA = (1, 1)
PREC_CUM = (1, 3)
PREC_SEG = (2, 1)
PREC_G = (1, 1)
PREC_INV = (1, 1)
PREC_X2 = (1, 1)
PREC_WU = (1, 1)
PREC_US = (1, 1)
PREC_UPD = (1, 1)
PREC_Y = (1, 1)


def _inproj_kernel(x_ref, g_ref, w_ref, o_ref, h_ref):
    @pl.when(pl.program_id(1) == 0)
    def _():
        x = x_ref[...]
        ms = jnp.mean(x * x, axis=-1, keepdims=True)
        h_ref[...] = ((x * lax.rsqrt(ms + EPS)) * g_ref[...]).astype(BF16)

    o_ref[...] = _dot(h_ref[...], w_ref[0])


def _inproj(x, g, w, layer, tm=1024, tn=512):
    m = x.shape[0]
    return pl.pallas_call(
        _inproj_kernel,
        grid=(m // tm, pl.cdiv(D_INP, tn)),
        in_specs=[pl.BlockSpec((tm, D_MODEL), lambda i, j: (i, 0)),
                  pl.BlockSpec((1, D_MODEL), lambda i, j: (0, 0)),
                  pl.BlockSpec((1, D_MODEL, tn), lambda i, j: (layer, 0, j))],
        out_specs=pl.BlockSpec((tm, tn), lambda i, j: (i, j)),
        out_shape=jax.ShapeDtypeStruct((m, D_INP), F32),
        scratch_shapes=[pltpu.VMEM((tm, D_MODEL), BF16)],
        compiler_params=pltpu.CompilerParams(
            dimension_semantics=("parallel", "arbitrary"), vmem_limit_bytes=VMEM_LIMIT),
        name="inproj",
    )(x, g, w)


def _outproj_kernel(ca_ref, cb_ref, wa_ref, wb_ref, x_ref, g_ref, o_ref, *, final):
    y = _dot(ca_ref[...], wa_ref[0]) + _dot(cb_ref[...], wb_ref[0])
    out = x_ref[...] + y
    if final:
        ms = jnp.mean(out * out, axis=-1, keepdims=True)
        out = (out * lax.rsqrt(ms + EPS)) * g_ref[...]
    o_ref[...] = out


def _outproj(cat_a, cat_b, w, layer, x, g, final, tm=512):
    m = x.shape[0]
    half = D_MODEL // 2
    return pl.pallas_call(
        functools.partial(_outproj_kernel, final=final),
        grid=(m // tm,),
        in_specs=[pl.BlockSpec((tm, half), lambda i: (i, 0)),
                  pl.BlockSpec((tm, half), lambda i: (i, 0)),
                  pl.BlockSpec((1, half, D_MODEL), lambda i: (layer, 0, 0)),
                  pl.BlockSpec((1, half, D_MODEL), lambda i: (layer, 1, 0)),
                  pl.BlockSpec((tm, D_MODEL), lambda i: (i, 0)),
                  pl.BlockSpec((1, D_MODEL), lambda i: (0, 0))],
        out_specs=pl.BlockSpec((tm, D_MODEL), lambda i: (i, 0)),
        out_shape=jax.ShapeDtypeStruct((m, D_MODEL), F32),
        compiler_params=pltpu.CompilerParams(
            dimension_semantics=("parallel",), vmem_limit_bytes=VMEM_LIMIT),
        name="outproj",
    )(cat_a, cat_b, w, w, x, g)


PR_MU_R, PR_MU_K, PR_MU_V, PR_W0, PR_A0, PR_KK, PR_KA, PR_RK, PR_LNG, PR_LNB, PR_MU_L = range(11)


def _softplus(z):
    return jnp.maximum(z, 0.0) + jnp.log(1.0 + jnp.exp(-jnp.abs(z)))


def _sigmoid(z):
    return 1.0 / (1.0 + jnp.exp(-z))


def _silu(z):
    return z * _sigmoid(z)


def _wkv_kernel(r_ref, k_ref, v_ref, g_ref, l_ref, sh_ref, shl_ref, s0_ref, prm_ref, w2_ref,
                o_ref, sout_ref, shm_out_ref, shl_out_ref,
                st_ref, prev_ref, prevl_ref,
                *, C, NB, n_chunks, has_state):
    c_idx = pl.program_id(1)
    rows = ROWS

    @pl.when(c_idx == 0)
    def _():
        prev_ref[...] = sh_ref[0]
        prevl_ref[...] = shl_ref[0]
        if has_state:
            lane = lax.broadcasted_iota(jnp.int32, (PAIR, PAIR), 1)
            row = lax.broadcasted_iota(jnp.int32, (PAIR, PAIR), 0)
            same = (lane // HEAD_A) == (row // HEAD_A)
            for b in range(NB):
                for p in range(N_PAIRS):
                    s = s0_ref[0, b, p]
                    s2 = jnp.concatenate([s, s], axis=1)
                    st_ref[p * NB + b] = jnp.where(same, s2, 0.0)
        else:
            st_ref[...] = jnp.zeros_like(st_ref)

    prm = prm_ref[0]

    def prow(i):
        return prm[i:i + 1, :]

    row_id = lax.broadcasted_iota(jnp.int32, (rows, 1), 0)
    first = (row_id % C) == 0

    def shift_mix(x, prev, mu):
        rolled = pltpu.roll(x, 1, 0)
        prev_rows = jnp.concatenate(
            [jnp.broadcast_to(prev[b], (C, x.shape[1])) for b in range(NB)], axis=0)
        shifted = jnp.where(first, prev_rows, rolled)
        return x + (shifted - x) * mu

    r_raw = r_ref[...]
    k_raw = k_ref[...]
    v_raw = v_ref[...]
    l_raw = l_ref[...]
    r = shift_mix(r_raw, prev_ref[0], prow(PR_MU_R))
    k = shift_mix(k_raw, prev_ref[1], prow(PR_MU_K))
    v = shift_mix(v_raw, prev_ref[2], prow(PR_MU_V))
    xl = shift_mix(l_raw, prevl_ref[...], prow(PR_MU_L)[:, :PAIR])

    if n_chunks > 1:
        prev_ref[0, 0] = r_raw[rows - 1:rows, :]
        prev_ref[1, 0] = k_raw[rows - 1:rows, :]
        prev_ref[2, 0] = v_raw[rows - 1:rows, :]
        prevl_ref[0] = l_raw[rows - 1:rows, :]

    for b in range(NB):
        last = (b + 1) * C - 1
        shm_out_ref[0, b] = r_raw[last:last + 1, :]
        shm_out_ref[1, b] = k_raw[last:last + 1, :]
        shm_out_ref[2, b] = v_raw[last:last + 1, :]
        shl_out_ref[b] = l_raw[last:last + 1, :]

    lane128 = lax.broadcasted_iota(jnp.int32, (rows, PAIR), 1)
    lo_half = lane128 < HEAD_A
    w2 = w2_ref[0]
    lora_w = _mm(jnp.where(lo_half, jnp.tanh(xl), 0.0), w2, "nn", *PREC_LORA)
    lora_a = _mm(jnp.where(lo_half, 0.0, xl), w2, "nn", *PREC_LORA)
    w_log = -_softplus(-(prow(PR_W0) + lora_w)) - 0.5
    dec = jnp.exp(w_log)
    a = _sigmoid(prow(PR_A0) + lora_a)
    kk_raw = k * prow(PR_KK)
    k2 = k * (1.0 + (a - 1.0) * prow(PR_KA))
    rkr = r * k2 * prow(PR_RK)

    ri = lax.broadcasted_iota(jnp.int32, (rows, rows), 0)
    ci = lax.broadcasted_iota(jnp.int32, (rows, rows), 1)
    tri = jnp.where(((ri // C) == (ci // C)) & (ri >= ci), 1.0, 0.0)
    cum = _mm(tri, -dec, "nn", *PREC_CUM)
    tot = jnp.concatenate(
        [jnp.broadcast_to(cum[(b + 1) * C - 1:(b + 1) * C, :], (C, D_A)) for b in range(NB)], axis=0)
    g_incl = jnp.exp(cum)
    g_excl = jnp.exp(cum + dec)
    g_inv = jnp.exp(-cum)
    g_rest = jnp.exp(tot - cum)
    g_tot = jnp.exp(tot)

    sg = _silu(g_ref[...])

    pr = lax.broadcasted_iota(jnp.int32, (PAIR, PAIR), 0)
    pc = lax.broadcasted_iota(jnp.int32, (PAIR, PAIR), 1)
    seg = jnp.where((pr // HEAD_A) == (pc // HEAD_A), 1.0, 0.0)
    r2 = lax.broadcasted_iota(jnp.int32, (2 * rows, 2 * rows), 0)
    c2 = lax.broadcasted_iota(jnp.int32, (2 * rows, 2 * rows), 1)
    same_blk = (r2 // C) == (c2 // C)
    m_strict = same_blk & (r2 > c2)
    m_incl = same_blk & (r2 >= c2)
    m_incl2 = jnp.concatenate([m_incl, m_incl], axis=1)
    eye = jnp.where(r2 == c2, 1.0, 0.0)
    same_head = (pr // HEAD_A) == (pc // HEAD_A)

    def stack(x):
        xb = x.astype(BF16)
        zb = jnp.zeros_like(xb)
        return jnp.concatenate([jnp.where(lo_half, xb, zb), jnp.where(lo_half, zb, xb)], axis=0)

    def bmask(m, xb):
        return jnp.where(m, xb, jnp.zeros_like(xb))

    def seq_rows(x, b):
        if NB == 1:
            return x
        return jnp.concatenate([x[b * C:(b + 1) * C], x[rows + b * C:rows + (b + 1) * C]], axis=0)

    def unseq_rows(pieces):
        if NB == 1:
            return pieces[0]
        return jnp.concatenate([pc_[:C] for pc_ in pieces] + [pc_[C:] for pc_ in pieces], axis=0)

    n_rounds = int(np.log2(C))
    n2 = 2 * rows
    nb2 = 2 * C

    def run_group(pairs):
        sls = [slice(p * PAIR, (p + 1) * PAIR) for p in pairs]
        s0 = [_mm(jnp.concatenate([kk_raw[:, sl] * kk_raw[:, sl], rkr[:, sl]], axis=0), seg, "nn", *PREC_SEG)
              for sl in sls]
        kkp = [kk_raw[:, sl] * lax.rsqrt(jnp.maximum(q[:rows], 1e-12)) for sl, q in zip(sls, s0)]
        bon = [q[rows:] for q in s0]
        kka = [kq * a[:, sl] for kq, sl in zip(kkp, sls)]
        at_s = [stack(-kq * g_excl[:, sl]) for kq, sl in zip(kkp, sls)]
        rt_s = [stack(r[:, sl] * g_incl[:, sl]) for sl in sls]
        kh_s = [stack(k2[:, sl] * g_inv[:, sl]) for sl in sls]
        bh_s = [stack(q * g_inv[:, sl]) for q, sl in zip(kka, sls)]
        kg_s = [stack(k2[:, sl] * g_rest[:, sl]) for sl in sls]
        bg_s = [stack(q * g_rest[:, sl]) for q, sl in zip(kka, sls)]
        v_s = [stack(v[:, sl]) for sl in sls]

        gm = [_mm(jnp.concatenate([x, y_], axis=0), jnp.concatenate([z, w_], axis=0), "nt", *PREC_G)
              for x, y_, z, w_ in zip(at_s, rt_s, kh_s, bh_s)]
        gmb = [q.astype(BF16) for q in gm]
        a_ak = [bmask(m_strict, q[:n2, :n2]) for q in gmb]
        a_ab = [jnp.where(m_strict, q[:n2, n2:], 0.0) for q in gm]
        a_r = [bmask(m_incl2, q[n2:, :]) for q in gmb]

        x2 = [_mm(q, vs, "nn", *PREC_X2) for q, vs in zip(a_ak, v_s)]
        pwb = [bmask(m_strict, q[:n2, n2:]) for q in gmb]
        pw = [_mm(q, q, "nn", *PREC_INV) for q in pwb]
        t_inv = [eye + q for q in a_ab]
        for rd in range(1, n_rounds):
            pwb = [q.astype(BF16) for q in pw]
            tb = [t.astype(BF16) for t in t_inv]
            if rd < n_rounds - 1:
                res = [_mm(jnp.concatenate([q, t], axis=0), q, "nn", *PREC_INV) for q, t in zip(pwb, tb)]
                t_inv = [t + q[n2:] for t, q in zip(t_inv, res)]
                pw = [q[:n2] for q in res]
            else:
                t_inv = [t + _mm(t_, q, "nn", *PREC_INV) for q, t, t_ in zip(pwb, t_inv, tb)]

        wu = [_mm(t, jnp.concatenate([x, y_.astype(BF16)], axis=1), "nn", *PREC_WU)
              for t, x, y_ in zip(t_inv, at_s, x2)]
        w_t = [q[:, :PAIR].astype(BF16) for q in wu]
        u_t = [q[:, PAIR:] for q in wu]

        s_old = [[st_ref[p * NB + b] for b in range(NB)] for p in pairs]
        us = [[_mm(jnp.concatenate([seq_rows(w_t[i], b), seq_rows(rt_s[i], b)], axis=0),
                   s_old[i][b], "nt", *PREC_US) for b in range(NB)] for i in range(len(pairs))]
        u_b = [[(us[i][b][:nb2] + seq_rows(u_t[i], b)).astype(BF16) for b in range(NB)]
               for i in range(len(pairs))]
        for i, p in enumerate(pairs):
            for b in range(NB):
                upd = _mm(jnp.concatenate([seq_rows(v_s[i], b), u_b[i][b]], axis=0),
                          jnp.concatenate([seq_rows(kg_s[i], b), seq_rows(bg_s[i], b)], axis=0),
                          "tn", *PREC_UPD)
                st_ref[p * NB + b] = s_old[i][b] * g_tot[b * C:b * C + 1, sls[i]] + upd
        y2 = [unseq_rows([us[i][b][nb2:] for b in range(NB)])
              + _mm(a_r[i], jnp.concatenate([v_s[i], unseq_rows(u_b[i])], axis=0), "nn", *PREC_Y)
              for i in range(len(pairs))]
        y = [q[:rows] + q[rows:] for q in y2]

        ym = [_mm(q, seg, "nn", *PREC_SEG) * (1.0 / HEAD_A) for q in y]
        yc = [q - m for q, m in zip(y, ym)]
        yv = [_mm(q * q, seg, "nn", *PREC_SEG) * (1.0 / HEAD_A) for q in yc]
        for i, sl in enumerate(sls):
            yn = yc[i] * lax.rsqrt(yv[i] + GN_EPS) * prow(PR_LNG)[:, sl] + prow(PR_LNB)[:, sl]
            o_ref[:, sl] = ((yn + bon[i] * v[:, sl]) * sg[:, sl]).astype(o_ref.dtype)

    for g0 in range(0, N_PAIRS, PAIR_GROUP):
        run_group(list(range(g0, g0 + PAIR_GROUP)))

    @pl.when(c_idx == n_chunks - 1)
    def _():
        for b in range(NB):
            for p in range(N_PAIRS):
                sm = jnp.where(same_head, st_ref[p * NB + b], 0.0)
                sout_ref[0, b, p] = sm[:, :HEAD_A] + sm[:, HEAD_A:]


def _wkv(proj, n_seq, seq_len, C, NB, sh_main, sh_lora, s0, prm, w2, layer, has_state):
    n_chunks = seq_len // C
    n_groups = n_seq // NB
    assert NB * C == ROWS
    sl = layer if has_state else 0

    def rmap(col):
        return lambda i, c: (i * n_chunks + c, col)

    state_spec = pl.BlockSpec((1, NB, N_PAIRS, PAIR, HEAD_A), lambda i, c: (sl, i, 0, 0, 0))
    kern = functools.partial(_wkv_kernel, C=C, NB=NB, n_chunks=n_chunks, has_state=has_state)
    return pl.pallas_call(
        kern,
        grid=(n_groups, n_chunks),
        in_specs=[pl.BlockSpec((ROWS, D_A), rmap(P_R // D_A)),
                  pl.BlockSpec((ROWS, D_A), rmap(P_K // D_A)),
                  pl.BlockSpec((ROWS, D_A), rmap(P_V // D_A)),
                  pl.BlockSpec((ROWS, D_A), rmap(P_GA // D_A)),
                  pl.BlockSpec((ROWS, PAIR), rmap(P_LORA // PAIR)),
                  pl.BlockSpec((1, 3, NB, 1, D_A), lambda i, c: (sl, 0, i, 0, 0)),
                  pl.BlockSpec((1, NB, 1, PAIR), lambda i, c: (sl, i, 0, 0)),
                  state_spec,
                  pl.BlockSpec((1, 16, D_A), lambda i, c: (layer, 0, 0)),
                  pl.BlockSpec((1, PAIR, D_A), lambda i, c: (layer, 0, 0))],
        out_specs=[pl.BlockSpec((ROWS, D_A), lambda i, c: (i * n_chunks + c, 0)),
                   state_spec,
                   pl.BlockSpec((3, NB, 1, D_A), lambda i, c: (0, i, 0, 0)),
                   pl.BlockSpec((NB, 1, PAIR), lambda i, c: (i, 0, 0))],
        out_shape=[jax.ShapeDtypeStruct((n_seq * seq_len, D_A), BF16),
                   jax.ShapeDtypeStruct(s0.shape, F32),
                   jax.ShapeDtypeStruct((3, n_seq, 1, D_A), F32),
                   jax.ShapeDtypeStruct((n_seq, 1, PAIR), F32)],
        scratch_shapes=[pltpu.VMEM((N_PAIRS * NB, PAIR, PAIR), F32),
                        pltpu.VMEM((3, NB, 1, D_A), F32),
                        pltpu.VMEM((NB, 1, PAIR), F32)],
        input_output_aliases={7: 1} if has_state else {},
        compiler_params=pltpu.CompilerParams(
            dimension_semantics=("parallel", "arbitrary"), vmem_limit_bytes=VMEM_LIMIT),
        name="wkv_c%d" % C,
    )(proj, proj, proj, proj, proj, sh_main, sh_lora, s0, prm, w2)


HALO = 16


def _window_sums(ext):
    w2 = ext + pltpu.roll(ext, 1, 0)
    w4 = w2 + pltpu.roll(w2, 2, 0)
    w8 = w4 + pltpu.roll(w4, 4, 0)
    w16 = w8 + pltpu.roll(w8, 8, 0)
    return (w2, w4, w8, w16)


def _pool_gate(pooled_groups, gb, pw_ref, pscale):
    mixed = [_dot(pg.astype(BF16), pw_ref[g]) for g, pg in enumerate(pooled_groups)]
    yb = jnp.concatenate(mixed, axis=1) * pscale
    return yb * _silu(gb)


def _layer_norm_v(vc, ln_g):
    vm = jnp.mean(vc, axis=-1, keepdims=True)
    d = vc - vm
    vv = jnp.mean(d * d, axis=-1, keepdims=True)
    return d * lax.rsqrt(vv + LN_EPS) * ln_g


def _chunk_gate(vn, uc, gc, wm_ref, bm_ref):
    n_rows = vn.shape[0]
    vnb = vn.astype(BF16)
    outs = []
    for j in range(n_rows // CHUNK):
        rs = slice(j * CHUNK, (j + 1) * CHUNK)
        mix = [_dot(wm_ref[g], vnb[rs, g * GC:(g + 1) * GC]) + bm_ref[g] for g in range(N_GROUPS_C)]
        outs.append(jnp.concatenate(mix, axis=1))
    mix = outs[0] if len(outs) == 1 else jnp.concatenate(outs, axis=0)
    return uc * mix * _silu(gc)


def _bc_prompt_kernel(ub_ref, gb_ref, uc_ref, vc_ref, gc_ref, pw_ref, ps_ref, lng_ref, wm_ref, bm_ref,
                      o_ref, halo_ref, *, tt):
    j = pl.program_id(1)

    @pl.when(j == 0)
    def _():
        halo_ref[...] = jnp.zeros_like(halo_ref)

    u = ub_ref[...]
    ext = jnp.concatenate([halo_ref[...], u], axis=0)
    halo_ref[...] = u[tt - HALO:, :]
    sums = _window_sums(ext)
    pos = j * tt + lax.broadcasted_iota(jnp.int32, (tt, 1), 0)
    pooled = []
    for g, win in enumerate(POOL_WINDOWS):
        ls = slice(g * POOL_GC, (g + 1) * POOL_GC)
        cnt = jnp.minimum(pos + 1, win).astype(F32)
        pooled.append(sums[g][HALO:, ls] / cnt - u[:, ls])
    o_ref[:, :D_B] = _pool_gate(pooled, gb_ref[...], pw_ref, ps_ref[...]).astype(o_ref.dtype)
    vn = _layer_norm_v(vc_ref[...], lng_ref[...])
    o_ref[:, D_B:] = _chunk_gate(vn, uc_ref[...], gc_ref[...], wm_ref, bm_ref).astype(o_ref.dtype)


def _bc_prompt(proj, n_seq, seq_len, pw, pscale, ln_g, wm, bm, tt=256):
    n_t = seq_len // tt

    def rmap(col):
        return lambda i, j: (i * n_t + j, col)

    wspec = pl.BlockSpec((4, 128, 128), lambda i, j: (0, 0, 0))
    vspec = pl.BlockSpec((1, D_B), lambda i, j: (0, 0))
    return pl.pallas_call(
        functools.partial(_bc_prompt_kernel, tt=tt),
        grid=(n_seq, n_t),
        in_specs=[pl.BlockSpec((tt, D_B), rmap(P_UB // D_B)),
                  pl.BlockSpec((tt, D_B), rmap(P_GB // D_B)),
                  pl.BlockSpec((tt, D_B), rmap(P_UC // D_B)),
                  pl.BlockSpec((tt, D_B), rmap(P_VC // D_B)),
                  pl.BlockSpec((tt, D_B), rmap(P_GC // D_B)),
                  wspec, vspec, vspec, wspec, wspec],
        out_specs=pl.BlockSpec((tt, D_B + D_C), rmap(0)),
        out_shape=jax.ShapeDtypeStruct((n_seq * seq_len, D_B + D_C), BF16),
        scratch_shapes=[pltpu.VMEM((HALO, D_B), F32)],
        compiler_params=pltpu.CompilerParams(
            dimension_semantics=("parallel", "arbitrary"), vmem_limit_bytes=VMEM_LIMIT),
        name="bc_prompt",
    )(proj, proj, proj, proj, proj, pw, pscale, ln_g, wm, bm)


def _bc_sample_kernel(buf_ref, ub_ref, gb_ref, uc_ref, vc_ref, gc_ref, pw_ref, ps_ref, lng_ref,
                      wm_ref, bm_ref, o_ref, vn_ref, *, nb, t_len):
    u3 = ub_ref[...]
    ext = jnp.concatenate([buf_ref[...], u3], axis=1)
    per = HALO + t_len
    sums = _window_sums(ext.reshape(nb * per, D_B))
    u = u3.reshape(nb * t_len, D_B)
    pooled = []
    for g, win in enumerate(POOL_WINDOWS):
        ls = slice(g * POOL_GC, (g + 1) * POOL_GC)
        s3 = sums[g].reshape(nb, per, D_B)[:, HALO:, ls].reshape(nb * t_len, POOL_GC)
        cnt = float(min(PAST_LEN + 1, win))
        pooled.append(s3 / cnt - u[:, ls])
    rows = nb * t_len
    gb = gb_ref[...].reshape(rows, D_B)
    o_ref[:, :D_B] = _pool_gate(pooled, gb, pw_ref, ps_ref[...]).astype(o_ref.dtype)
    vn = _layer_norm_v(vc_ref[...].reshape(rows, D_C), lng_ref[...])
    vn_ref[...] = vn
    uc = uc_ref[...].reshape(rows, D_C)
    gc = gc_ref[...].reshape(rows, D_C)
    o_ref[:, D_B:] = _chunk_gate(vn, uc, gc, wm_ref, bm_ref).astype(o_ref.dtype)


def _bc_sample(proj3, seq0, n_seq, t_len, buf16, pw, pscale, ln_g, wm, bm):
    nb = CHUNK // t_len
    sb0 = seq0 // nb

    def rmap(col):
        return lambda i: (sb0 + i, 0, col)

    wspec = pl.BlockSpec((4, 128, 128), lambda i: (0, 0, 0))
    vspec = pl.BlockSpec((1, D_B), lambda i: (0, 0))
    rows = nb * t_len
    return pl.pallas_call(
        functools.partial(_bc_sample_kernel, nb=nb, t_len=t_len),
        grid=(n_seq // nb,),
        in_specs=[pl.BlockSpec((nb, HALO, D_B), lambda i: (i, 0, 0)),
                  pl.BlockSpec((nb, t_len, D_B), rmap(P_UB // D_B)),
                  pl.BlockSpec((nb, t_len, D_B), rmap(P_GB // D_B)),
                  pl.BlockSpec((nb, t_len, D_B), rmap(P_UC // D_B)),
                  pl.BlockSpec((nb, t_len, D_B), rmap(P_VC // D_B)),
                  pl.BlockSpec((nb, t_len, D_B), rmap(P_GC // D_B)),
                  wspec, vspec, vspec, wspec, wspec],
        out_specs=[pl.BlockSpec((rows, D_B + D_C), lambda i: (i, 0)),
                   pl.BlockSpec((rows, D_C), lambda i: (i, 0))],
        out_shape=[jax.ShapeDtypeStruct((n_seq * t_len, D_B + D_C), BF16),
                   jax.ShapeDtypeStruct((n_seq * t_len, D_C), F32)],
        compiler_params=pltpu.CompilerParams(
            dimension_semantics=("parallel",), vmem_limit_bytes=VMEM_LIMIT),
        name="bc_sample",
    )(buf16, proj3, proj3, proj3, proj3, proj3, pw, pscale, ln_g, wm, bm)


WPREP_ROWS = 256


def _wprep_kernel(w_ref, o_ref):
    o_ref[0, :, :3 * D_A] = w_ref[0, :, :3 * D_A].astype(BF16)
    o_ref[0, :, 3 * D_A:P_LORA] = w_ref[0, :, SHIFT_W:].astype(BF16)
    o_ref[0, :, P_LORA:] = w_ref[0, :, 3 * D_A:SHIFT_W].astype(BF16)


def _prep_w_in(w_in):
    spec = pl.BlockSpec((1, WPREP_ROWS, D_INP), lambda l, i: (l, i, 0))
    return pl.pallas_call(
        _wprep_kernel,
        grid=(DEPTH, D_MODEL // WPREP_ROWS),
        in_specs=[spec],
        out_specs=spec,
        out_shape=jax.ShapeDtypeStruct((DEPTH, D_MODEL, D_INP), BF16),
        compiler_params=pltpu.CompilerParams(
            dimension_semantics=("parallel", "parallel"), vmem_limit_bytes=VMEM_LIMIT),
        name="wprep",
    )(w_in)


def kernel(x_prompt, x_sample, state_shift, state_wkv, state_pool, norm_g, final_norm_g, w_in,
           shift_mu, w0, w_up, a0, a_up, k_k, k_a, r_k, lnx_g, lnx_b, pool_w, pool_scale,
           gmlp_ln_g, gmlp_ws, gmlp_b, w_out):
    bp, seq, _ = x_prompt.shape
    bs, dseq, _ = x_sample.shape
    n_p = bp * seq
    n_s = bs * dseq
    xp = x_prompt.reshape(n_p, D_MODEL)
    xs = x_sample.reshape(n_s, D_MODEL)

    w_in_p = _prep_w_in(w_in)
    w_out_h = w_out.astype(BF16)
    mu_l = jnp.pad(shift_mu[:, 3 * D_A:], ((0, 0), (0, D_A - 2 * LORA)))
    prm = jnp.stack([shift_mu[:, :D_A], shift_mu[:, D_A:2 * D_A], shift_mu[:, 2 * D_A:3 * D_A],
                     w0, a0, k_k, k_a, r_k.reshape(DEPTH, D_A), lnx_g, lnx_b, mu_l], axis=1)
    prm = jnp.pad(prm, ((0, 0), (0, 16 - prm.shape[1]), (0, 0)))
    w2 = jnp.concatenate([w_up, a_up], axis=1)
    pw = pool_w.astype(BF16)
    tril = jnp.tril(jnp.ones((CHUNK, CHUNK), F32))
    wm_p = (gmlp_ws * tril).astype(BF16)
    bm_p = jnp.broadcast_to(gmlp_b[:, :, :, None], (DEPTH, N_GROUPS_C, CHUNK, GC))
    nb_s = CHUNK // dseq
    eye_b = jnp.eye(nb_s, dtype=F32)
    ws_small = gmlp_ws[:, :, :dseq, :dseq] * tril[:dseq, :dseq]
    wm_s = jnp.einsum('ab,lgts->lgatbs', eye_b, ws_small).reshape(DEPTH, N_GROUPS_C, CHUNK, CHUNK)
    wm_s = wm_s.astype(BF16)
    bm_s = jnp.broadcast_to(jnp.tile(gmlp_b[:, :, :dseq], (1, 1, nb_s))[:, :, :, None],
                            (DEPTH, N_GROUPS_C, CHUNK, GC))

    zsh_main = jnp.zeros((1, 3, bp, 1, D_A), F32)
    zsh_lora = jnp.zeros((1, bp, 1, PAIR), F32)
    zs0 = jnp.zeros((1, bp, N_PAIRS, PAIR, HEAD_A), F32)
    ssh_main = state_shift[:, :, :3 * D_A].reshape(DEPTH, bs, 3, 1, D_A).transpose(0, 2, 1, 3, 4)
    ssh_lora = state_shift[:, :, 3 * D_A:].reshape(DEPTH, bs, 1, PAIR)
    wkv_s = state_wkv.reshape(DEPTH, bs, N_PAIRS, PAIR, HEAD_A)
    buf16 = jnp.pad(state_pool, ((0, 0), (0, 0), (HALO - POOL_BUF, 0), (0, 0)))

    p_shift, p_wkv, p_pool, s_shift, s_pool, s_v = [], [], [], [], [], []
    for l in range(DEPTH):
        final = l == DEPTH - 1
        g_l = norm_g[l][None]
        bc_w = (pw[l], pool_scale[l][None], gmlp_ln_g[l][None])

        proj_p = _inproj(xp, g_l, w_in_p, l)
        ya_p, wk_p, shm_p, shl_p = _wkv(proj_p, bp, seq, ROWS, 1, zsh_main, zsh_lora, zs0, prm, w2, l, False)
        cb_p = _bc_prompt(proj_p, bp, seq, *bc_w, wm_p[l], bm_p[l])
        xp = _outproj(ya_p, cb_p, w_out_h, l, xp, final_norm_g[None], final)

        proj_s = _inproj(xs, g_l, w_in_p, l)
        ya_s, wkv_s, shm_s, shl_s = _wkv(proj_s, bs, dseq, dseq, ROWS // dseq, ssh_main, ssh_lora,
                                         wkv_s, prm, w2, l, True)
        cb_s, vn_s = _bc_sample(proj_s.reshape(bs, dseq, D_INP), 0, bs, dseq, buf16[l], *bc_w,
                                wm_s[l], bm_s[l])
        xs = _outproj(ya_s, cb_s, w_out_h, l, xs, final_norm_g[None], final)

        p_shift.append(jnp.concatenate([shm_p[0, :, 0], shm_p[1, :, 0], shm_p[2, :, 0], shl_p[:, 0]], axis=-1))
        s_shift.append(jnp.concatenate([shm_s[0, :, 0], shm_s[1, :, 0], shm_s[2, :, 0], shl_s[:, 0]], axis=-1))
        p_wkv.append(wk_p.reshape(bp, N_HEADS_A, HEAD_A, HEAD_A))
        p_pool.append(jnp.stack([proj_p[(b + 1) * seq - POOL_BUF:(b + 1) * seq, P_UB:P_UB + D_B]
                                 for b in range(bp)]))
        ub_s = proj_s[:, P_UB:P_UB + D_B].reshape(bs, dseq, D_B)
        s_pool.append(jnp.concatenate([state_pool[l], ub_s], axis=1)[:, -POOL_BUF:])
        s_v.append(vn_s.reshape(bs, dseq, D_C))

    y_prompt = xp.reshape(bp, seq, D_MODEL)
    y_sample = xs.reshape(bs, dseq, D_MODEL)
    s_wkv = wkv_s.reshape(DEPTH, bs, N_HEADS_A, HEAD_A, HEAD_A)
    return (y_prompt, y_sample, jnp.stack(p_shift), jnp.stack(p_wkv), jnp.stack(p_pool),
            jnp.stack(s_shift), s_wkv, jnp.stack(s_pool), jnp.stack(s_v))
```

```python
import functools

import jax
import jax.numpy as jnp
import numpy as np
from jax import lax
from jax.experimental import pallas as pl
from jax.experimental.pallas import tpu as pltpu

F32 = jnp.float32
BF16 = jnp.bfloat16

D_MODEL = 2048
DEPTH = 4
PAST_LEN = 16384
D_A = 1024
HEAD_A = 64
N_HEADS_A = 16
LORA = 64
D_B = 512
POOL_WINDOWS = (2, 4, 8, 16)
POOL_GC = 128
POOL_BUF = 15
D_C = 512
N_GROUPS_C = 4
GC = 128
CHUNK = 128
SHIFT_W = 3 * D_A + 2 * LORA
EPS = 1e-6
GN_EPS = HEAD_A * 1e-5
LN_EPS = 1e-5

P_R, P_K, P_V, P_GA = 0, 1024, 2048, 3072
P_UB, P_GB, P_UC, P_VC, P_GC = 4096, 4608, 5120, 5632, 6144
P_LORA = 6656
D_INP = 6784

PAIR = 128
N_PAIRS = D_A // PAIR
ROWS = 64

VMEM_LIMIT = 52 * 1024 * 1024
HI = lax.Precision.HIGHEST


def _dot(a, b, prec=None):
    return jnp.dot(a, b, precision=prec, preferred_element_type=F32)


def _dot_nt(a, b, prec=None):
    return lax.dot_general(a, b, (((1,), (1,)), ((), ())), precision=prec,
                           preferred_element_type=F32)


def _dot_tn(a, b, prec=None):
    return lax.dot_general(a, b, (((0,), (0,)), ((), ())), precision=prec,
                           preferred_element_type=F32)


def _split(x, n):
    pieces = []
    rem = x
    for i in range(n):
        hi = rem.astype(BF16)
        pieces.append(hi)
        if i + 1 < n:
            rem = rem - hi.astype(F32)
    return pieces


_CONTRACT = {"nn": (1, 0), "nt": (1, 1), "tn": (0, 0)}


def _mm(a, b, mode="nn", pa=1, pb=1):
    ca, cb = _CONTRACT[mode]
    sa, sb = _split(a, pa), _split(b, pb)
    terms = [(i, j) for i in range(pa) for j in range(pb) if i + j < max(pa, pb)]
    lhs = jnp.concatenate([sa[i] for i, _ in terms], axis=ca) if len(terms) > 1 else sa[0]
    rhs = jnp.concatenate([sb[j] for _, j in terms], axis=cb) if len(terms) > 1 else sb[0]
    return lax.dot_general(lhs, rhs, (((ca,), (cb,)), ((), ())), preferred_element_type=F32)


PREC_LORA = (1, 1)
PREC_CUM = (1, 3)
PREC_SEG = (2, 1)
PREC_G = (1, 1)
PREC_INV = (1, 1)
PREC_X2 = (1, 1)
PREC_WU = (1, 1)
PREC_US = (1, 1)
PREC_UPD = (1, 1)
PREC_Y = (1, 1)


def _inproj_kernel(x_ref, g_ref, w_ref, o_ref, h_ref):
    @pl.when(pl.program_id(1) == 0)
    def _():
        x = x_ref[...]
        ms = jnp.mean(x * x, axis=-1, keepdims=True)
        h_ref[...] = ((x * lax.rsqrt(ms + EPS)) * g_ref[...]).astype(BF16)

    o_ref[...] = _dot(h_ref[...], w_ref[0])


def _inproj(x, g, w, layer, tm=1024, tn=768):
    m = x.shape[0]
    return pl.pallas_call(
        _inproj_kernel,
        grid=(m // tm, pl.cdiv(D_INP, tn)),
        in_specs=[pl.BlockSpec((tm, D_MODEL), lambda i, j: (i, 0)),
                  pl.BlockSpec((1, D_MODEL), lambda i, j: (0, 0)),
                  pl.BlockSpec((1, D_MODEL, tn), lambda i, j: (layer, 0, j))],
        out_specs=pl.BlockSpec((tm, tn), lambda i, j: (i, j)),
        out_shape=jax.ShapeDtypeStruct((m, D_INP), F32),
        scratch_shapes=[pltpu.VMEM((tm, D_MODEL), BF16)],
        compiler_params=pltpu.CompilerParams(
            dimension_semantics=("parallel", "arbitrary"), vmem_limit_bytes=VMEM_LIMIT),
        name="inproj",
    )(x, g, w)


def _outproj_kernel(ca_ref, cb_ref, wa_ref, wb_ref, x_ref, g_ref, o_ref, *, final):
    y = _dot(ca_ref[...], wa_ref[0]) + _dot(cb_ref[...], wb_ref[0])
    out = x_ref[...] + y
    if final:
        ms = jnp.mean(out * out, axis=-1, keepdims=True)
        out = (out * lax.rsqrt(ms + EPS)) * g_ref[...]
    o_ref[...] = out


def _outproj(cat_a, cat_b, w, layer, x, g, final, tm=512):
    m = x.shape[0]
    half = D_MODEL // 2
    return pl.pallas_call(
        functools.partial(_outproj_kernel, final=final),
        grid=(m // tm,),
        in_specs=[pl.BlockSpec((tm, half), lambda i: (i, 0)),
                  pl.BlockSpec((tm, half), lambda i: (i, 0)),
                  pl.BlockSpec((1, half, D_MODEL), lambda i: (layer, 0, 0)),
                  pl.BlockSpec((1, half, D_MODEL), lambda i: (layer, 1, 0)),
                  pl.BlockSpec((tm, D_MODEL), lambda i: (i, 0)),
                  pl.BlockSpec((1, D_MODEL), lambda i: (0, 0))],
        out_specs=pl.BlockSpec((tm, D_MODEL), lambda i: (i, 0)),
        out_shape=jax.ShapeDtypeStruct((m, D_MODEL), F32),
        compiler_params=pltpu.CompilerParams(
            dimension_semantics=("parallel",), vmem_limit_bytes=VMEM_LIMIT),
        name="outproj",
    )(cat_a, cat_b, w, w, x, g)


PR_MU_R, PR_MU_K, PR_MU_V, PR_W0, PR_A0, PR_KK, PR_KA, PR_RK, PR_LNG, PR_LNB, PR_MU_L = range(11)


def _softplus(z):
    return jnp.maximum(z, 0.0) + jnp.log(1.0 + jnp.exp(-jnp.abs(z)))


def _sigmoid(z):
    return 1.0 / (1.0 + jnp.exp(-z))


def _silu(z):
    return z * _sigmoid(z)


def _run(gen):
    try:
        while True:
            next(gen)
    except StopIteration as e:
        return e.value


def _zip_run(main, prep, pattern):
    done = {}

    def step(gen, key):
        if key not in done:
            try:
                next(gen)
            except StopIteration as e:
                done[key] = e.value

    for ch in pattern:
        step(main if ch == "m" else prep, ch)
    while "m" not in done:
        step(main, "m")
    while "p" not in done:
        step(prep, "p")
    return done["m"], done["p"]


class _ScanConsts:
    def __init__(self, C, NB):
        rows = ROWS
        self.C, self.NB = C, NB
        row_id = lax.broadcasted_iota(jnp.int32, (rows, 1), 0)
        self.first = (row_id % C) == 0
        self.lo_half = lax.broadcasted_iota(jnp.int32, (rows, PAIR), 1) < HEAD_A
        ri = lax.broadcasted_iota(jnp.int32, (rows, rows), 0)
        ci = lax.broadcasted_iota(jnp.int32, (rows, rows), 1)
        self.tri = jnp.where(((ri // C) == (ci // C)) & (ri >= ci), 1.0, 0.0)
        pr = lax.broadcasted_iota(jnp.int32, (PAIR, PAIR), 0)
        pc = lax.broadcasted_iota(jnp.int32, (PAIR, PAIR), 1)
        self.same_head = (pr // HEAD_A) == (pc // HEAD_A)
        self.seg = jnp.where(self.same_head, 1.0, 0.0)
        r2 = lax.broadcasted_iota(jnp.int32, (2 * rows, 2 * rows), 0)
        c2 = lax.broadcasted_iota(jnp.int32, (2 * rows, 2 * rows), 1)
        same_blk = (r2 // C) == (c2 // C)
        self.m_strict = same_blk & (r2 > c2)
        m_incl = same_blk & (r2 >= c2)
        self.m_incl2 = jnp.concatenate([m_incl, m_incl], axis=1)
        self.eye = jnp.where(r2 == c2, 1.0, 0.0)


def _bmask(m, xb):
    return jnp.where(m, xb, jnp.zeros_like(xb))


_PP_BF16 = ("at", "rt", "kh", "bh", "kg", "bg", "vs")
_PP_F32 = ("bon", "v", "sg")


def _prep_steps(raw, prev, prm, w2, cs):
    C, NB, rows = cs.C, cs.NB, ROWS
    r_raw, k_raw, v_raw, g_raw, l_raw = raw
    prev_r, prev_k, prev_v, prev_l = prev

    def prow(i):
        return prm[i:i + 1, :]

    def shift_mix(x, prev_rows, mu):
        rolled = pltpu.roll(x, 1, 0)
        pr_ = jnp.concatenate([jnp.broadcast_to(q, (C, x.shape[1])) for q in prev_rows], axis=0)
        shifted = jnp.where(cs.first, pr_, rolled)
        return x + (shifted - x) * mu

    xl = shift_mix(l_raw, prev_l, prow(PR_MU_L)[:, :PAIR])
    lora_w = _mm(jnp.where(cs.lo_half, jnp.tanh(xl), 0.0), w2, "nn", *PREC_LORA)
    lora_a = _mm(jnp.where(cs.lo_half, 0.0, xl), w2, "nn", *PREC_LORA)
    yield
    r = shift_mix(r_raw, prev_r, prow(PR_MU_R))
    k = shift_mix(k_raw, prev_k, prow(PR_MU_K))
    v = shift_mix(v_raw, prev_v, prow(PR_MU_V))
    w_log = -_softplus(-(prow(PR_W0) + lora_w)) - 0.5
    dec = jnp.exp(w_log)
    a = _sigmoid(prow(PR_A0) + lora_a)
    cum = _mm(cs.tri, -dec, "nn", *PREC_CUM)
    yield
    kk_raw = k * prow(PR_KK)
    k2 = k * (1.0 + (a - 1.0) * prow(PR_KA))
    rkr = r * k2 * prow(PR_RK)
    sls = [slice(p * PAIR, (p + 1) * PAIR) for p in range(N_PAIRS)]
    s0 = [_mm(jnp.concatenate([kk_raw[:, sl] * kk_raw[:, sl], rkr[:, sl]], axis=0), cs.seg, "nn", *PREC_SEG)
          for sl in sls]
    yield
    tot = jnp.concatenate(
        [jnp.broadcast_to(cum[(b + 1) * C - 1:(b + 1) * C, :], (C, D_A)) for b in range(NB)], axis=0)
    g_incl = jnp.exp(cum)
    g_excl = jnp.exp(cum + dec)
    g_inv = jnp.exp(-cum)
    g_rest = jnp.exp(tot - cum)
    g_tot = jnp.exp(tot)
    sg = _silu(g_raw)

    def stack(x):
        xb = x.astype(BF16)
        zb = jnp.zeros_like(xb)
        return jnp.concatenate([jnp.where(cs.lo_half, xb, zb), jnp.where(cs.lo_half, zb, xb)], axis=0)

    pp = {n: [] for n in _PP_BF16 + _PP_F32 + ("gtot",)}
    for sl, q in zip(sls, s0):
        kkp = kk_raw[:, sl] * lax.rsqrt(jnp.maximum(q[:rows], 1e-12))
        kka = kkp * a[:, sl]
        pp["at"].append(stack(-kkp * g_excl[:, sl]))
        pp["rt"].append(stack(r[:, sl] * g_incl[:, sl]))
        pp["kh"].append(stack(k2[:, sl] * g_inv[:, sl]))
        pp["bh"].append(stack(kka * g_inv[:, sl]))
        pp["kg"].append(stack(k2[:, sl] * g_rest[:, sl]))
        pp["bg"].append(stack(kka * g_rest[:, sl]))
        pp["vs"].append(stack(v[:, sl]))
        pp["bon"].append(q[rows:])
        pp["v"].append(v[:, sl])
        pp["sg"].append(sg[:, sl])
        pp["gtot"].append(jnp.concatenate([g_tot[b * C:b * C + 1, sl] for b in range(NB)], axis=0))
    return pp


def _scan_steps(pp, st_ref, o_ref, row0, prm, cs):
    C, NB, rows = cs.C, cs.NB, ROWS
    n_rounds = int(np.log2(C))
    n2 = 2 * rows
    nb2 = 2 * C
    npair = N_PAIRS

    def prow(i):
        return prm[i:i + 1, :]

    def seq_rows(x, b):
        if NB == 1:
            return x
        xf = x.astype(F32)
        out = jnp.concatenate([xf[b * C:(b + 1) * C], xf[rows + b * C:rows + (b + 1) * C]], axis=0)
        return out.astype(x.dtype)

    def unseq_rows(pieces):
        if NB == 1:
            return pieces[0]
        pf = [q.astype(F32) for q in pieces]
        out = jnp.concatenate([q[:C] for q in pf] + [q[C:] for q in pf], axis=0)
        return out.astype(pieces[0].dtype)

    at_s, rt_s, kh_s, bh_s, kg_s, bg_s, v_s = (pp[n] for n in _PP_BF16)

    gm = [_mm(jnp.concatenate([x, y_], axis=0), jnp.concatenate([z, w_], axis=0), "nt", *PREC_G)
          for x, y_, z, w_ in zip(at_s, rt_s, kh_s, bh_s)]
    yield
    gmb = [q.astype(BF16) for q in gm]
    a_ak = [_bmask(cs.m_strict, q[:n2, :n2]) for q in gmb]
    a_ab = [jnp.where(cs.m_strict, q[:n2, n2:], 0.0) for q in gm]
    a_r = [_bmask(cs.m_incl2, q[n2:, :]) for q in gmb]

    x2 = [_mm(q, vs, "nn", *PREC_X2) for q, vs in zip(a_ak, v_s)]
    pwb = [_bmask(cs.m_strict, q[:n2, n2:]) for q in gmb]
    pw = [_mm(q, q, "nn", *PREC_INV) for q in pwb]
    t_inv = [cs.eye + q for q in a_ab]
    yield
    for rd in range(1, n_rounds):
        pwb = [q.astype(BF16) for q in pw]
        tb = [t.astype(BF16) for t in t_inv]
        if rd < n_rounds - 1:
            res = [_mm(jnp.concatenate([q, t], axis=0), q, "nn", *PREC_INV) for q, t in zip(pwb, tb)]
            t_inv = [t + q[n2:] for t, q in zip(t_inv, res)]
            pw = [q[:n2] for q in res]
        else:
            t_inv = [t + _mm(t_, q, "nn", *PREC_INV) for q, t, t_ in zip(pwb, t_inv, tb)]
        yield

    wu = [_mm(t, jnp.concatenate([x, y_.astype(BF16)], axis=1), "nn", *PREC_WU)
          for t, x, y_ in zip(t_inv, at_s, x2)]
    yield
    w_t = [q[:, :PAIR].astype(BF16) for q in wu]
    u_t = [q[:, PAIR:] for q in wu]

    s_old = [[st_ref[p * NB + b] for b in range(NB)] for p in range(npair)]
    us = [[_mm(jnp.concatenate([seq_rows(w_t[p], b), seq_rows(rt_s[p], b)], axis=0),
               s_old[p][b], "nt", *PREC_US) for b in range(NB)] for p in range(npair)]
    yield
    u_b = [[(us[p][b][:nb2] + seq_rows(u_t[p], b)).astype(BF16) for b in range(NB)] for p in range(npair)]
    for p in range(npair):
        for b in range(NB):
            upd = _mm(jnp.concatenate([seq_rows(v_s[p], b), u_b[p][b]], axis=0),
                      jnp.concatenate([seq_rows(kg_s[p], b), seq_rows(bg_s[p], b)], axis=0),
                      "tn", *PREC_UPD)
            st_ref[p * NB + b] = s_old[p][b] * pp["gtot"][p][b:b + 1, :] + upd
    yield
    y2 = [unseq_rows([us[p][b][nb2:] for b in range(NB)])
          + _mm(a_r[p], jnp.concatenate([v_s[p], unseq_rows(u_b[p])], axis=0), "nn", *PREC_Y)
          for p in range(npair)]
    yield
    y = [q[:rows] + q[rows:] for q in y2]
    ym = [_mm(q, cs.seg, "nn", *PREC_SEG) * (1.0 / HEAD_A) for q in y]
    yield
    yc = [q - m for q, m in zip(y, ym)]
    yv = [_mm(q * q, cs.seg, "nn", *PREC_SEG) * (1.0 / HEAD_A) for q in yc]
    yield
    for p in range(npair):
        sl = slice(p * PAIR, (p + 1) * PAIR)
        yn = yc[p] * lax.rsqrt(yv[p] + GN_EPS) * prow(PR_LNG)[:, sl] + prow(PR_LNB)[:, sl]
        o_ref[row0:row0 + rows, sl] = ((yn + pp["bon"][p] * pp["v"][p]) * pp["sg"][p]).astype(o_ref.dtype)


def _write_state(st_ref, sout_ref, cs):
    for b in range(cs.NB):
        for p in range(N_PAIRS):
            sm = jnp.where(cs.same_head, st_ref[p * cs.NB + b], 0.0)
            sout_ref[0, b, p] = sm[:, :HEAD_A] + sm[:, HEAD_A:]


def _wkv_kernel(r_ref, k_ref, v_ref, g_ref, l_ref, sh_ref, shl_ref, s0_ref, prm_ref, w2_ref,
                o_ref, sout_ref, shm_out_ref, shl_out_ref, st_ref, *, C, NB):
    cs = _ScanConsts(C, NB)
    for b in range(NB):
        for p in range(N_PAIRS):
            s = s0_ref[0, b, p]
            s2 = jnp.concatenate([s, s], axis=1)
            st_ref[p * NB + b] = jnp.where(cs.same_head, s2, 0.0)

    raw = (r_ref[...], k_ref[...], v_ref[...], g_ref[...], l_ref[...])
    prev = ([sh_ref[0, 0, b] for b in range(NB)], [sh_ref[0, 1, b] for b in range(NB)],
            [sh_ref[0, 2, b] for b in range(NB)], [shl_ref[0, b] for b in range(NB)])
    pp = _run(_prep_steps(raw, prev, prm_ref[0], w2_ref[0], cs))
    _run(_scan_steps(pp, st_ref, o_ref, 0, prm_ref[0], cs))

    for b in range(NB):
        last = (b + 1) * C - 1
        shm_out_ref[0, b] = raw[0][last:last + 1, :]
        shm_out_ref[1, b] = raw[1][last:last + 1, :]
        shm_out_ref[2, b] = raw[2][last:last + 1, :]
        shl_out_ref[b] = raw[4][last:last + 1, :]
    _write_state(st_ref, sout_ref, cs)


def _wkv_sample(proj, n_seq, C, NB, sh_main, sh_lora, s0, prm, w2, layer):
    n_groups = n_seq // NB
    assert NB * C == ROWS

    def rmap(col):
        return lambda i: (i, col)

    state_spec = pl.BlockSpec((1, NB, N_PAIRS, PAIR, HEAD_A), lambda i: (layer, i, 0, 0, 0))
    return pl.pallas_call(
        functools.partial(_wkv_kernel, C=C, NB=NB),
        grid=(n_groups,),
        in_specs=[pl.BlockSpec((ROWS, D_A), rmap(P_R // D_A)),
                  pl.BlockSpec((ROWS, D_A), rmap(P_K // D_A)),
                  pl.BlockSpec((ROWS, D_A), rmap(P_V // D_A)),
                  pl.BlockSpec((ROWS, D_A), rmap(P_GA // D_A)),
                  pl.BlockSpec((ROWS, PAIR), rmap(P_LORA // PAIR)),
                  pl.BlockSpec((1, 3, NB, 1, D_A), lambda i: (layer, 0, i, 0, 0)),
                  pl.BlockSpec((1, NB, 1, PAIR), lambda i: (layer, i, 0, 0)),
                  state_spec,
                  pl.BlockSpec((1, 16, D_A), lambda i: (layer, 0, 0)),
                  pl.BlockSpec((1, PAIR, D_A), lambda i: (layer, 0, 0))],
        out_specs=[pl.BlockSpec((ROWS, D_A), lambda i: (i, 0)),
                   state_spec,
                   pl.BlockSpec((3, NB, 1, D_A), lambda i: (0, i, 0, 0)),
                   pl.BlockSpec((NB, 1, PAIR), lambda i: (i, 0, 0))],
        out_shape=[jax.ShapeDtypeStruct((n_seq * C, D_A), BF16),
                   jax.ShapeDtypeStruct(s0.shape, F32),
                   jax.ShapeDtypeStruct((3, n_seq, 1, D_A), F32),
                   jax.ShapeDtypeStruct((n_seq, 1, PAIR), F32)],
        scratch_shapes=[pltpu.VMEM((N_PAIRS * NB, PAIR, PAIR), F32)],
        input_output_aliases={7: 1},
        compiler_params=pltpu.CompilerParams(
            dimension_semantics=("parallel",), vmem_limit_bytes=VMEM_LIMIT),
        name="wkv_c%d" % C,
    )(proj, proj, proj, proj, proj, sh_main, sh_lora, s0, prm, w2)


_PIPE_PATTERN = "mpmmpmmpmm"


def _wkv_pipe_kernel(rc_ref, kc_ref, vc_ref, gc_ref, lc_ref, rn_ref, kn_ref, vn_ref, gn_ref, ln_ref,
                     prm_ref, w2_ref, o_ref, sout_ref, shm_out_ref, shl_out_ref,
                     st_ref, pb_ref, pf_ref, pg_ref, *, n_steps):
    k = pl.program_id(1)
    rows = ROWS
    cs = _ScanConsts(rows, 1)
    prm = prm_ref[0]
    w2 = w2_ref[0]

    def raw_rows(refs, lo):
        return tuple(q[lo:lo + rows, :] for q in refs)

    def prev_rows(refs, row):
        rr, kr, vr, _, lr = refs
        return ([rr[row:row + 1, :]], [kr[row:row + 1, :]], [vr[row:row + 1, :]], [lr[row:row + 1, :]])

    def store_pp(pp):
        for p in range(N_PAIRS):
            for j, n in enumerate(_PP_BF16):
                pb_ref[p, j] = pp[n][p]
            for j, n in enumerate(_PP_F32):
                pf_ref[p, j] = pp[n][p]
            pg_ref[p] = pp["gtot"][p]

    def load_pp():
        pp = {n: [pb_ref[p, j] for p in range(N_PAIRS)] for j, n in enumerate(_PP_BF16)}
        pp.update({n: [pf_ref[p, j] for p in range(N_PAIRS)] for j, n in enumerate(_PP_F32)})
        pp["gtot"] = [pg_ref[p] for p in range(N_PAIRS)]
        return pp

    cur = (rc_ref, kc_ref, vc_ref, gc_ref, lc_ref)
    nxt = (rn_ref, kn_ref, vn_ref, gn_ref, ln_ref)

    @pl.when(k == 0)
    def _():
        st_ref[...] = jnp.zeros_like(st_ref)
        zero = ([jnp.zeros((1, D_A), F32)], [jnp.zeros((1, D_A), F32)], [jnp.zeros((1, D_A), F32)],
                [jnp.zeros((1, PAIR), F32)])
        store_pp(_run(_prep_steps(raw_rows(cur, 0), zero, prm, w2, cs)))

    pp_a = load_pp()
    _, pp_b = _zip_run(_scan_steps(pp_a, st_ref, o_ref, 0, prm, cs),
                       _prep_steps(raw_rows(cur, rows), prev_rows(cur, rows - 1), prm, w2, cs),
                       _PIPE_PATTERN)
    _, pp_n = _zip_run(_scan_steps(pp_b, st_ref, o_ref, rows, prm, cs),
                       _prep_steps(raw_rows(nxt, 0), prev_rows(cur, 2 * rows - 1), prm, w2, cs),
                       _PIPE_PATTERN)
    store_pp(pp_n)

    last = 2 * rows - 1
    shm_out_ref[0, 0] = rc_ref[last:last + 1, :]
    shm_out_ref[1, 0] = kc_ref[last:last + 1, :]
    shm_out_ref[2, 0] = vc_ref[last:last + 1, :]
    shl_out_ref[0] = lc_ref[last:last + 1, :]

    @pl.when(k == n_steps - 1)
    def _():
        _write_state(st_ref, sout_ref, cs)


def _wkv_prompt(proj, n_seq, seq_len, prm, w2, layer):
    n_steps = seq_len // (2 * ROWS)
    n_chunks = seq_len // ROWS

    def cmap(col):
        return lambda i, k: (i * n_steps + k, col)

    def nmap(col):
        return lambda i, k: (i * n_chunks + jnp.minimum(2 * k + 2, n_chunks - 1), col)

    def specs(rows_, m):
        return [pl.BlockSpec((rows_, D_A), m(P_R // D_A)),
                pl.BlockSpec((rows_, D_A), m(P_K // D_A)),
                pl.BlockSpec((rows_, D_A), m(P_V // D_A)),
                pl.BlockSpec((rows_, D_A), m(P_GA // D_A)),
                pl.BlockSpec((rows_, PAIR), m(P_LORA // PAIR))]

    return pl.pallas_call(
        functools.partial(_wkv_pipe_kernel, n_steps=n_steps),
        grid=(n_seq, n_steps),
        in_specs=specs(2 * ROWS, cmap) + specs(ROWS, nmap) + [
            pl.BlockSpec((1, 16, D_A), lambda i, k: (layer, 0, 0)),
            pl.BlockSpec((1, PAIR, D_A), lambda i, k: (layer, 0, 0))],
        out_specs=[pl.BlockSpec((2 * ROWS, D_A), lambda i, k: (i * n_steps + k, 0)),
                   pl.BlockSpec((1, 1, N_PAIRS, PAIR, HEAD_A), lambda i, k: (0, i, 0, 0, 0)),
                   pl.BlockSpec((3, 1, 1, D_A), lambda i, k: (0, i, 0, 0)),
                   pl.BlockSpec((1, 1, PAIR), lambda i, k: (i, 0, 0))],
        out_shape=[jax.ShapeDtypeStruct((n_seq * seq_len, D_A), BF16),
                   jax.ShapeDtypeStruct((1, n_seq, N_PAIRS, PAIR, HEAD_A), F32),
                   jax.ShapeDtypeStruct((3, n_seq, 1, D_A), F32),
                   jax.ShapeDtypeStruct((n_seq, 1, PAIR), F32)],
        scratch_shapes=[pltpu.VMEM((N_PAIRS, PAIR, PAIR), F32),
                        pltpu.VMEM((N_PAIRS, len(_PP_BF16), 2 * ROWS, PAIR), BF16),
                        pltpu.VMEM((N_PAIRS, len(_PP_F32), ROWS, PAIR), F32),
                        pltpu.VMEM((N_PAIRS, 1, PAIR), F32)],
        compiler_params=pltpu.CompilerParams(
            dimension_semantics=("parallel", "arbitrary"), vmem_limit_bytes=VMEM_LIMIT),
        name="wkv_pipe",
    )(*([proj] * 10), prm, w2)


HALO = 16


def _window_sums(ext):
    w2 = ext + pltpu.roll(ext, 1, 0)
    w4 = w2 + pltpu.roll(w2, 2, 0)
    w8 = w4 + pltpu.roll(w4, 4, 0)
    w16 = w8 + pltpu.roll(w8, 8, 0)
    return (w2, w4, w8, w16)


def _pool_gate(pooled_groups, gb, pw_ref, pscale):
    mixed = [_dot(pg.astype(BF16), pw_ref[g]) for g, pg in enumerate(pooled_groups)]
    yb = jnp.concatenate(mixed, axis=1) * pscale
    return yb * _silu(gb)


def _layer_norm_v(vc, ln_g):
    vm = jnp.mean(vc, axis=-1, keepdims=True)
    d = vc - vm
    vv = jnp.mean(d * d, axis=-1, keepdims=True)
    return d * lax.rsqrt(vv + LN_EPS) * ln_g


def _chunk_gate(vn, uc, gc, wm_ref, bm_ref):
    n_rows = vn.shape[0]
    vnb = vn.astype(BF16)
    outs = []
    for j in range(n_rows // CHUNK):
        rs = slice(j * CHUNK, (j + 1) * CHUNK)
        mix = [_dot(wm_ref[g], vnb[rs, g * GC:(g + 1) * GC]) + bm_ref[g] for g in range(N_GROUPS_C)]
        outs.append(jnp.concatenate(mix, axis=1))
    mix = outs[0] if len(outs) == 1 else jnp.concatenate(outs, axis=0)
    return uc * mix * _silu(gc)


def _bc_prompt_kernel(ub_ref, gb_ref, uc_ref, vc_ref, gc_ref, pw_ref, ps_ref, lng_ref, wm_ref, bm_ref,
                      o_ref, halo_ref, *, tt):
    j = pl.program_id(1)

    @pl.when(j == 0)
    def _():
        halo_ref[...] = jnp.zeros_like(halo_ref)

    u = ub_ref[...]
    ext = jnp.concatenate([halo_ref[...], u], axis=0)
    halo_ref[...] = u[tt - HALO:, :]
    sums = _window_sums(ext)
    pos = j * tt + lax.broadcasted_iota(jnp.int32, (tt, 1), 0)
    pooled = []
    for g, win in enumerate(POOL_WINDOWS):
        ls = slice(g * POOL_GC, (g + 1) * POOL_GC)
        cnt = jnp.minimum(pos + 1, win).astype(F32)
        pooled.append(sums[g][HALO:, ls] / cnt - u[:, ls])
    o_ref[:, :D_B] = _pool_gate(pooled, gb_ref[...], pw_ref, ps_ref[...]).astype(o_ref.dtype)
    vn = _layer_norm_v(vc_ref[...], lng_ref[...])
    o_ref[:, D_B:] = _chunk_gate(vn, uc_ref[...], gc_ref[...], wm_ref, bm_ref).astype(o_ref.dtype)


def _bc_prompt(proj, n_seq, seq_len, pw, pscale, ln_g, wm, bm, tt=256):
    n_t = seq_len // tt

    def rmap(col):
        return lambda i, j: (i * n_t + j, col)

    wspec = pl.BlockSpec((4, 128, 128), lambda i, j: (0, 0, 0))
    vspec = pl.BlockSpec((1, D_B), lambda i, j: (0, 0))
    return pl.pallas_call(
        functools.partial(_bc_prompt_kernel, tt=tt),
        grid=(n_seq, n_t),
        in_specs=[pl.BlockSpec((tt, D_B), rmap(P_UB // D_B)),
                  pl.BlockSpec((tt, D_B), rmap(P_GB // D_B)),
                  pl.BlockSpec((tt, D_B), rmap(P_UC // D_B)),
                  pl.BlockSpec((tt, D_B), rmap(P_VC // D_B)),
                  pl.BlockSpec((tt, D_B), rmap(P_GC // D_B)),
                  wspec, vspec, vspec, wspec, wspec],
        out_specs=pl.BlockSpec((tt, D_B + D_C), rmap(0)),
        out_shape=jax.ShapeDtypeStruct((n_seq * seq_len, D_B + D_C), BF16),
        scratch_shapes=[pltpu.VMEM((HALO, D_B), F32)],
        compiler_params=pltpu.CompilerParams(
            dimension_semantics=("parallel", "arbitrary"), vmem_limit_bytes=VMEM_LIMIT),
        name="bc_prompt",
    )(proj, proj, proj, proj, proj, pw, pscale, ln_g, wm, bm)


def _bc_sample_kernel(buf_ref, ub_ref, gb_ref, uc_ref, vc_ref, gc_ref, pw_ref, ps_ref, lng_ref,
                      wm_ref, bm_ref, o_ref, vn_ref, *, nb, t_len):
    u3 = ub_ref[...]
    ext = jnp.concatenate([buf_ref[...], u3], axis=1)
    per = HALO + t_len
    sums = _window_sums(ext.reshape(nb * per, D_B))
    u = u3.reshape(nb * t_len, D_B)
    pooled = []
    for g, win in enumerate(POOL_WINDOWS):
        ls = slice(g * POOL_GC, (g + 1) * POOL_GC)
        s3 = sums[g].reshape(nb, per, D_B)[:, HALO:, ls].reshape(nb * t_len, POOL_GC)
        cnt = float(min(PAST_LEN + 1, win))
        pooled.append(s3 / cnt - u[:, ls])
    rows = nb * t_len
    gb = gb_ref[...].reshape(rows, D_B)
    o_ref[:, :D_B] = _pool_gate(pooled, gb, pw_ref, ps_ref[...]).astype(o_ref.dtype)
    vn = _layer_norm_v(vc_ref[...].reshape(rows, D_C), lng_ref[...])
    vn_ref[...] = vn
    uc = uc_ref[...].reshape(rows, D_C)
    gc = gc_ref[...].reshape(rows, D_C)
    o_ref[:, D_B:] = _chunk_gate(vn, uc, gc, wm_ref, bm_ref).astype(o_ref.dtype)


def _bc_sample(proj3, seq0, n_seq, t_len, buf16, pw, pscale, ln_g, wm, bm):
    nb = CHUNK // t_len
    sb0 = seq0 // nb

    def rmap(col):
        return lambda i: (sb0 + i, 0, col)

    wspec = pl.BlockSpec((4, 128, 128), lambda i: (0, 0, 0))
    vspec = pl.BlockSpec((1, D_B), lambda i: (0, 0))
    rows = nb * t_len
    return pl.pallas_call(
        functools.partial(_bc_sample_kernel, nb=nb, t_len=t_len),
        grid=(n_seq // nb,),
        in_specs=[pl.BlockSpec((nb, HALO, D_B), lambda i: (i, 0, 0)),
                  pl.BlockSpec((nb, t_len, D_B), rmap(P_UB // D_B)),
                  pl.BlockSpec((nb, t_len, D_B), rmap(P_GB // D_B)),
                  pl.BlockSpec((nb, t_len, D_B), rmap(P_UC // D_B)),
                  pl.BlockSpec((nb, t_len, D_B), rmap(P_VC // D_B)),
                  pl.BlockSpec((nb, t_len, D_B), rmap(P_GC // D_B)),
                  wspec, vspec, vspec, wspec, wspec],
        out_specs=[pl.BlockSpec((rows, D_B + D_C), lambda i: (i, 0)),
                   pl.BlockSpec((rows, D_C), lambda i: (i, 0))],
        out_shape=[jax.ShapeDtypeStruct((n_seq * t_len, D_B + D_C), BF16),
                   jax.ShapeDtypeStruct((n_seq * t_len, D_C), F32)],
        compiler_params=pltpu.CompilerParams(
            dimension_semantics=("parallel",), vmem_limit_bytes=VMEM_LIMIT),
        name="bc_sample",
    )(buf16, proj3, proj3, proj3, proj3, proj3, pw, pscale, ln_g, wm, bm)


WPREP_ROWS = 256


def _wprep_kernel(w_ref, o_ref):
    o_ref[0, :, :3 * D_A] = w_ref[0, :, :3 * D_A].astype(BF16)
    o_ref[0, :, 3 * D_A:P_LORA] = w_ref[0, :, SHIFT_W:].astype(BF16)
    o_ref[0, :, P_LORA:] = w_ref[0, :, 3 * D_A:SHIFT_W].astype(BF16)


def _prep_w_in(w_in):
    spec = pl.BlockSpec((1, WPREP_ROWS, D_INP), lambda l, i: (l, i, 0))
    return pl.pallas_call(
        _wprep_kernel,
        grid=(DEPTH, D_MODEL // WPREP_ROWS),
        in_specs=[spec],
        out_specs=spec,
        out_shape=jax.ShapeDtypeStruct((DEPTH, D_MODEL, D_INP), BF16),
        compiler_params=pltpu.CompilerParams(
            dimension_semantics=("parallel", "parallel"), vmem_limit_bytes=VMEM_LIMIT),
        name="wprep",
    )(w_in)


def kernel(x_prompt, x_sample, state_shift, state_wkv, state_pool, norm_g, final_norm_g, w_in,
           shift_mu, w0, w_up, a0, a_up, k_k, k_a, r_k, lnx_g, lnx_b, pool_w, pool_scale,
           gmlp_ln_g, gmlp_ws, gmlp_b, w_out):
    bp, seq, _ = x_prompt.shape
    bs, dseq, _ = x_sample.shape
    n_p = bp * seq
    n_s = bs * dseq
    xp = x_prompt.reshape(n_p, D_MODEL)
    xs = x_sample.reshape(n_s, D_MODEL)

    w_in_p = _prep_w_in(w_in)
    w_out_h = w_out.astype(BF16)
    mu_l = jnp.pad(shift_mu[:, 3 * D_A:], ((0, 0), (0, D_A - 2 * LORA)))
    prm = jnp.stack([shift_mu[:, :D_A], shift_mu[:, D_A:2 * D_A], shift_mu[:, 2 * D_A:3 * D_A],
                     w0, a0, k_k, k_a, r_k.reshape(DEPTH, D_A), lnx_g, lnx_b, mu_l], axis=1)
    prm = jnp.pad(prm, ((0, 0), (0, 16 - prm.shape[1]), (0, 0)))
    w2 = jnp.concatenate([w_up, a_up], axis=1)
    pw = pool_w.astype(BF16)
    tril = jnp.tril(jnp.ones((CHUNK, CHUNK), F32))
    wm_p = (gmlp_ws * tril).astype(BF16)
    bm_p = jnp.broadcast_to(gmlp_b[:, :, :, None], (DEPTH, N_GROUPS_C, CHUNK, GC))
    nb_s = CHUNK // dseq
    eye_b = jnp.eye(nb_s, dtype=F32)
    ws_small = gmlp_ws[:, :, :dseq, :dseq] * tril[:dseq, :dseq]
    wm_s = jnp.einsum('ab,lgts->lgatbs', eye_b, ws_small).reshape(DEPTH, N_GROUPS_C, CHUNK, CHUNK)
    wm_s = wm_s.astype(BF16)
    bm_s = jnp.broadcast_to(jnp.tile(gmlp_b[:, :, :dseq], (1, 1, nb_s))[:, :, :, None],
                            (DEPTH, N_GROUPS_C, CHUNK, GC))

    ssh_main = state_shift[:, :, :3 * D_A].reshape(DEPTH, bs, 3, 1, D_A).transpose(0, 2, 1, 3, 4)
    ssh_lora = state_shift[:, :, 3 * D_A:].reshape(DEPTH, bs, 1, PAIR)
    wkv_s = state_wkv.reshape(DEPTH, bs, N_PAIRS, PAIR, HEAD_A)
    buf16 = jnp.pad(state_pool, ((0, 0), (0, 0), (HALO - POOL_BUF, 0), (0, 0)))

    p_shift, p_wkv, p_pool, s_shift, s_pool, s_v = [], [], [], [], [], []
    for l in range(DEPTH):
        final = l == DEPTH - 1
        g_l = norm_g[l][None]
        bc_w = (pw[l], pool_scale[l][None], gmlp_ln_g[l][None])

        proj_p = _inproj(xp, g_l, w_in_p, l)
        ya_p, wk_p, shm_p, shl_p = _wkv_prompt(proj_p, bp, seq, prm, w2, l)
        cb_p = _bc_prompt(proj_p, bp, seq, *bc_w, wm_p[l], bm_p[l])
        xp = _outproj(ya_p, cb_p, w_out_h, l, xp, final_norm_g[None], final)

        proj_s = _inproj(xs, g_l, w_in_p, l)
        ya_s, wkv_s, shm_s, shl_s = _wkv_sample(proj_s, bs, dseq, ROWS // dseq, ssh_main, ssh_lora,
                                                wkv_s, prm, w2, l)
        cb_s, vn_s = _bc_sample(proj_s.reshape(bs, dseq, D_INP), 0, bs, dseq, buf16[l], *bc_w,
                                wm_s[l], bm_s[l])
        xs = _outproj(ya_s, cb_s, w_out_h, l, xs, final_norm_g[None], final)

        p_shift.append(jnp.concatenate([shm_p[0, :, 0], shm_p[1, :, 0], shm_p[2, :, 0], shl_p[:, 0]], axis=-1))
        s_shift.append(jnp.concatenate([shm_s[0, :, 0], shm_s[1, :, 0], shm_s[2, :, 0], shl_s[:, 0]], axis=-1))
        p_wkv.append(wk_p.reshape(bp, N_HEADS_A, HEAD_A, HEAD_A))
        p_pool.append(jnp.stack([proj_p[(b + 1) * seq - POOL_BUF:(b + 1) * seq, P_UB:P_UB + D_B]
                                 for b in range(bp)]))
        ub_s = proj_s[:, P_UB:P_UB + D_B].reshape(bs, dseq, D_B)
        s_pool.append(jnp.concatenate([state_pool[l], ub_s], axis=1)[:, -POOL_BUF:])
        s_v.append(vn_s.reshape(bs, dseq, D_C))

    y_prompt = xp.reshape(bp, seq, D_MODEL)
    y_sample = xs.reshape(bs, dseq, D_MODEL)
    s_wkv = wkv_s.reshape(DEPTH, bs, N_HEADS_A, HEAD_A, HEAD_A)
    return (y_prompt, y_sample, jnp.stack(p_shift), jnp.stack(p_wkv), jnp.stack(p_pool),
            jnp.stack(s_shift), s_wkv, jnp.stack(s_pool), jnp.stack(s_v))
```

```python
import functools

import jax
import jax.numpy as jnp
import numpy as np
from jax import lax
from jax.experimental import pallas as pl
from jax.experimental.pallas import tpu as pltpu

F32 = jnp.float32
BF16 = jnp.bfloat16

D_MODEL = 2048
DEPTH = 4
PAST_LEN = 16384
D_A = 1024
HEAD_A = 64
N_HEADS_A = 16
LORA = 64
D_B = 512
POOL_WINDOWS = (2, 4, 8, 16)
POOL_GC = 128
POOL_BUF = 15
D_C = 512
N_GROUPS_C = 4
GC = 128
CHUNK = 128
SHIFT_W = 3 * D_A + 2 * LORA
EPS = 1e-6
GN_EPS = HEAD_A * 1e-5
LN_EPS = 1e-5

P_R, P_K, P_V, P_GA = 0, 1024, 2048, 3072
P_UB, P_GB, P_UC, P_VC, P_GC = 4096, 4608, 5120, 5632, 6144
P_LORA = 6656
D_INP = 6784

PAIR = 128
N_PAIRS = D_A // PAIR
ROWS = 64

VMEM_LIMIT = 52 * 1024 * 1024
HI = lax.Precision.HIGHEST


def _dot(a, b, prec=None):
    return jnp.dot(a, b, precision=prec, preferred_element_type=F32)


def _dot_nt(a, b, prec=None):
    return lax.dot_general(a, b, (((1,), (1,)), ((), ())), precision=prec,
                           preferred_element_type=F32)


def _dot_tn(a, b, prec=None):
    return lax.dot_general(a, b, (((0,), (0,)), ((), ())), precision=prec,
                           preferred_element_type=F32)


def _split(x, n):
    pieces = []
    rem = x
    for i in range(n):
        hi = rem.astype(BF16)
        pieces.append(hi)
        if i + 1 < n:
            rem = rem - hi.astype(F32)
    return pieces


_CONTRACT = {"nn": (1, 0), "nt": (1, 1), "tn": (0, 0)}


def _mm(a, b, mode="nn", pa=1, pb=1):
    ca, cb = _CONTRACT[mode]
    sa, sb = _split(a, pa), _split(b, pb)
    terms = [(i, j) for i in range(pa) for j in range(pb) if i + j < max(pa, pb)]
    lhs = jnp.concatenate([sa[i] for i, _ in terms], axis=ca) if len(terms) > 1 else sa[0]
    rhs = jnp.concatenate([sb[j] for _, j in terms], axis=cb) if len(terms) > 1 else sb[0]
    return lax.dot_general(lhs, rhs, (((ca,), (cb,)), ((), ())), preferred_element_type=F32)


PREC_LORA = (1, 1)
PREC_CUM = (1, 3)
PREC_SEG = (2, 1)
PREC_G = (1, 1)
PREC_INV = (1, 1)
PREC_X2 = (1, 1)
PREC_WU = (1, 1)
PREC_US = (1, 1)
PREC_UPD = (1, 1)
PREC_Y = (1, 1)


def _inproj_kernel(x_ref, g_ref, w_ref, o_ref, h_ref):
    @pl.when(pl.program_id(1) == 0)
    def _():
        x = x_ref[...]
        ms = jnp.mean(x * x, axis=-1, keepdims=True)
        h_ref[...] = ((x * lax.rsqrt(ms + EPS)) * g_ref[...]).astype(BF16)

    o_ref[...] = _dot(h_ref[...], w_ref[0])


def _inproj(x, g, w, layer, tm=1024, tn=768):
    m = x.shape[0]
    return pl.pallas_call(
        _inproj_kernel,
        grid=(m // tm, pl.cdiv(D_INP, tn)),
        in_specs=[pl.BlockSpec((tm, D_MODEL), lambda i, j: (i, 0)),
                  pl.BlockSpec((1, D_MODEL), lambda i, j: (0, 0)),
                  pl.BlockSpec((1, D_MODEL, tn), lambda i, j: (layer, 0, j))],
        out_specs=pl.BlockSpec((tm, tn), lambda i, j: (i, j)),
        out_shape=jax.ShapeDtypeStruct((m, D_INP), F32),
        scratch_shapes=[pltpu.VMEM((tm, D_MODEL), BF16)],
        compiler_params=pltpu.CompilerParams(
            dimension_semantics=("parallel", "arbitrary"), vmem_limit_bytes=VMEM_LIMIT),
        name="inproj",
    )(x, g, w)


def _inproj_h_kernel(h_ref, w_ref, o_ref):
    o_ref[...] = _dot(h_ref[...], w_ref[0])


def _inproj_h(h, w, layer, tm, tn=768):
    m = h.shape[0]
    return pl.pallas_call(
        _inproj_h_kernel,
        grid=(m // tm, pl.cdiv(D_INP, tn)),
        in_specs=[pl.BlockSpec((tm, D_MODEL), lambda i, j: (i, 0)),
                  pl.BlockSpec((1, D_MODEL, tn), lambda i, j: (layer, 0, j))],
        out_specs=pl.BlockSpec((tm, tn), lambda i, j: (i, j)),
        out_shape=jax.ShapeDtypeStruct((m, D_INP), F32),
        compiler_params=pltpu.CompilerParams(
            dimension_semantics=("parallel", "arbitrary"), vmem_limit_bytes=VMEM_LIMIT),
        name="inproj_h",
    )(h, w)


def _outproj_kernel(ca_ref, cb_ref, wa_ref, wb_ref, x_ref, g_ref, *o_refs, final):
    y = _dot(ca_ref[...], wa_ref[0]) + _dot(cb_ref[...], wb_ref[0])
    out = x_ref[...] + y
    ms = jnp.mean(out * out, axis=-1, keepdims=True)
    normed = (out * lax.rsqrt(ms + EPS)) * g_ref[...]
    if final:
        o_refs[0][...] = normed
    else:
        o_refs[0][...] = out
        o_refs[1][...] = normed.astype(BF16)


def _outproj(cat_a, cat_b, w, layer, x, g, final, tm=512):
    m = x.shape[0]
    half = D_MODEL // 2
    row_spec = pl.BlockSpec((tm, D_MODEL), lambda i: (i, 0))
    f32_out = jax.ShapeDtypeStruct((m, D_MODEL), F32)
    return pl.pallas_call(
        functools.partial(_outproj_kernel, final=final),
        grid=(m // tm,),
        in_specs=[pl.BlockSpec((tm, half), lambda i: (i, 0)),
                  pl.BlockSpec((tm, half), lambda i: (i, 0)),
                  pl.BlockSpec((1, half, D_MODEL), lambda i: (layer, 0, 0)),
                  pl.BlockSpec((1, half, D_MODEL), lambda i: (layer, 1, 0)),
                  pl.BlockSpec((tm, D_MODEL), lambda i: (i, 0)),
                  pl.BlockSpec((1, D_MODEL), lambda i: (0, 0))],
        out_specs=row_spec if final else [row_spec, row_spec],
        out_shape=f32_out if final else [f32_out, jax.ShapeDtypeStruct((m, D_MODEL), BF16)],
        compiler_params=pltpu.CompilerParams(
            dimension_semantics=("parallel",), vmem_limit_bytes=VMEM_LIMIT),
        name="outproj",
    )(cat_a, cat_b, w, w, x, g)


PR_MU_R, PR_MU_K, PR_MU_V, PR_W0, PR_A0, PR_KK, PR_KA, PR_RK, PR_LNG, PR_LNB, PR_MU_L = range(11)


def _softplus(z):
    return jnp.maximum(z, 0.0) + jnp.log(1.0 + jnp.exp(-jnp.abs(z)))


def _sigmoid(z):
    return 1.0 / (1.0 + jnp.exp(-z))


def _silu(z):
    return z * _sigmoid(z)


def _run(gen):
    try:
        while True:
            next(gen)
    except StopIteration as e:
        return e.value


def _zip_run(main, prep, pattern):
    done = {}

    def step(gen, key):
        if key not in done:
            try:
                next(gen)
            except StopIteration as e:
                done[key] = e.value

    for ch in pattern:
        step(main if ch == "m" else prep, ch)
    while "m" not in done:
        step(main, "m")
    while "p" not in done:
        step(prep, "p")
    return done["m"], done["p"]


class _ScanConsts:
    def __init__(self, C, NB):
        rows = ROWS
        self.C, self.NB = C, NB
        row_id = lax.broadcasted_iota(jnp.int32, (rows, 1), 0)
        self.first = (row_id % C) == 0
        self.lo_half = lax.broadcasted_iota(jnp.int32, (rows, PAIR), 1) < HEAD_A
        ri = lax.broadcasted_iota(jnp.int32, (rows, rows), 0)
        ci = lax.broadcasted_iota(jnp.int32, (rows, rows), 1)
        self.tri = jnp.where(((ri // C) == (ci // C)) & (ri >= ci), 1.0, 0.0)
        pr = lax.broadcasted_iota(jnp.int32, (PAIR, PAIR), 0)
        pc = lax.broadcasted_iota(jnp.int32, (PAIR, PAIR), 1)
        self.same_head = (pr // HEAD_A) == (pc // HEAD_A)
        self.seg = jnp.where(self.same_head, 1.0, 0.0)
        r2 = lax.broadcasted_iota(jnp.int32, (2 * rows, 2 * rows), 0)
        c2 = lax.broadcasted_iota(jnp.int32, (2 * rows, 2 * rows), 1)
        same_blk = (r2 // C) == (c2 // C)
        self.m_strict = same_blk & (r2 > c2)
        m_incl = same_blk & (r2 >= c2)
        self.m_incl2 = jnp.concatenate([m_incl, m_incl], axis=1)
        self.eye = jnp.where(r2 == c2, 1.0, 0.0)


def _bmask(m, xb):
    return jnp.where(m, xb, jnp.zeros_like(xb))


_PP_BF16 = ("at", "rt", "kh", "bh", "kg", "bg", "vs")
_PP_F32 = ("bon", "v", "sg")


def _prep_steps(raw, prev, prm, w2, cs):
    C, NB, rows = cs.C, cs.NB, ROWS
    r_raw, k_raw, v_raw, g_raw, l_raw = raw
    prev_r, prev_k, prev_v, prev_l = prev

    def prow(i):
        return prm[i:i + 1, :]

    def shift_mix(x, prev_rows, mu):
        rolled = pltpu.roll(x, 1, 0)
        pr_ = jnp.concatenate([jnp.broadcast_to(q, (C, x.shape[1])) for q in prev_rows], axis=0)
        shifted = jnp.where(cs.first, pr_, rolled)
        return x + (shifted - x) * mu

    xl = shift_mix(l_raw, prev_l, prow(PR_MU_L)[:, :PAIR])
    lora_w = _mm(jnp.where(cs.lo_half, jnp.tanh(xl), 0.0), w2, "nn", *PREC_LORA)
    lora_a = _mm(jnp.where(cs.lo_half, 0.0, xl), w2, "nn", *PREC_LORA)
    yield
    r = shift_mix(r_raw, prev_r, prow(PR_MU_R))
    k = shift_mix(k_raw, prev_k, prow(PR_MU_K))
    v = shift_mix(v_raw, prev_v, prow(PR_MU_V))
    w_log = -_softplus(-(prow(PR_W0) + lora_w)) - 0.5
    dec = jnp.exp(w_log)
    a = _sigmoid(prow(PR_A0) + lora_a)
    cum = _mm(cs.tri, -dec, "nn", *PREC_CUM)
    yield
    kk_raw = k * prow(PR_KK)
    k2 = k * (1.0 + (a - 1.0) * prow(PR_KA))
    rkr = r * k2 * prow(PR_RK)
    sls = [slice(p * PAIR, (p + 1) * PAIR) for p in range(N_PAIRS)]
    s0 = [_mm(jnp.concatenate([kk_raw[:, sl] * kk_raw[:, sl], rkr[:, sl]], axis=0), cs.seg, "nn", *PREC_SEG)
          for sl in sls]
    yield
    tot = jnp.concatenate(
        [jnp.broadcast_to(cum[(b + 1) * C - 1:(b + 1) * C, :], (C, D_A)) for b in range(NB)], axis=0)
    g_incl = jnp.exp(cum)
    g_excl = jnp.exp(cum + dec)
    g_inv = jnp.exp(-cum)
    g_rest = jnp.exp(tot - cum)
    g_tot = jnp.exp(tot)
    sg = _silu(g_raw)

    def stack(x):
        xb = x.astype(BF16)
        zb = jnp.zeros_like(xb)
        return jnp.concatenate([jnp.where(cs.lo_half, xb, zb), jnp.where(cs.lo_half, zb, xb)], axis=0)

    pp = {n: [] for n in _PP_BF16 + _PP_F32 + ("gtot",)}
    for sl, q in zip(sls, s0):
        kkp = kk_raw[:, sl] * lax.rsqrt(jnp.maximum(q[:rows], 1e-12))
        kka = kkp * a[:, sl]
        pp["at"].append(stack(-kkp * g_excl[:, sl]))
        pp["rt"].append(stack(r[:, sl] * g_incl[:, sl]))
        pp["kh"].append(stack(k2[:, sl] * g_inv[:, sl]))
        pp["bh"].append(stack(kka * g_inv[:, sl]))
        pp["kg"].append(stack(k2[:, sl] * g_rest[:, sl]))
        pp["bg"].append(stack(kka * g_rest[:, sl]))
        pp["vs"].append(stack(v[:, sl]))
        pp["bon"].append(q[rows:])
        pp["v"].append(v[:, sl])
        pp["sg"].append(sg[:, sl])
        pp["gtot"].append(jnp.concatenate([g_tot[b * C:b * C + 1, sl] for b in range(NB)], axis=0))
    return pp


def _scan_steps(pp, st_ref, o_ref, row0, prm, cs):
    C, NB, rows = cs.C, cs.NB, ROWS
    n_rounds = int(np.log2(C))
    n2 = 2 * rows
    nb2 = 2 * C
    npair = N_PAIRS

    def prow(i):
        return prm[i:i + 1, :]

    def seq_rows(x, b):
        if NB == 1:
            return x
        xf = x.astype(F32)
        out = jnp.concatenate([xf[b * C:(b + 1) * C], xf[rows + b * C:rows + (b + 1) * C]], axis=0)
        return out.astype(x.dtype)

    def unseq_rows(pieces):
        if NB == 1:
            return pieces[0]
        pf = [q.astype(F32) for q in pieces]
        out = jnp.concatenate([q[:C] for q in pf] + [q[C:] for q in pf], axis=0)
        return out.astype(pieces[0].dtype)

    at_s, rt_s, kh_s, bh_s, kg_s, bg_s, v_s = (pp[n] for n in _PP_BF16)

    gm = [_mm(jnp.concatenate([x, y_], axis=0), jnp.concatenate([z, w_], axis=0), "nt", *PREC_G)
          for x, y_, z, w_ in zip(at_s, rt_s, kh_s, bh_s)]
    yield
    gmb = [q.astype(BF16) for q in gm]
    a_ak = [_bmask(cs.m_strict, q[:n2, :n2]) for q in gmb]
    a_ab = [jnp.where(cs.m_strict, q[:n2, n2:], 0.0) for q in gm]
    a_r = [_bmask(cs.m_incl2, q[n2:, :]) for q in gmb]

    x2 = [_mm(q, vs, "nn", *PREC_X2) for q, vs in zip(a_ak, v_s)]
    pwb = [_bmask(cs.m_strict, q[:n2, n2:]) for q in gmb]
    pw = [_mm(q, q, "nn", *PREC_INV) for q in pwb]
    t_inv = [cs.eye + q for q in a_ab]
    yield
    for rd in range(1, n_rounds):
        pwb = [q.astype(BF16) for q in pw]
        tb = [t.astype(BF16) for t in t_inv]
        if rd < n_rounds - 1:
            res = [_mm(jnp.concatenate([q, t], axis=0), q, "nn", *PREC_INV) for q, t in zip(pwb, tb)]
            t_inv = [t + q[n2:] for t, q in zip(t_inv, res)]
            pw = [q[:n2] for q in res]
        else:
            t_inv = [t + _mm(t_, q, "nn", *PREC_INV) for q, t, t_ in zip(pwb, t_inv, tb)]
        yield

    wu = [_mm(t, jnp.concatenate([x, y_.astype(BF16)], axis=1), "nn", *PREC_WU)
          for t, x, y_ in zip(t_inv, at_s, x2)]
    yield
    w_t = [q[:, :PAIR].astype(BF16) for q in wu]
    u_t = [q[:, PAIR:] for q in wu]

    s_old = [[st_ref[p * NB + b] for b in range(NB)] for p in range(npair)]
    us = [[_mm(jnp.concatenate([seq_rows(w_t[p], b), seq_rows(rt_s[p], b)], axis=0),
               s_old[p][b], "nt", *PREC_US) for b in range(NB)] for p in range(npair)]
    yield
    u_b = [[(us[p][b][:nb2] + seq_rows(u_t[p], b)).astype(BF16) for b in range(NB)] for p in range(npair)]
    for p in range(npair):
        for b in range(NB):
            upd = _mm(jnp.concatenate([seq_rows(v_s[p], b), u_b[p][b]], axis=0),
                      jnp.concatenate([seq_rows(kg_s[p], b), seq_rows(bg_s[p], b)], axis=0),
                      "tn", *PREC_UPD)
            st_ref[p * NB + b] = s_old[p][b] * pp["gtot"][p][b:b + 1, :] + upd
    yield
    y2 = [unseq_rows([us[p][b][nb2:] for b in range(NB)])
          + _mm(a_r[p], jnp.concatenate([v_s[p], unseq_rows(u_b[p])], axis=0), "nn", *PREC_Y)
          for p in range(npair)]
    yield
    y = [q[:rows] + q[rows:] for q in y2]
    ym = [_mm(q, cs.seg, "nn", *PREC_SEG) * (1.0 / HEAD_A) for q in y]
    yield
    yc = [q - m for q, m in zip(y, ym)]
    yv = [_mm(q * q, cs.seg, "nn", *PREC_SEG) * (1.0 / HEAD_A) for q in yc]
    yield
    for p in range(npair):
        sl = slice(p * PAIR, (p + 1) * PAIR)
        yn = yc[p] * lax.rsqrt(yv[p] + GN_EPS) * prow(PR_LNG)[:, sl] + prow(PR_LNB)[:, sl]
        o_ref[row0:row0 + rows, sl] = ((yn + pp["bon"][p] * pp["v"][p]) * pp["sg"][p]).astype(o_ref.dtype)


def _write_state(st_ref, sout_ref, cs):
    for b in range(cs.NB):
        for p in range(N_PAIRS):
            sm = jnp.where(cs.same_head, st_ref[p * cs.NB + b], 0.0)
            sout_ref[0, b, p] = sm[:, :HEAD_A] + sm[:, HEAD_A:]


def _wkv_kernel(r_ref, k_ref, v_ref, g_ref, l_ref, sh_ref, shl_ref, s0_ref, prm_ref, w2_ref,
                o_ref, sout_ref, shm_out_ref, shl_out_ref, st_ref, *, C, NB):
    cs = _ScanConsts(C, NB)
    for b in range(NB):
        for p in range(N_PAIRS):
            s = s0_ref[0, b, p]
            s2 = jnp.concatenate([s, s], axis=1)
            st_ref[p * NB + b] = jnp.where(cs.same_head, s2, 0.0)

    raw = (r_ref[...], k_ref[...], v_ref[...], g_ref[...], l_ref[...])
    prev = ([sh_ref[0, 0, b] for b in range(NB)], [sh_ref[0, 1, b] for b in range(NB)],
            [sh_ref[0, 2, b] for b in range(NB)], [shl_ref[0, b] for b in range(NB)])
    pp = _run(_prep_steps(raw, prev, prm_ref[0], w2_ref[0], cs))
    _run(_scan_steps(pp, st_ref, o_ref, 0, prm_ref[0], cs))

    for b in range(NB):
        last = (b + 1) * C - 1
        shm_out_ref[0, b] = raw[0][last:last + 1, :]
        shm_out_ref[1, b] = raw[1][last:last + 1, :]
        shm_out_ref[2, b] = raw[2][last:last + 1, :]
        shl_out_ref[b] = raw[4][last:last + 1, :]
    _write_state(st_ref, sout_ref, cs)


def _wkv_sample(proj, n_seq, C, NB, sh_main, sh_lora, s0, prm, w2, layer):
    n_groups = n_seq // NB
    assert NB * C == ROWS

    def rmap(col):
        return lambda i: (i, col)

    state_spec = pl.BlockSpec((1, NB, N_PAIRS, PAIR, HEAD_A), lambda i: (layer, i, 0, 0, 0))
    return pl.pallas_call(
        functools.partial(_wkv_kernel, C=C, NB=NB),
        grid=(n_groups,),
        in_specs=[pl.BlockSpec((ROWS, D_A), rmap(P_R // D_A)),
                  pl.BlockSpec((ROWS, D_A), rmap(P_K // D_A)),
                  pl.BlockSpec((ROWS, D_A), rmap(P_V // D_A)),
                  pl.BlockSpec((ROWS, D_A), rmap(P_GA // D_A)),
                  pl.BlockSpec((ROWS, PAIR), rmap(P_LORA // PAIR)),
                  pl.BlockSpec((1, 3, NB, 1, D_A), lambda i: (layer, 0, i, 0, 0)),
                  pl.BlockSpec((1, NB, 1, PAIR), lambda i: (layer, i, 0, 0)),
                  state_spec,
                  pl.BlockSpec((1, 16, D_A), lambda i: (layer, 0, 0)),
                  pl.BlockSpec((1, PAIR, D_A), lambda i: (layer, 0, 0))],
        out_specs=[pl.BlockSpec((ROWS, D_A), lambda i: (i, 0)),
                   state_spec,
                   pl.BlockSpec((3, NB, 1, D_A), lambda i: (0, i, 0, 0)),
                   pl.BlockSpec((NB, 1, PAIR), lambda i: (i, 0, 0))],
        out_shape=[jax.ShapeDtypeStruct((n_seq * C, D_A), BF16),
                   jax.ShapeDtypeStruct(s0.shape, F32),
                   jax.ShapeDtypeStruct((3, n_seq, 1, D_A), F32),
                   jax.ShapeDtypeStruct((n_seq, 1, PAIR), F32)],
        scratch_shapes=[pltpu.VMEM((N_PAIRS * NB, PAIR, PAIR), F32)],
        input_output_aliases={7: 1},
        compiler_params=pltpu.CompilerParams(
            dimension_semantics=("parallel",), vmem_limit_bytes=VMEM_LIMIT),
        name="wkv_c%d" % C,
    )(proj, proj, proj, proj, proj, sh_main, sh_lora, s0, prm, w2)


_PIPE_PATTERN = "mpmmpmmpmm"


def _wkv_pipe_kernel(rc_ref, kc_ref, vc_ref, gc_ref, lc_ref, rn_ref, kn_ref, vn_ref, gn_ref, ln_ref,
                     prm_ref, w2_ref, o_ref, sout_ref, shm_out_ref, shl_out_ref,
                     st_ref, pb_ref, pf_ref, pg_ref, *, n_steps):
    k = pl.program_id(1)
    rows = ROWS
    cs = _ScanConsts(rows, 1)
    prm = prm_ref[0]
    w2 = w2_ref[0]

    def raw_rows(refs, lo):
        return tuple(q[lo:lo + rows, :] for q in refs)

    def prev_rows(refs, row):
        rr, kr, vr, _, lr = refs
        return ([rr[row:row + 1, :]], [kr[row:row + 1, :]], [vr[row:row + 1, :]], [lr[row:row + 1, :]])

    def store_pp(pp):
        for p in range(N_PAIRS):
            for j, n in enumerate(_PP_BF16):
                pb_ref[p, j] = pp[n][p]
            for j, n in enumerate(_PP_F32):
                pf_ref[p, j] = pp[n][p]
            pg_ref[p] = pp["gtot"][p]

    def load_pp():
        pp = {n: [pb_ref[p, j] for p in range(N_PAIRS)] for j, n in enumerate(_PP_BF16)}
        pp.update({n: [pf_ref[p, j] for p in range(N_PAIRS)] for j, n in enumerate(_PP_F32)})
        pp["gtot"] = [pg_ref[p] for p in range(N_PAIRS)]
        return pp

    cur = (rc_ref, kc_ref, vc_ref, gc_ref, lc_ref)
    nxt = (rn_ref, kn_ref, vn_ref, gn_ref, ln_ref)

    @pl.when(k == 0)
    def _():
        st_ref[...] = jnp.zeros_like(st_ref)
        zero = ([jnp.zeros((1, D_A), F32)], [jnp.zeros((1, D_A), F32)], [jnp.zeros((1, D_A), F32)],
                [jnp.zeros((1, PAIR), F32)])
        store_pp(_run(_prep_steps(raw_rows(cur, 0), zero, prm, w2, cs)))

    pp_a = load_pp()
    _, pp_b = _zip_run(_scan_steps(pp_a, st_ref, o_ref, 0, prm, cs),
                       _prep_steps(raw_rows(cur, rows), prev_rows(cur, rows - 1), prm, w2, cs),
                       _PIPE_PATTERN)
    _, pp_n = _zip_run(_scan_steps(pp_b, st_ref, o_ref, rows, prm, cs),
                       _prep_steps(raw_rows(nxt, 0), prev_rows(cur, 2 * rows - 1), prm, w2, cs),
                       _PIPE_PATTERN)
    store_pp(pp_n)

    last = 2 * rows - 1
    shm_out_ref[0, 0] = rc_ref[last:last + 1, :]
    shm_out_ref[1, 0] = kc_ref[last:last + 1, :]
    shm_out_ref[2, 0] = vc_ref[last:last + 1, :]
    shl_out_ref[0] = lc_ref[last:last + 1, :]

    @pl.when(k == n_steps - 1)
    def _():
        _write_state(st_ref, sout_ref, cs)


def _wkv_prompt(proj, n_seq, seq_len, prm, w2, layer):
    n_steps = seq_len // (2 * ROWS)
    n_chunks = seq_len // ROWS

    def cmap(col):
        return lambda i, k: (i * n_steps + k, col)

    def nmap(col):
        return lambda i, k: (i * n_chunks + jnp.minimum(2 * k + 2, n_chunks - 1), col)

    def specs(rows_, m):
        return [pl.BlockSpec((rows_, D_A), m(P_R // D_A)),
                pl.BlockSpec((rows_, D_A), m(P_K // D_A)),
                pl.BlockSpec((rows_, D_A), m(P_V // D_A)),
                pl.BlockSpec((rows_, D_A), m(P_GA // D_A)),
                pl.BlockSpec((rows_, PAIR), m(P_LORA // PAIR))]

    return pl.pallas_call(
        functools.partial(_wkv_pipe_kernel, n_steps=n_steps),
        grid=(n_seq, n_steps),
        in_specs=specs(2 * ROWS, cmap) + specs(ROWS, nmap) + [
            pl.BlockSpec((1, 16, D_A), lambda i, k: (layer, 0, 0)),
            pl.BlockSpec((1, PAIR, D_A), lambda i, k: (layer, 0, 0))],
        out_specs=[pl.BlockSpec((2 * ROWS, D_A), lambda i, k: (i * n_steps + k, 0)),
                   pl.BlockSpec((1, 1, N_PAIRS, PAIR, HEAD_A), lambda i, k: (0, i, 0, 0, 0)),
                   pl.BlockSpec((3, 1, 1, D_A), lambda i, k: (0, i, 0, 0)),
                   pl.BlockSpec((1, 1, PAIR), lambda i, k: (i, 0, 0))],
        out_shape=[jax.ShapeDtypeStruct((n_seq * seq_len, D_A), BF16),
                   jax.ShapeDtypeStruct((1, n_seq, N_PAIRS, PAIR, HEAD_A), F32),
                   jax.ShapeDtypeStruct((3, n_seq, 1, D_A), F32),
                   jax.ShapeDtypeStruct((n_seq, 1, PAIR), F32)],
        scratch_shapes=[pltpu.VMEM((N_PAIRS, PAIR, PAIR), F32),
                        pltpu.VMEM((N_PAIRS, len(_PP_BF16), 2 * ROWS, PAIR), BF16),
                        pltpu.VMEM((N_PAIRS, len(_PP_F32), ROWS, PAIR), F32),
                        pltpu.VMEM((N_PAIRS, 1, PAIR), F32)],
        compiler_params=pltpu.CompilerParams(
            dimension_semantics=("parallel", "arbitrary"), vmem_limit_bytes=VMEM_LIMIT),
        name="wkv_pipe",
    )(*([proj] * 10), prm, w2)


HALO = 16


def _window_sums(ext):
    w2 = ext + pltpu.roll(ext, 1, 0)
    w4 = w2 + pltpu.roll(w2, 2, 0)
    w8 = w4 + pltpu.roll(w4, 4, 0)
    w16 = w8 + pltpu.roll(w8, 8, 0)
    return (w2, w4, w8, w16)


def _pool_gate(pooled_groups, gb, pw_ref, pscale):
    mixed = [_dot(pg.astype(BF16), pw_ref[g]) for g, pg in enumerate(pooled_groups)]
    yb = jnp.concatenate(mixed, axis=1) * pscale
    return yb * _silu(gb)


def _layer_norm_v(vc, ln_g):
    vm = jnp.mean(vc, axis=-1, keepdims=True)
    d = vc - vm
    vv = jnp.mean(d * d, axis=-1, keepdims=True)
    return d * lax.rsqrt(vv + LN_EPS) * ln_g


def _chunk_gate(vn, uc, gc, wm_ref, bm_ref):
    n_rows = vn.shape[0]
    vnb = vn.astype(BF16)
    outs = []
    for j in range(n_rows // CHUNK):
        rs = slice(j * CHUNK, (j + 1) * CHUNK)
        mix = [_dot(wm_ref[g], vnb[rs, g * GC:(g + 1) * GC]) + bm_ref[g] for g in range(N_GROUPS_C)]
        outs.append(jnp.concatenate(mix, axis=1))
    mix = outs[0] if len(outs) == 1 else jnp.concatenate(outs, axis=0)
    return uc * mix * _silu(gc)


def _bc_prompt_kernel(ub_ref, gb_ref, uc_ref, vc_ref, gc_ref, pw_ref, ps_ref, lng_ref, wm_ref, bm_ref,
                      o_ref, halo_ref, *, tt):
    j = pl.program_id(1)

    @pl.when(j == 0)
    def _():
        halo_ref[...] = jnp.zeros_like(halo_ref)

    u = ub_ref[...]
    ext = jnp.concatenate([halo_ref[...], u], axis=0)
    halo_ref[...] = u[tt - HALO:, :]
    sums = _window_sums(ext)
    pos = j * tt + lax.broadcasted_iota(jnp.int32, (tt, 1), 0)
    pooled = []
    for g, win in enumerate(POOL_WINDOWS):
        ls = slice(g * POOL_GC, (g + 1) * POOL_GC)
        cnt = jnp.minimum(pos + 1, win).astype(F32)
        pooled.append(sums[g][HALO:, ls] / cnt - u[:, ls])
    o_ref[:, :D_B] = _pool_gate(pooled, gb_ref[...], pw_ref, ps_ref[...]).astype(o_ref.dtype)
    vn = _layer_norm_v(vc_ref[...], lng_ref[...])
    o_ref[:, D_B:] = _chunk_gate(vn, uc_ref[...], gc_ref[...], wm_ref, bm_ref).astype(o_ref.dtype)


def _bc_prompt(proj, n_seq, seq_len, pw, pscale, ln_g, wm, bm, tt=256):
    n_t = seq_len // tt

    def rmap(col):
        return lambda i, j: (i * n_t + j, col)

    wspec = pl.BlockSpec((4, 128, 128), lambda i, j: (0, 0, 0))
    vspec = pl.BlockSpec((1, D_B), lambda i, j: (0, 0))
    return pl.pallas_call(
        functools.partial(_bc_prompt_kernel, tt=tt),
        grid=(n_seq, n_t),
        in_specs=[pl.BlockSpec((tt, D_B), rmap(P_UB // D_B)),
                  pl.BlockSpec((tt, D_B), rmap(P_GB // D_B)),
                  pl.BlockSpec((tt, D_B), rmap(P_UC // D_B)),
                  pl.BlockSpec((tt, D_B), rmap(P_VC // D_B)),
                  pl.BlockSpec((tt, D_B), rmap(P_GC // D_B)),
                  wspec, vspec, vspec, wspec, wspec],
        out_specs=pl.BlockSpec((tt, D_B + D_C), rmap(0)),
        out_shape=jax.ShapeDtypeStruct((n_seq * seq_len, D_B + D_C), BF16),
        scratch_shapes=[pltpu.VMEM((HALO, D_B), F32)],
        compiler_params=pltpu.CompilerParams(
            dimension_semantics=("parallel", "arbitrary"), vmem_limit_bytes=VMEM_LIMIT),
        name="bc_prompt",
    )(proj, proj, proj, proj, proj, pw, pscale, ln_g, wm, bm)


def _bc_sample_kernel(buf_ref, ub_ref, gb_ref, uc_ref, vc_ref, gc_ref, pw_ref, ps_ref, lng_ref,
                      wm_ref, bm_ref, o_ref, vn_ref, *, nb, t_len):
    u3 = ub_ref[...]
    ext = jnp.concatenate([buf_ref[...], u3], axis=1)
    per = HALO + t_len
    sums = _window_sums(ext.reshape(nb * per, D_B))
    u = u3.reshape(nb * t_len, D_B)
    pooled = []
    for g, win in enumerate(POOL_WINDOWS):
        ls = slice(g * POOL_GC, (g + 1) * POOL_GC)
        s3 = sums[g].reshape(nb, per, D_B)[:, HALO:, ls].reshape(nb * t_len, POOL_GC)
        cnt = float(min(PAST_LEN + 1, win))
        pooled.append(s3 / cnt - u[:, ls])
    rows = nb * t_len
    gb = gb_ref[...].reshape(rows, D_B)
    o_ref[:, :D_B] = _pool_gate(pooled, gb, pw_ref, ps_ref[...]).astype(o_ref.dtype)
    vn = _layer_norm_v(vc_ref[...].reshape(rows, D_C), lng_ref[...])
    vn_ref[...] = vn
    uc = uc_ref[...].reshape(rows, D_C)
    gc = gc_ref[...].reshape(rows, D_C)
    o_ref[:, D_B:] = _chunk_gate(vn, uc, gc, wm_ref, bm_ref).astype(o_ref.dtype)


def _bc_sample(proj3, seq0, n_seq, t_len, buf16, pw, pscale, ln_g, wm, bm):
    nb = CHUNK // t_len
    sb0 = seq0 // nb

    def rmap(col):
        return lambda i: (sb0 + i, 0, col)

    wspec = pl.BlockSpec((4, 128, 128), lambda i: (0, 0, 0))
    vspec = pl.BlockSpec((1, D_B), lambda i: (0, 0))
    rows = nb * t_len
    return pl.pallas_call(
        functools.partial(_bc_sample_kernel, nb=nb, t_len=t_len),
        grid=(n_seq // nb,),
        in_specs=[pl.BlockSpec((nb, HALO, D_B), lambda i: (i, 0, 0)),
                  pl.BlockSpec((nb, t_len, D_B), rmap(P_UB // D_B)),
                  pl.BlockSpec((nb, t_len, D_B), rmap(P_GB // D_B)),
                  pl.BlockSpec((nb, t_len, D_B), rmap(P_UC // D_B)),
                  pl.BlockSpec((nb, t_len, D_B), rmap(P_VC // D_B)),
                  pl.BlockSpec((nb, t_len, D_B), rmap(P_GC // D_B)),
                  wspec, vspec, vspec, wspec, wspec],
        out_specs=[pl.BlockSpec((rows, D_B + D_C), lambda i: (i, 0)),
                   pl.BlockSpec((rows, D_C), lambda i: (i, 0))],
        out_shape=[jax.ShapeDtypeStruct((n_seq * t_len, D_B + D_C), BF16),
                   jax.ShapeDtypeStruct((n_seq * t_len, D_C), F32)],
        compiler_params=pltpu.CompilerParams(
            dimension_semantics=("parallel",), vmem_limit_bytes=VMEM_LIMIT),
        name="bc_sample",
    )(buf16, proj3, proj3, proj3, proj3, proj3, pw, pscale, ln_g, wm, bm)


WPREP_ROWS = 256


def _wprep_kernel(w_ref, o_ref):
    o_ref[0, :, :3 * D_A] = w_ref[0, :, :3 * D_A].astype(BF16)
    o_ref[0, :, 3 * D_A:P_LORA] = w_ref[0, :, SHIFT_W:].astype(BF16)
    o_ref[0, :, P_LORA:] = w_ref[0, :, 3 * D_A:SHIFT_W].astype(BF16)


def _prep_w_in(w_in):
    spec = pl.BlockSpec((1, WPREP_ROWS, D_INP), lambda l, i: (l, i, 0))
    return pl.pallas_call(
        _wprep_kernel,
        grid=(DEPTH, D_MODEL // WPREP_ROWS),
        in_specs=[spec],
        out_specs=spec,
        out_shape=jax.ShapeDtypeStruct((DEPTH, D_MODEL, D_INP), BF16),
        compiler_params=pltpu.CompilerParams(
            dimension_semantics=("parallel", "parallel"), vmem_limit_bytes=VMEM_LIMIT),
        name="wprep",
    )(w_in)


def kernel(x_prompt, x_sample, state_shift, state_wkv, state_pool, norm_g, final_norm_g, w_in,
           shift_mu, w0, w_up, a0, a_up, k_k, k_a, r_k, lnx_g, lnx_b, pool_w, pool_scale,
           gmlp_ln_g, gmlp_ws, gmlp_b, w_out):
    bp, seq, _ = x_prompt.shape
    bs, dseq, _ = x_sample.shape
    n_p = bp * seq
    n_s = bs * dseq
    xp = x_prompt.reshape(n_p, D_MODEL)
    xs = x_sample.reshape(n_s, D_MODEL)

    w_in_p = _prep_w_in(w_in)
    w_out_h = w_out.astype(BF16)
    mu_l = jnp.pad(shift_mu[:, 3 * D_A:], ((0, 0), (0, D_A - 2 * LORA)))
    prm = jnp.stack([shift_mu[:, :D_A], shift_mu[:, D_A:2 * D_A], shift_mu[:, 2 * D_A:3 * D_A],
                     w0, a0, k_k, k_a, r_k.reshape(DEPTH, D_A), lnx_g, lnx_b, mu_l], axis=1)
    prm = jnp.pad(prm, ((0, 0), (0, 16 - prm.shape[1]), (0, 0)))
    w2 = jnp.concatenate([w_up, a_up], axis=1)
    pw = pool_w.astype(BF16)
    tril = jnp.tril(jnp.ones((CHUNK, CHUNK), F32))
    wm_p = (gmlp_ws * tril).astype(BF16)
    bm_p = jnp.broadcast_to(gmlp_b[:, :, :, None], (DEPTH, N_GROUPS_C, CHUNK, GC))
    nb_s = CHUNK // dseq
    eye_b = jnp.eye(nb_s, dtype=F32)
    ws_small = gmlp_ws[:, :, :dseq, :dseq] * tril[:dseq, :dseq]
    wm_s = jnp.einsum('ab,lgts->lgatbs', eye_b, ws_small).reshape(DEPTH, N_GROUPS_C, CHUNK, CHUNK)
    wm_s = wm_s.astype(BF16)
    bm_s = jnp.broadcast_to(jnp.tile(gmlp_b[:, :, :dseq], (1, 1, nb_s))[:, :, :, None],
                            (DEPTH, N_GROUPS_C, CHUNK, GC))

    ssh_main = state_shift[:, :, :3 * D_A].reshape(DEPTH, bs, 3, 1, D_A).transpose(0, 2, 1, 3, 4)
    ssh_lora = state_shift[:, :, 3 * D_A:].reshape(DEPTH, bs, 1, PAIR)
    wkv_s = state_wkv.reshape(DEPTH, bs, N_PAIRS, PAIR, HEAD_A)
    buf16 = jnp.pad(state_pool, ((0, 0), (0, 0), (HALO - POOL_BUF, 0), (0, 0)))

    p_shift, p_wkv, p_pool, s_shift, s_pool, s_v = [], [], [], [], [], []
    for l in range(DEPTH):
        final = l == DEPTH - 1
        g_out = final_norm_g[None] if final else norm_g[l + 1][None]
        bc_w = (pw[l], pool_scale[l][None], gmlp_ln_g[l][None])

        proj_p = _inproj(xp, norm_g[0][None], w_in_p, 0) if l == 0 else _inproj_h(hp, w_in_p, l, 2048)
        ya_p, wk_p, shm_p, shl_p = _wkv_prompt(proj_p, bp, seq, prm, w2, l)
        cb_p = _bc_prompt(proj_p, bp, seq, *bc_w, wm_p[l], bm_p[l])
        xp = _outproj(ya_p, cb_p, w_out_h, l, xp, g_out, final)
        if not final:
            xp, hp = xp

        proj_s = _inproj(xs, norm_g[0][None], w_in_p, 0) if l == 0 else _inproj_h(hs, w_in_p, l, 1024)
        ya_s, wkv_s, shm_s, shl_s = _wkv_sample(proj_s, bs, dseq, ROWS // dseq, ssh_main, ssh_lora,
                                                wkv_s, prm, w2, l)
        cb_s, vn_s = _bc_sample(proj_s.reshape(bs, dseq, D_INP), 0, bs, dseq, buf16[l], *bc_w,
                                wm_s[l], bm_s[l])
        xs = _outproj(ya_s, cb_s, w_out_h, l, xs, g_out, final)
        if not final:
            xs, hs = xs

        p_shift.append(jnp.concatenate([shm_p[0, :, 0], shm_p[1, :, 0], shm_p[2, :, 0], shl_p[:, 0]], axis=-1))
        s_shift.append(jnp.concatenate([shm_s[0, :, 0], shm_s[1, :, 0], shm_s[2, :, 0], shl_s[:, 0]], axis=-1))
        p_wkv.append(wk_p.reshape(bp, N_HEADS_A, HEAD_A, HEAD_A))
        p_pool.append(jnp.stack([proj_p[(b + 1) * seq - POOL_BUF:(b + 1) * seq, P_UB:P_UB + D_B]
                                 for b in range(bp)]))
        ub_s = proj_s[:, P_UB:P_UB + D_B].reshape(bs, dseq, D_B)
        s_pool.append(jnp.concatenate([state_pool[l], ub_s], axis=1)[:, -POOL_BUF:])
        s_v.append(vn_s.reshape(bs, dseq, D_C))

    y_prompt = xp.reshape(bp, seq, D_MODEL)
    y_sample = xs.reshape(bs, dseq, D_MODEL)
    s_wkv = wkv_s.reshape(DEPTH, bs, N_HEADS_A, HEAD_A, HEAD_A)
    return (y_prompt, y_sample, jnp.stack(p_shift), jnp.stack(p_wkv), jnp.stack(p_pool),
            jnp.stack(s_shift), s_wkv, jnp.stack(s_pool), jnp.stack(s_v))
```

```python
import functools

import jax
import jax.numpy as jnp
import numpy as np
from jax import lax
from jax.experimental import pallas as pl
from jax.experimental.pallas import tpu as pltpu

F32 = jnp.float32
BF16 = jnp.bfloat16

D_MODEL = 2048
DEPTH = 4
PAST_LEN = 16384
D_A = 1024
HEAD_A = 64
N_HEADS_A = 16
LORA = 64
D_B = 512
POOL_WINDOWS = (2, 4, 8, 16)
POOL_GC = 128
POOL_BUF = 15
D_C = 512
N_GROUPS_C = 4
GC = 128
CHUNK = 128
SHIFT_W = 3 * D_A + 2 * LORA
EPS = 1e-6
GN_EPS = HEAD_A * 1e-5
LN_EPS = 1e-5

P_R, P_K, P_V, P_GA = 0, 1024, 2048, 3072
P_UB, P_GB, P_UC, P_VC, P_GC = 4096, 4608, 5120, 5632, 6144
P_LORA = 6656
D_INP = 6784

PAIR = 128
N_PAIRS = D_A // PAIR
ROWS = 64

VMEM_LIMIT = 52 * 1024 * 1024
HI = lax.Precision.HIGHEST


def _dot(a, b, prec=None):
    return jnp.dot(a, b, precision=prec, preferred_element_type=F32)


def _dot_nt(a, b, prec=None):
    return lax.dot_general(a, b, (((1,), (1,)), ((), ())), precision=prec,
                           preferred_element_type=F32)


def _dot_tn(a, b, prec=None):
    return lax.dot_general(a, b, (((0,), (0,)), ((), ())), precision=prec,
                           preferred_element_type=F32)


def _split(x, n):
    pieces = []
    rem = x
    for i in range(n):
        hi = rem.astype(BF16)
        pieces.append(hi)
        if i + 1 < n:
            rem = rem - hi.astype(F32)
    return pieces


_CONTRACT = {"nn": (1, 0), "nt": (1, 1), "tn": (0, 0)}


def _mm(a, b, mode="nn", pa=1, pb=1):
    ca, cb = _CONTRACT[mode]
    sa, sb = _split(a, pa), _split(b, pb)
    terms = [(i, j) for i in range(pa) for j in range(pb) if i + j < max(pa, pb)]
    lhs = jnp.concatenate([sa[i] for i, _ in terms], axis=ca) if len(terms) > 1 else sa[0]
    rhs = jnp.concatenate([sb[j] for _, j in terms], axis=cb) if len(terms) > 1 else sb[0]
    return lax.dot_general(lhs, rhs, (((ca,), (cb,)), ((), ())), preferred_element_type=F32)


PREC_LORA = (1, 1)
PREC_CUM = (1, 2)
PREC_SEG = (1, 1)
PREC_G = (1, 1)
PREC_INV = (1, 1)
PREC_X2 = (1, 1)
PREC_WU = (1, 1)
PREC_US = (1, 1)
PREC_UPD = (1, 1)
PREC_Y = (1, 1)


def _norm_kernel(x_ref, g_ref, o_ref):
    x = x_ref[...]
    ms = jnp.mean(x * x, axis=-1, keepdims=True)
    o_ref[...] = ((x * lax.rsqrt(ms + EPS)) * g_ref[...]).astype(BF16)


def _norm_rows(x, g, tm=1024):
    m = x.shape[0]
    return pl.pallas_call(
        _norm_kernel,
        grid=(m // tm,),
        in_specs=[pl.BlockSpec((tm, D_MODEL), lambda i: (i, 0)),
                  pl.BlockSpec((1, D_MODEL), lambda i: (0, 0))],
        out_specs=pl.BlockSpec((tm, D_MODEL), lambda i: (i, 0)),
        out_shape=jax.ShapeDtypeStruct((m, D_MODEL), BF16),
        compiler_params=pltpu.CompilerParams(
            dimension_semantics=("parallel",), vmem_limit_bytes=VMEM_LIMIT),
        name="norm_rows",
    )(x, g)


def _inproj_h_kernel(h_ref, w_ref, o_ref):
    o_ref[...] = _dot(h_ref[...], w_ref[0])


def _inproj_h(h, w, layer, tm, tn=768):
    m = h.shape[0]
    return pl.pallas_call(
        _inproj_h_kernel,
        grid=(m // tm, pl.cdiv(D_INP, tn)),
        in_specs=[pl.BlockSpec((tm, D_MODEL), lambda i, j: (i, 0)),
                  pl.BlockSpec((1, D_MODEL, tn), lambda i, j: (layer, 0, j))],
        out_specs=pl.BlockSpec((tm, tn), lambda i, j: (i, j)),
        out_shape=jax.ShapeDtypeStruct((m, D_INP), F32),
        compiler_params=pltpu.CompilerParams(
            dimension_semantics=("parallel", "arbitrary"), vmem_limit_bytes=VMEM_LIMIT),
        name="inproj_h",
    )(h, w)


def _outproj_kernel(ca_ref, cb_ref, wa_ref, wb_ref, x_ref, g_ref, *o_refs, final):
    y = _dot(ca_ref[...], wa_ref[0]) + _dot(cb_ref[...], wb_ref[0])
    out = x_ref[...] + y
    ms = jnp.mean(out * out, axis=-1, keepdims=True)
    normed = (out * lax.rsqrt(ms + EPS)) * g_ref[...]
    if final:
        o_refs[0][...] = normed
    else:
        o_refs[0][...] = out
        o_refs[1][...] = normed.astype(BF16)


def _outproj(cat_a, cat_b, w, layer, x, g, final, tm=512):
    m = x.shape[0]
    half = D_MODEL // 2
    row_spec = pl.BlockSpec((tm, D_MODEL), lambda i: (i, 0))
    f32_out = jax.ShapeDtypeStruct((m, D_MODEL), F32)
    return pl.pallas_call(
        functools.partial(_outproj_kernel, final=final),
        grid=(m // tm,),
        in_specs=[pl.BlockSpec((tm, half), lambda i: (i, 0)),
                  pl.BlockSpec((tm, half), lambda i: (i, 0)),
                  pl.BlockSpec((1, half, D_MODEL), lambda i: (layer, 0, 0)),
                  pl.BlockSpec((1, half, D_MODEL), lambda i: (layer, 1, 0)),
                  pl.BlockSpec((tm, D_MODEL), lambda i: (i, 0)),
                  pl.BlockSpec((1, D_MODEL), lambda i: (0, 0))],
        out_specs=row_spec if final else [row_spec, row_spec],
        out_shape=f32_out if final else [f32_out, jax.ShapeDtypeStruct((m, D_MODEL), BF16)],
        compiler_params=pltpu.CompilerParams(
            dimension_semantics=("parallel",), vmem_limit_bytes=VMEM_LIMIT),
        name="outproj",
    )(cat_a, cat_b, w, w, x, g)


PR_MU_R, PR_MU_K, PR_MU_V, PR_W0, PR_A0, PR_KK, PR_KA, PR_RK, PR_LNG, PR_LNB, PR_MU_L = range(11)


def _softplus(z):
    return jnp.maximum(z, 0.0) + jnp.log(1.0 + jnp.exp(-jnp.abs(z)))


def _sigmoid(z):
    return 1.0 / (1.0 + jnp.exp(-z))


def _silu(z):
    return z * _sigmoid(z)


def _run(gen):
    try:
        while True:
            next(gen)
    except StopIteration as e:
        return e.value


def _zip_run(main, prep, pattern):
    done = {}

    def step(gen, key):
        if key not in done:
            try:
                next(gen)
            except StopIteration as e:
                done[key] = e.value

    for ch in pattern:
        step(main if ch == "m" else prep, ch)
    while "m" not in done:
        step(main, "m")
    while "p" not in done:
        step(prep, "p")
    return done["m"], done["p"]


class _ScanConsts:
    def __init__(self, C, NB):
        rows = ROWS
        self.C, self.NB = C, NB
        row_id = lax.broadcasted_iota(jnp.int32, (rows, 1), 0)
        self.first = (row_id % C) == 0
        self.lo_half = lax.broadcasted_iota(jnp.int32, (rows, PAIR), 1) < HEAD_A
        ri = lax.broadcasted_iota(jnp.int32, (rows, rows), 0)
        ci = lax.broadcasted_iota(jnp.int32, (rows, rows), 1)
        self.tri = jnp.where(((ri // C) == (ci // C)) & (ri >= ci), 1.0, 0.0)
        pr = lax.broadcasted_iota(jnp.int32, (PAIR, PAIR), 0)
        pc = lax.broadcasted_iota(jnp.int32, (PAIR, PAIR), 1)
        self.same_head = (pr // HEAD_A) == (pc // HEAD_A)
        self.seg = jnp.where(self.same_head, 1.0, 0.0)
        r2 = lax.broadcasted_iota(jnp.int32, (2 * rows, 2 * rows), 0)
        c2 = lax.broadcasted_iota(jnp.int32, (2 * rows, 2 * rows), 1)
        same_blk = (r2 // C) == (c2 // C)
        self.m_strict = same_blk & (r2 > c2)
        m_incl = same_blk & (r2 >= c2)
        self.m_incl2 = jnp.concatenate([m_incl, m_incl], axis=1)
        self.eye = jnp.where(r2 == c2, 1.0, 0.0)


def _bmask(m, xb):
    return jnp.where(m, xb, jnp.zeros_like(xb))


_PP_BF16 = ("at", "rt", "kh", "bh", "kg", "bg", "vs")
_PP_F32 = ("bon", "v", "sg")


def _prep_steps(raw, prev, prm, w2, cs):
    C, NB, rows = cs.C, cs.NB, ROWS
    r_raw, k_raw, v_raw, g_raw, l_raw = raw
    prev_r, prev_k, prev_v, prev_l = prev

    def prow(i):
        return prm[i:i + 1, :]

    def shift_mix(x, prev_rows, mu):
        rolled = pltpu.roll(x, 1, 0)
        pr_ = jnp.concatenate([jnp.broadcast_to(q, (C, x.shape[1])) for q in prev_rows], axis=0)
        shifted = jnp.where(cs.first, pr_, rolled)
        return x + (shifted - x) * mu

    xl = shift_mix(l_raw, prev_l, prow(PR_MU_L)[:, :PAIR])
    lora_w = _mm(jnp.where(cs.lo_half, jnp.tanh(xl), 0.0), w2, "nn", *PREC_LORA)
    lora_a = _mm(jnp.where(cs.lo_half, 0.0, xl), w2, "nn", *PREC_LORA)
    yield
    r = shift_mix(r_raw, prev_r, prow(PR_MU_R))
    k = shift_mix(k_raw, prev_k, prow(PR_MU_K))
    v = shift_mix(v_raw, prev_v, prow(PR_MU_V))
    w_log = -_softplus(-(prow(PR_W0) + lora_w)) - 0.5
    dec = jnp.exp(w_log)
    a = _sigmoid(prow(PR_A0) + lora_a)
    cum = _mm(cs.tri, -dec, "nn", *PREC_CUM)
    yield
    kk_raw = k * prow(PR_KK)
    k2 = k * (1.0 + (a - 1.0) * prow(PR_KA))
    rkr = r * k2 * prow(PR_RK)
    sls = [slice(p * PAIR, (p + 1) * PAIR) for p in range(N_PAIRS)]
    s0 = [_mm(jnp.concatenate([kk_raw[:, sl] * kk_raw[:, sl], rkr[:, sl]], axis=0), cs.seg, "nn", *PREC_SEG)
          for sl in sls]
    yield
    tot = jnp.concatenate(
        [jnp.broadcast_to(cum[(b + 1) * C - 1:(b + 1) * C, :], (C, D_A)) for b in range(NB)], axis=0)
    g_incl = jnp.exp(cum)
    g_excl = jnp.exp(cum + dec)
    g_inv = jnp.exp(-cum)
    g_rest = jnp.exp(tot - cum)
    g_tot = jnp.exp(tot)
    sg = _silu(g_raw)

    def stack(x):
        xb = x.astype(BF16)
        zb = jnp.zeros_like(xb)
        return jnp.concatenate([jnp.where(cs.lo_half, xb, zb), jnp.where(cs.lo_half, zb, xb)], axis=0)

    pp = {n: [] for n in _PP_BF16 + _PP_F32 + ("gtot",)}
    for sl, q in zip(sls, s0):
        kkp = kk_raw[:, sl] * lax.rsqrt(jnp.maximum(q[:rows], 1e-12))
        kka = kkp * a[:, sl]
        pp["at"].append(stack(-kkp * g_excl[:, sl]))
        pp["rt"].append(stack(r[:, sl] * g_incl[:, sl]))
        pp["kh"].append(stack(k2[:, sl] * g_inv[:, sl]))
        pp["bh"].append(stack(kka * g_inv[:, sl]))
        pp["kg"].append(stack(k2[:, sl] * g_rest[:, sl]))
        pp["bg"].append(stack(kka * g_rest[:, sl]))
        pp["vs"].append(stack(v[:, sl]))
        pp["bon"].append(q[rows:])
        pp["v"].append(v[:, sl])
        pp["sg"].append(sg[:, sl])
        pp["gtot"].append(jnp.concatenate([g_tot[b * C:b * C + 1, sl] for b in range(NB)], axis=0))
    return pp


def _scan_steps(pp, st_ref, o_ref, row0, prm, cs):
    C, NB, rows = cs.C, cs.NB, ROWS
    n_rounds = int(np.log2(C))
    n2 = 2 * rows
    nb2 = 2 * C
    npair = N_PAIRS

    def prow(i):
        return prm[i:i + 1, :]

    def seq_rows(x, b):
        if NB == 1:
            return x
        xf = x.astype(F32)
        out = jnp.concatenate([xf[b * C:(b + 1) * C], xf[rows + b * C:rows + (b + 1) * C]], axis=0)
        return out.astype(x.dtype)

    def unseq_rows(pieces):
        if NB == 1:
            return pieces[0]
        pf = [q.astype(F32) for q in pieces]
        out = jnp.concatenate([q[:C] for q in pf] + [q[C:] for q in pf], axis=0)
        return out.astype(pieces[0].dtype)

    at_s, rt_s, kh_s, bh_s, kg_s, bg_s, v_s = (pp[n] for n in _PP_BF16)

    gm = [_mm(jnp.concatenate([x, y_], axis=0), jnp.concatenate([z, w_], axis=0), "nt", *PREC_G)
          for x, y_, z, w_ in zip(at_s, rt_s, kh_s, bh_s)]
    yield
    gmb = [q.astype(BF16) for q in gm]
    a_ak = [_bmask(cs.m_strict, q[:n2, :n2]) for q in gmb]
    a_ab = [jnp.where(cs.m_strict, q[:n2, n2:], 0.0) for q in gm]
    a_r = [_bmask(cs.m_incl2, q[n2:, :]) for q in gmb]

    x2 = [_mm(q, vs, "nn", *PREC_X2) for q, vs in zip(a_ak, v_s)]
    pwb = [_bmask(cs.m_strict, q[:n2, n2:]) for q in gmb]
    pw = [_mm(q, q, "nn", *PREC_INV) for q in pwb]
    t_inv = [cs.eye + q for q in a_ab]
    yield
    for rd in range(1, n_rounds):
        pwb = [q.astype(BF16) for q in pw]
        tb = [t.astype(BF16) for t in t_inv]
        if rd < n_rounds - 1:
            res = [_mm(jnp.concatenate([q, t], axis=0), q, "nn", *PREC_INV) for q, t in zip(pwb, tb)]
            t_inv = [t + q[n2:] for t, q in zip(t_inv, res)]
            pw = [q[:n2] for q in res]
        else:
            t_inv = [t + _mm(t_, q, "nn", *PREC_INV) for q, t, t_ in zip(pwb, t_inv, tb)]
        yield

    wu = [_mm(t, jnp.concatenate([x, y_.astype(BF16)], axis=1), "nn", *PREC_WU)
          for t, x, y_ in zip(t_inv, at_s, x2)]
    yield
    w_t = [q[:, :PAIR].astype(BF16) for q in wu]
    u_t = [q[:, PAIR:] for q in wu]

    s_old = [[st_ref[p * NB + b] for b in range(NB)] for p in range(npair)]
    us = [[_mm(jnp.concatenate([seq_rows(w_t[p], b), seq_rows(rt_s[p], b)], axis=0),
               s_old[p][b], "nt", *PREC_US) for b in range(NB)] for p in range(npair)]
    yield
    u_b = [[(us[p][b][:nb2] + seq_rows(u_t[p], b)).astype(BF16) for b in range(NB)] for p in range(npair)]
    for p in range(npair):
        for b in range(NB):
            upd = _mm(jnp.concatenate([seq_rows(v_s[p], b), u_b[p][b]], axis=0),
                      jnp.concatenate([seq_rows(kg_s[p], b), seq_rows(bg_s[p], b)], axis=0),
                      "tn", *PREC_UPD)
            st_ref[p * NB + b] = s_old[p][b] * pp["gtot"][p][b:b + 1, :] + upd
    yield
    y2 = [unseq_rows([us[p][b][nb2:] for b in range(NB)])
          + _mm(a_r[p], jnp.concatenate([v_s[p], unseq_rows(u_b[p])], axis=0), "nn", *PREC_Y)
          for p in range(npair)]
    yield
    y = [q[:rows] + q[rows:] for q in y2]
    ym = [_mm(q, cs.seg, "nn", *PREC_SEG) * (1.0 / HEAD_A) for q in y]
    yield
    yc = [q - m for q, m in zip(y, ym)]
    yv = [_mm(q * q, cs.seg, "nn", *PREC_SEG) * (1.0 / HEAD_A) for q in yc]
    yield
    for p in range(npair):
        sl = slice(p * PAIR, (p + 1) * PAIR)
        yn = yc[p] * lax.rsqrt(yv[p] + GN_EPS) * prow(PR_LNG)[:, sl] + prow(PR_LNB)[:, sl]
        o_ref[row0:row0 + rows, sl] = ((yn + pp["bon"][p] * pp["v"][p]) * pp["sg"][p]).astype(o_ref.dtype)


def _write_state(st_ref, sout_ref, cs):
    for b in range(cs.NB):
        for p in range(N_PAIRS):
            sm = jnp.where(cs.same_head, st_ref[p * cs.NB + b], 0.0)
            sout_ref[0, b, p] = sm[:, :HEAD_A] + sm[:, HEAD_A:]


def _wkv_kernel(r_ref, k_ref, v_ref, g_ref, l_ref, sh_ref, shl_ref, s0_ref, prm_ref, w2_ref,
                o_ref, sout_ref, shm_out_ref, shl_out_ref, st_ref, *, C, NB):
    cs = _ScanConsts(C, NB)
    for b in range(NB):
        for p in range(N_PAIRS):
            s = s0_ref[0, b, p]
            s2 = jnp.concatenate([s, s], axis=1)
            st_ref[p * NB + b] = jnp.where(cs.same_head, s2, 0.0)

    raw = (r_ref[...], k_ref[...], v_ref[...], g_ref[...], l_ref[...])
    prev = ([sh_ref[0, 0, b] for b in range(NB)], [sh_ref[0, 1, b] for b in range(NB)],
            [sh_ref[0, 2, b] for b in range(NB)], [shl_ref[0, b] for b in range(NB)])
    pp = _run(_prep_steps(raw, prev, prm_ref[0], w2_ref[0], cs))
    _run(_scan_steps(pp, st_ref, o_ref, 0, prm_ref[0], cs))

    for b in range(NB):
        last = (b + 1) * C - 1
        shm_out_ref[0, b] = raw[0][last:last + 1, :]
        shm_out_ref[1, b] = raw[1][last:last + 1, :]
        shm_out_ref[2, b] = raw[2][last:last + 1, :]
        shl_out_ref[b] = raw[4][last:last + 1, :]
    _write_state(st_ref, sout_ref, cs)


def _wkv_sample(proj, n_seq, C, NB, sh_main, sh_lora, s0, prm, w2, layer):
    n_groups = n_seq // NB
    assert NB * C == ROWS

    def rmap(col):
        return lambda i: (i, col)

    state_spec = pl.BlockSpec((1, NB, N_PAIRS, PAIR, HEAD_A), lambda i: (layer, i, 0, 0, 0))
    return pl.pallas_call(
        functools.partial(_wkv_kernel, C=C, NB=NB),
        grid=(n_groups,),
        in_specs=[pl.BlockSpec((ROWS, D_A), rmap(P_R // D_A)),
                  pl.BlockSpec((ROWS, D_A), rmap(P_K // D_A)),
                  pl.BlockSpec((ROWS, D_A), rmap(P_V // D_A)),
                  pl.BlockSpec((ROWS, D_A), rmap(P_GA // D_A)),
                  pl.BlockSpec((ROWS, PAIR), rmap(P_LORA // PAIR)),
                  pl.BlockSpec((1, 3, NB, 1, D_A), lambda i: (layer, 0, i, 0, 0)),
                  pl.BlockSpec((1, NB, 1, PAIR), lambda i: (layer, i, 0, 0)),
                  state_spec,
                  pl.BlockSpec((1, 16, D_A), lambda i: (layer, 0, 0)),
                  pl.BlockSpec((1, PAIR, D_A), lambda i: (layer, 0, 0))],
        out_specs=[pl.BlockSpec((ROWS, D_A), lambda i: (i, 0)),
                   state_spec,
                   pl.BlockSpec((3, NB, 1, D_A), lambda i: (0, i, 0, 0)),
                   pl.BlockSpec((NB, 1, PAIR), lambda i: (i, 0, 0))],
        out_shape=[jax.ShapeDtypeStruct((n_seq * C, D_A), BF16),
                   jax.ShapeDtypeStruct(s0.shape, F32),
                   jax.ShapeDtypeStruct((3, n_seq, 1, D_A), F32),
                   jax.ShapeDtypeStruct((n_seq, 1, PAIR), F32)],
        scratch_shapes=[pltpu.VMEM((N_PAIRS * NB, PAIR, PAIR), F32)],
        input_output_aliases={7: 1},
        compiler_params=pltpu.CompilerParams(
            dimension_semantics=("parallel",), vmem_limit_bytes=VMEM_LIMIT),
        name="wkv_c%d" % C,
    )(proj, proj, proj, proj, proj, sh_main, sh_lora, s0, prm, w2)


_PIPE_PATTERN = "mpmmpmmpmm"


def _wkv_pipe_kernel(rc_ref, kc_ref, vc_ref, gc_ref, lc_ref, rn_ref, kn_ref, vn_ref, gn_ref, ln_ref,
                     prm_ref, w2_ref, o_ref, sout_ref, shm_out_ref, shl_out_ref,
                     st_ref, pb_ref, pf_ref, pg_ref, *, n_steps):
    k = pl.program_id(1)
    rows = ROWS
    cs = _ScanConsts(rows, 1)
    prm = prm_ref[0]
    w2 = w2_ref[0]

    def raw_rows(refs, lo):
        return tuple(q[lo:lo + rows, :] for q in refs)

    def prev_rows(refs, row):
        rr, kr, vr, _, lr = refs
        return ([rr[row:row + 1, :]], [kr[row:row + 1, :]], [vr[row:row + 1, :]], [lr[row:row + 1, :]])

    def store_pp(pp):
        for p in range(N_PAIRS):
            for j, n in enumerate(_PP_BF16):
                pb_ref[p, j] = pp[n][p]
            for j, n in enumerate(_PP_F32):
                pf_ref[p, j] = pp[n][p]
            pg_ref[p] = pp["gtot"][p]

    def load_pp():
        pp = {n: [pb_ref[p, j] for p in range(N_PAIRS)] for j, n in enumerate(_PP_BF16)}
        pp.update({n: [pf_ref[p, j] for p in range(N_PAIRS)] for j, n in enumerate(_PP_F32)})
        pp["gtot"] = [pg_ref[p] for p in range(N_PAIRS)]
        return pp

    cur = (rc_ref, kc_ref, vc_ref, gc_ref, lc_ref)
    nxt = (rn_ref, kn_ref, vn_ref, gn_ref, ln_ref)

    @pl.when(k == 0)
    def _():
        st_ref[...] = jnp.zeros_like(st_ref)
        zero = ([jnp.zeros((1, D_A), F32)], [jnp.zeros((1, D_A), F32)], [jnp.zeros((1, D_A), F32)],
                [jnp.zeros((1, PAIR), F32)])
        store_pp(_run(_prep_steps(raw_rows(cur, 0), zero, prm, w2, cs)))

    pp_a = load_pp()
    _, pp_b = _zip_run(_scan_steps(pp_a, st_ref, o_ref, 0, prm, cs),
                       _prep_steps(raw_rows(cur, rows), prev_rows(cur, rows - 1), prm, w2, cs),
                       _PIPE_PATTERN)
    _, pp_n = _zip_run(_scan_steps(pp_b, st_ref, o_ref, rows, prm, cs),
                       _prep_steps(raw_rows(nxt, 0), prev_rows(cur, 2 * rows - 1), prm, w2, cs),
                       _PIPE_PATTERN)
    store_pp(pp_n)

    last = 2 * rows - 1
    shm_out_ref[0, 0] = rc_ref[last:last + 1, :]
    shm_out_ref[1, 0] = kc_ref[last:last + 1, :]
    shm_out_ref[2, 0] = vc_ref[last:last + 1, :]
    shl_out_ref[0] = lc_ref[last:last + 1, :]

    @pl.when(k == n_steps - 1)
    def _():
        _write_state(st_ref, sout_ref, cs)


def _wkv_prompt(proj, n_seq, seq_len, prm, w2, layer):
    n_steps = seq_len // (2 * ROWS)
    n_chunks = seq_len // ROWS

    def cmap(col):
        return lambda i, k: (i * n_steps + k, col)

    def nmap(col):
        return lambda i, k: (i * n_chunks + jnp.minimum(2 * k + 2, n_chunks - 1), col)

    def specs(rows_, m):
        return [pl.BlockSpec((rows_, D_A), m(P_R // D_A)),
                pl.BlockSpec((rows_, D_A), m(P_K // D_A)),
                pl.BlockSpec((rows_, D_A), m(P_V // D_A)),
                pl.BlockSpec((rows_, D_A), m(P_GA // D_A)),
                pl.BlockSpec((rows_, PAIR), m(P_LORA // PAIR))]

    return pl.pallas_call(
        functools.partial(_wkv_pipe_kernel, n_steps=n_steps),
        grid=(n_seq, n_steps),
        in_specs=specs(2 * ROWS, cmap) + specs(ROWS, nmap) + [
            pl.BlockSpec((1, 16, D_A), lambda i, k: (layer, 0, 0)),
            pl.BlockSpec((1, PAIR, D_A), lambda i, k: (layer, 0, 0))],
        out_specs=[pl.BlockSpec((2 * ROWS, D_A), lambda i, k: (i * n_steps + k, 0)),
                   pl.BlockSpec((1, 1, N_PAIRS, PAIR, HEAD_A), lambda i, k: (0, i, 0, 0, 0)),
                   pl.BlockSpec((3, 1, 1, D_A), lambda i, k: (0, i, 0, 0)),
                   pl.BlockSpec((1, 1, PAIR), lambda i, k: (i, 0, 0))],
        out_shape=[jax.ShapeDtypeStruct((n_seq * seq_len, D_A), BF16),
                   jax.ShapeDtypeStruct((1, n_seq, N_PAIRS, PAIR, HEAD_A), F32),
                   jax.ShapeDtypeStruct((3, n_seq, 1, D_A), F32),
                   jax.ShapeDtypeStruct((n_seq, 1, PAIR), F32)],
        scratch_shapes=[pltpu.VMEM((N_PAIRS, PAIR, PAIR), F32),
                        pltpu.VMEM((N_PAIRS, len(_PP_BF16), 2 * ROWS, PAIR), BF16),
                        pltpu.VMEM((N_PAIRS, len(_PP_F32), ROWS, PAIR), F32),
                        pltpu.VMEM((N_PAIRS, 1, PAIR), F32)],
        compiler_params=pltpu.CompilerParams(
            dimension_semantics=("parallel", "arbitrary"), vmem_limit_bytes=VMEM_LIMIT),
        name="wkv_pipe",
    )(*([proj] * 10), prm, w2)


HALO = 16


def _window_sums(ext):
    w2 = ext + pltpu.roll(ext, 1, 0)
    w4 = w2 + pltpu.roll(w2, 2, 0)
    w8 = w4 + pltpu.roll(w4, 4, 0)
    w16 = w8 + pltpu.roll(w8, 8, 0)
    return (w2, w4, w8, w16)


def _pool_gate(pooled_groups, gb, pw_ref, pscale):
    mixed = [_dot(pg.astype(BF16), pw_ref[g]) for g, pg in enumerate(pooled_groups)]
    yb = jnp.concatenate(mixed, axis=1) * pscale
    return yb * _silu(gb)


def _layer_norm_v(vc, ln_g):
    vm = jnp.mean(vc, axis=-1, keepdims=True)
    d = vc - vm
    vv = jnp.mean(d * d, axis=-1, keepdims=True)
    return d * lax.rsqrt(vv + LN_EPS) * ln_g


def _chunk_gate(vn, uc, gc, wm_ref, bm_ref):
    n_rows = vn.shape[0]
    vnb = vn.astype(BF16)
    outs = []
    for j in range(n_rows // CHUNK):
        rs = slice(j * CHUNK, (j + 1) * CHUNK)
        mix = [_dot(wm_ref[g], vnb[rs, g * GC:(g + 1) * GC]) + bm_ref[g] for g in range(N_GROUPS_C)]
        outs.append(jnp.concatenate(mix, axis=1))
    mix = outs[0] if len(outs) == 1 else jnp.concatenate(outs, axis=0)
    return uc * mix * _silu(gc)


def _bc_prompt_kernel(ub_ref, gb_ref, uc_ref, vc_ref, gc_ref, pw_ref, ps_ref, lng_ref, wm_ref, bm_ref,
                      o_ref, halo_ref, *, tt):
    j = pl.program_id(1)

    @pl.when(j == 0)
    def _():
        halo_ref[...] = jnp.zeros_like(halo_ref)

    u = ub_ref[...]
    ext = jnp.concatenate([halo_ref[...], u], axis=0)
    halo_ref[...] = u[tt - HALO:, :]
    sums = _window_sums(ext)
    pos = j * tt + lax.broadcasted_iota(jnp.int32, (tt, 1), 0)
    pooled = []
    for g, win in enumerate(POOL_WINDOWS):
        ls = slice(g * POOL_GC, (g + 1) * POOL_GC)
        cnt = jnp.minimum(pos + 1, win).astype(F32)
        pooled.append(sums[g][HALO:, ls] / cnt - u[:, ls])
    o_ref[:, :D_B] = _pool_gate(pooled, gb_ref[...], pw_ref, ps_ref[...]).astype(o_ref.dtype)
    vn = _layer_norm_v(vc_ref[...], lng_ref[...])
    o_ref[:, D_B:] = _chunk_gate(vn, uc_ref[...], gc_ref[...], wm_ref, bm_ref).astype(o_ref.dtype)


def _bc_prompt(proj, n_seq, seq_len, pw, pscale, ln_g, wm, bm, tt=256):
    n_t = seq_len // tt

    def rmap(col):
        return lambda i, j: (i * n_t + j, col)

    wspec = pl.BlockSpec((4, 128, 128), lambda i, j: (0, 0, 0))
    vspec = pl.BlockSpec((1, D_B), lambda i, j: (0, 0))
    return pl.pallas_call(
        functools.partial(_bc_prompt_kernel, tt=tt),
        grid=(n_seq, n_t),
        in_specs=[pl.BlockSpec((tt, D_B), rmap(P_UB // D_B)),
                  pl.BlockSpec((tt, D_B), rmap(P_GB // D_B)),
                  pl.BlockSpec((tt, D_B), rmap(P_UC // D_B)),
                  pl.BlockSpec((tt, D_B), rmap(P_VC // D_B)),
                  pl.BlockSpec((tt, D_B), rmap(P_GC // D_B)),
                  wspec, vspec, vspec, wspec, wspec],
        out_specs=pl.BlockSpec((tt, D_B + D_C), rmap(0)),
        out_shape=jax.ShapeDtypeStruct((n_seq * seq_len, D_B + D_C), BF16),
        scratch_shapes=[pltpu.VMEM((HALO, D_B), F32)],
        compiler_params=pltpu.CompilerParams(
            dimension_semantics=("parallel", "arbitrary"), vmem_limit_bytes=VMEM_LIMIT),
        name="bc_prompt",
    )(proj, proj, proj, proj, proj, pw, pscale, ln_g, wm, bm)


def _bc_sample_kernel(buf_ref, ub_ref, gb_ref, uc_ref, vc_ref, gc_ref, pw_ref, ps_ref, lng_ref,
                      wm_ref, bm_ref, o_ref, vn_ref, *, nb, t_len):
    u3 = ub_ref[...]
    ext = jnp.concatenate([buf_ref[...], u3], axis=1)
    per = HALO + t_len
    sums = _window_sums(ext.reshape(nb * per, D_B))
    u = u3.reshape(nb * t_len, D_B)
    pooled = []
    for g, win in enumerate(POOL_WINDOWS):
        ls = slice(g * POOL_GC, (g + 1) * POOL_GC)
        s3 = sums[g].reshape(nb, per, D_B)[:, HALO:, ls].reshape(nb * t_len, POOL_GC)
        cnt = float(min(PAST_LEN + 1, win))
        pooled.append(s3 / cnt - u[:, ls])
    rows = nb * t_len
    gb = gb_ref[...].reshape(rows, D_B)
    o_ref[:, :D_B] = _pool_gate(pooled, gb, pw_ref, ps_ref[...]).astype(o_ref.dtype)
    vn = _layer_norm_v(vc_ref[...].reshape(rows, D_C), lng_ref[...])
    vn_ref[...] = vn
    uc = uc_ref[...].reshape(rows, D_C)
    gc = gc_ref[...].reshape(rows, D_C)
    o_ref[:, D_B:] = _chunk_gate(vn, uc, gc, wm_ref, bm_ref).astype(o_ref.dtype)


def _bc_sample(proj3, seq0, n_seq, t_len, buf16, pw, pscale, ln_g, wm, bm):
    nb = CHUNK // t_len
    sb0 = seq0 // nb

    def rmap(col):
        return lambda i: (sb0 + i, 0, col)

    wspec = pl.BlockSpec((4, 128, 128), lambda i: (0, 0, 0))
    vspec = pl.BlockSpec((1, D_B), lambda i: (0, 0))
    rows = nb * t_len
    return pl.pallas_call(
        functools.partial(_bc_sample_kernel, nb=nb, t_len=t_len),
        grid=(n_seq // nb,),
        in_specs=[pl.BlockSpec((nb, HALO, D_B), lambda i: (i, 0, 0)),
                  pl.BlockSpec((nb, t_len, D_B), rmap(P_UB // D_B)),
                  pl.BlockSpec((nb, t_len, D_B), rmap(P_GB // D_B)),
                  pl.BlockSpec((nb, t_len, D_B), rmap(P_UC // D_B)),
                  pl.BlockSpec((nb, t_len, D_B), rmap(P_VC // D_B)),
                  pl.BlockSpec((nb, t_len, D_B), rmap(P_GC // D_B)),
                  wspec, vspec, vspec, wspec, wspec],
        out_specs=[pl.BlockSpec((rows, D_B + D_C), lambda i: (i, 0)),
                   pl.BlockSpec((rows, D_C), lambda i: (i, 0))],
        out_shape=[jax.ShapeDtypeStruct((n_seq * t_len, D_B + D_C), BF16),
                   jax.ShapeDtypeStruct((n_seq * t_len, D_C), F32)],
        compiler_params=pltpu.CompilerParams(
            dimension_semantics=("parallel",), vmem_limit_bytes=VMEM_LIMIT),
        name="bc_sample",
    )(buf16, proj3, proj3, proj3, proj3, proj3, pw, pscale, ln_g, wm, bm)


WPREP_ROWS = 256


def _wprep_kernel(w_ref, o_ref):
    o_ref[0, :, :3 * D_A] = w_ref[0, :, :3 * D_A].astype(BF16)
    o_ref[0, :, 3 * D_A:P_LORA] = w_ref[0, :, SHIFT_W:].astype(BF16)
    o_ref[0, :, P_LORA:] = w_ref[0, :, 3 * D_A:SHIFT_W].astype(BF16)


def _prep_w_in(w_in):
    spec = pl.BlockSpec((1, WPREP_ROWS, D_INP), lambda l, i: (l, i, 0))
    return pl.pallas_call(
        _wprep_kernel,
        grid=(DEPTH, D_MODEL // WPREP_ROWS),
        in_specs=[spec],
        out_specs=spec,
        out_shape=jax.ShapeDtypeStruct((DEPTH, D_MODEL, D_INP), BF16),
        compiler_params=pltpu.CompilerParams(
            dimension_semantics=("parallel", "parallel"), vmem_limit_bytes=VMEM_LIMIT),
        name="wprep",
    )(w_in)


def kernel(x_prompt, x_sample, state_shift, state_wkv, state_pool, norm_g, final_norm_g, w_in,
           shift_mu, w0, w_up, a0, a_up, k_k, k_a, r_k, lnx_g, lnx_b, pool_w, pool_scale,
           gmlp_ln_g, gmlp_ws, gmlp_b, w_out):
    bp, seq, _ = x_prompt.shape
    bs, dseq, _ = x_sample.shape
    n_p = bp * seq
    n_s = bs * dseq
    xp = x_prompt.reshape(n_p, D_MODEL)
    xs = x_sample.reshape(n_s, D_MODEL)
    hp = _norm_rows(xp, norm_g[0][None])
    hs = _norm_rows(xs, norm_g[0][None])

    w_in_p = _prep_w_in(w_in)
    w_out_h = w_out.astype(BF16)
    mu_l = jnp.pad(shift_mu[:, 3 * D_A:], ((0, 0), (0, D_A - 2 * LORA)))
    prm = jnp.stack([shift_mu[:, :D_A], shift_mu[:, D_A:2 * D_A], shift_mu[:, 2 * D_A:3 * D_A],
                     w0, a0, k_k, k_a, r_k.reshape(DEPTH, D_A), lnx_g, lnx_b, mu_l], axis=1)
    prm = jnp.pad(prm, ((0, 0), (0, 16 - prm.shape[1]), (0, 0)))
    w2 = jnp.concatenate([w_up, a_up], axis=1)
    pw = pool_w.astype(BF16)
    tril = jnp.tril(jnp.ones((CHUNK, CHUNK), F32))
    wm_p = (gmlp_ws * tril).astype(BF16)
    bm_p = jnp.broadcast_to(gmlp_b[:, :, :, None], (DEPTH, N_GROUPS_C, CHUNK, GC))
    nb_s = CHUNK // dseq
    eye_b = jnp.eye(nb_s, dtype=F32)
    ws_small = gmlp_ws[:, :, :dseq, :dseq] * tril[:dseq, :dseq]
    wm_s = jnp.einsum('ab,lgts->lgatbs', eye_b, ws_small).reshape(DEPTH, N_GROUPS_C, CHUNK, CHUNK)
    wm_s = wm_s.astype(BF16)
    bm_s = jnp.broadcast_to(jnp.tile(gmlp_b[:, :, :dseq], (1, 1, nb_s))[:, :, :, None],
                            (DEPTH, N_GROUPS_C, CHUNK, GC))

    ssh_main = state_shift[:, :, :3 * D_A].reshape(DEPTH, bs, 3, 1, D_A).transpose(0, 2, 1, 3, 4)
    ssh_lora = state_shift[:, :, 3 * D_A:].reshape(DEPTH, bs, 1, PAIR)
    wkv_s = state_wkv.reshape(DEPTH, bs, N_PAIRS, PAIR, HEAD_A)
    buf16 = jnp.pad(state_pool, ((0, 0), (0, 0), (HALO - POOL_BUF, 0), (0, 0)))

    p_shift, p_wkv, p_pool, s_shift, s_pool, s_v = [], [], [], [], [], []
    for l in range(DEPTH):
        final = l == DEPTH - 1
        g_out = final_norm_g[None] if final else norm_g[l + 1][None]
        bc_w = (pw[l], pool_scale[l][None], gmlp_ln_g[l][None])

        proj_p = _inproj_h(hp, w_in_p, l, 2048)
        ya_p, wk_p, shm_p, shl_p = _wkv_prompt(proj_p, bp, seq, prm, w2, l)
        cb_p = _bc_prompt(proj_p, bp, seq, *bc_w, wm_p[l], bm_p[l])
        xp = _outproj(ya_p, cb_p, w_out_h, l, xp, g_out, final)
        if not final:
            xp, hp = xp

        proj_s = _inproj_h(hs, w_in_p, l, 1024)
        ya_s, wkv_s, shm_s, shl_s = _wkv_sample(proj_s, bs, dseq, ROWS // dseq, ssh_main, ssh_lora,
                                                wkv_s, prm, w2, l)
        cb_s, vn_s = _bc_sample(proj_s.reshape(bs, dseq, D_INP), 0, bs, dseq, buf16[l], *bc_w,
                                wm_s[l], bm_s[l])
        xs = _outproj(ya_s, cb_s, w_out_h, l, xs, g_out, final)
        if not final:
            xs, hs = xs

        p_shift.append(jnp.concatenate([shm_p[0, :, 0], shm_p[1, :, 0], shm_p[2, :, 0], shl_p[:, 0]], axis=-1))
        s_shift.append(jnp.concatenate([shm_s[0, :, 0], shm_s[1, :, 0], shm_s[2, :, 0], shl_s[:, 0]], axis=-1))
        p_wkv.append(wk_p.reshape(bp, N_HEADS_A, HEAD_A, HEAD_A))
        p_pool.append(jnp.stack([proj_p[(b + 1) * seq - POOL_BUF:(b + 1) * seq, P_UB:P_UB + D_B]
                                 for b in range(bp)]))
        ub_s = proj_s[:, P_UB:P_UB + D_B].reshape(bs, dseq, D_B)
        s_pool.append(jnp.concatenate([state_pool[l], ub_s], axis=1)[:, -POOL_BUF:])
        s_v.append(vn_s.reshape(bs, dseq, D_C))

    y_prompt = xp.reshape(bp, seq, D_MODEL)
    y_sample = xs.reshape(bs, dseq, D_MODEL)
    s_wkv = wkv_s.reshape(DEPTH, bs, N_HEADS_A, HEAD_A, HEAD_A)
    return (y_prompt, y_sample, jnp.stack(p_shift), jnp.stack(p_wkv), jnp.stack(p_pool),
            jnp.stack(s_shift), s_wkv, jnp.stack(s_pool), jnp.stack(s_v))
```

```python
import functools

import jax
import jax.numpy as jnp
import numpy as np
from jax import lax
from jax.experimental import pallas as pl
from jax.experimental.pallas import tpu as pltpu

F32 = jnp.float32
BF16 = jnp.bfloat16

D_MODEL = 2048
DEPTH = 4
PAST_LEN = 16384
D_A = 1024
HEAD_A = 64
N_HEADS_A = 16
LORA = 64
D_B = 512
POOL_WINDOWS = (2, 4, 8, 16)
POOL_GC = 128
POOL_BUF = 15
D_C = 512
N_GROUPS_C = 4
GC = 128
CHUNK = 128
SHIFT_W = 3 * D_A + 2 * LORA
EPS = 1e-6
GN_EPS = HEAD_A * 1e-5
LN_EPS = 1e-5

P_R, P_K, P_V, P_GA = 0, 1024, 2048, 3072
P_UB, P_GB, P_UC, P_VC, P_GC = 4096, 4608, 5120, 5632, 6144
P_LORA = 6656
D_INP = 6784

PAIR = 128
N_PAIRS = D_A // PAIR
ROWS = 64

PROJ_DTYPE = BF16
VMEM_LIMIT = 52 * 1024 * 1024
HI = lax.Precision.HIGHEST


def _dot(a, b, prec=None):
    return jnp.dot(a, b, precision=prec, preferred_element_type=F32)


def _dot_nt(a, b, prec=None):
    return lax.dot_general(a, b, (((1,), (1,)), ((), ())), precision=prec,
                           preferred_element_type=F32)


def _dot_tn(a, b, prec=None):
    return lax.dot_general(a, b, (((0,), (0,)), ((), ())), precision=prec,
                           preferred_element_type=F32)


def _split(x, n):
    pieces = []
    rem = x
    for i in range(n):
        hi = rem.astype(BF16)
        pieces.append(hi)
        if i + 1 < n:
            rem = rem - hi.astype(F32)
    return pieces


_CONTRACT = {"nn": (1, 0), "nt": (1, 1), "tn": (0, 0)}


def _mm(a, b, mode="nn", pa=1, pb=1):
    ca, cb = _CONTRACT[mode]
    sa, sb = _split(a, pa), _split(b, pb)
    terms = [(i, j) for i in range(pa) for j in range(pb) if i + j < max(pa, pb)]
    lhs = jnp.concatenate([sa[i] for i, _ in terms], axis=ca) if len(terms) > 1 else sa[0]
    rhs = jnp.concatenate([sb[j] for _, j in terms], axis=cb) if len(terms) > 1 else sb[0]
    return lax.dot_general(lhs, rhs, (((ca,), (cb,)), ((), ())), preferred_element_type=F32)


PREC_LORA = (1, 1)
PREC_CUM = (1, 2)
PREC_SEG = (1, 1)
PREC_G = (1, 1)
PREC_INV = (1, 1)
PREC_X2 = (1, 1)
PREC_WU = (1, 1)
PREC_US = (1, 1)
PREC_UPD = (1, 1)
PREC_Y = (1, 1)


def _norm_kernel(x_ref, g_ref, o_ref):
    x = x_ref[...]
    ms = jnp.mean(x * x, axis=-1, keepdims=True)
    o_ref[...] = ((x * lax.rsqrt(ms + EPS)) * g_ref[...]).astype(BF16)


def _norm_rows(x, g, tm=1024):
    m = x.shape[0]
    return pl.pallas_call(
        _norm_kernel,
        grid=(m // tm,),
        in_specs=[pl.BlockSpec((tm, D_MODEL), lambda i: (i, 0)),
                  pl.BlockSpec((1, D_MODEL), lambda i: (0, 0))],
        out_specs=pl.BlockSpec((tm, D_MODEL), lambda i: (i, 0)),
        out_shape=jax.ShapeDtypeStruct((m, D_MODEL), BF16),
        compiler_params=pltpu.CompilerParams(
            dimension_semantics=("parallel",), vmem_limit_bytes=VMEM_LIMIT),
        name="norm_rows",
    )(x, g)


def _inproj_h_kernel(h_ref, w_ref, o_ref):
    o_ref[...] = _dot(h_ref[...], w_ref[0]).astype(o_ref.dtype)


def _inproj_h(h, w, layer, tm, tn=768):
    m = h.shape[0]
    return pl.pallas_call(
        _inproj_h_kernel,
        grid=(m // tm, pl.cdiv(D_INP, tn)),
        in_specs=[pl.BlockSpec((tm, D_MODEL), lambda i, j: (i, 0)),
                  pl.BlockSpec((1, D_MODEL, tn), lambda i, j: (layer, 0, j))],
        out_specs=pl.BlockSpec((tm, tn), lambda i, j: (i, j)),
        out_shape=jax.ShapeDtypeStruct((m, D_INP), PROJ_DTYPE),
        compiler_params=pltpu.CompilerParams(
            dimension_semantics=("parallel", "arbitrary"), vmem_limit_bytes=VMEM_LIMIT),
        name="inproj_h",
    )(h, w)


def _outproj_kernel(ca_ref, cb_ref, wa_ref, wb_ref, x_ref, g_ref, *o_refs, final):
    y = _dot(ca_ref[...], wa_ref[0]) + _dot(cb_ref[...], wb_ref[0])
    out = x_ref[...] + y
    ms = jnp.mean(out * out, axis=-1, keepdims=True)
    normed = (out * lax.rsqrt(ms + EPS)) * g_ref[...]
    if final:
        o_refs[0][...] = normed
    else:
        o_refs[0][...] = out
        o_refs[1][...] = normed.astype(BF16)


def _outproj(cat_a, cat_b, w, layer, x, g, final, tm=512):
    m = x.shape[0]
    half = D_MODEL // 2
    row_spec = pl.BlockSpec((tm, D_MODEL), lambda i: (i, 0))
    f32_out = jax.ShapeDtypeStruct((m, D_MODEL), F32)
    return pl.pallas_call(
        functools.partial(_outproj_kernel, final=final),
        grid=(m // tm,),
        in_specs=[pl.BlockSpec((tm, half), lambda i: (i, 0)),
                  pl.BlockSpec((tm, half), lambda i: (i, 0)),
                  pl.BlockSpec((1, half, D_MODEL), lambda i: (layer, 0, 0)),
                  pl.BlockSpec((1, half, D_MODEL), lambda i: (layer, 1, 0)),
                  pl.BlockSpec((tm, D_MODEL), lambda i: (i, 0)),
                  pl.BlockSpec((1, D_MODEL), lambda i: (0, 0))],
        out_specs=row_spec if final else [row_spec, row_spec],
        out_shape=f32_out if final else [f32_out, jax.ShapeDtypeStruct((m, D_MODEL), BF16)],
        compiler_params=pltpu.CompilerParams(
            dimension_semantics=("parallel",), vmem_limit_bytes=VMEM_LIMIT),
        name="outproj",
    )(cat_a, cat_b, w, w, x, g)


PR_MU_R, PR_MU_K, PR_MU_V, PR_W0, PR_A0, PR_KK, PR_KA, PR_RK, PR_LNG, PR_LNB, PR_MU_L = range(11)


def _softplus(z):
    return jnp.maximum(z, 0.0) + jnp.log(1.0 + jnp.exp(-jnp.abs(z)))


def _sigmoid(z):
    return 1.0 / (1.0 + jnp.exp(-z))


def _silu(z):
    return z * _sigmoid(z)


def _run(gen):
    try:
        while True:
            next(gen)
    except StopIteration as e:
        return e.value


def _zip_run(main, prep, pattern):
    done = {}

    def step(gen, key):
        if key not in done:
            try:
                next(gen)
            except StopIteration as e:
                done[key] = e.value

    for ch in pattern:
        step(main if ch == "m" else prep, ch)
    while "m" not in done:
        step(main, "m")
    while "p" not in done:
        step(prep, "p")
    return done["m"], done["p"]


class _ScanConsts:
    def __init__(self, C, NB):
        rows = ROWS
        self.C, self.NB = C, NB
        row_id = lax.broadcasted_iota(jnp.int32, (rows, 1), 0)
        self.first = (row_id % C) == 0
        self.lo_half = lax.broadcasted_iota(jnp.int32, (rows, PAIR), 1) < HEAD_A
        ri = lax.broadcasted_iota(jnp.int32, (rows, rows), 0)
        ci = lax.broadcasted_iota(jnp.int32, (rows, rows), 1)
        self.tri = jnp.where(((ri // C) == (ci // C)) & (ri >= ci), 1.0, 0.0)
        pr = lax.broadcasted_iota(jnp.int32, (PAIR, PAIR), 0)
        pc = lax.broadcasted_iota(jnp.int32, (PAIR, PAIR), 1)
        self.same_head = (pr // HEAD_A) == (pc // HEAD_A)
        self.seg = jnp.where(self.same_head, 1.0, 0.0)
        r2 = lax.broadcasted_iota(jnp.int32, (2 * rows, 2 * rows), 0)
        c2 = lax.broadcasted_iota(jnp.int32, (2 * rows, 2 * rows), 1)
        same_blk = (r2 // C) == (c2 // C)
        self.m_strict = same_blk & (r2 > c2)
        m_incl = same_blk & (r2 >= c2)
        self.m_incl2 = jnp.concatenate([m_incl, m_incl], axis=1)
        self.eye = jnp.where(r2 == c2, 1.0, 0.0)


def _bmask(m, xb):
    return jnp.where(m, xb, jnp.zeros_like(xb))


_PP_BF16 = ("at", "rt", "kh", "bh", "kg", "bg", "vs")
_PP_F32 = ("bon", "v", "sg")


def _prep_steps(raw, prev, prm, w2, cs):
    C, NB, rows = cs.C, cs.NB, ROWS
    r_raw, k_raw, v_raw, g_raw, l_raw = raw
    prev_r, prev_k, prev_v, prev_l = prev

    def prow(i):
        return prm[i:i + 1, :]

    def shift_mix(x, prev_rows, mu):
        rolled = pltpu.roll(x, 1, 0)
        pr_ = jnp.concatenate([jnp.broadcast_to(q, (C, x.shape[1])) for q in prev_rows], axis=0)
        shifted = jnp.where(cs.first, pr_, rolled)
        return x + (shifted - x) * mu

    xl = shift_mix(l_raw, prev_l, prow(PR_MU_L)[:, :PAIR])
    lora_w = _mm(jnp.where(cs.lo_half, jnp.tanh(xl), 0.0), w2, "nn", *PREC_LORA)
    lora_a = _mm(jnp.where(cs.lo_half, 0.0, xl), w2, "nn", *PREC_LORA)
    yield
    r = shift_mix(r_raw, prev_r, prow(PR_MU_R))
    k = shift_mix(k_raw, prev_k, prow(PR_MU_K))
    v = shift_mix(v_raw, prev_v, prow(PR_MU_V))
    w_log = -_softplus(-(prow(PR_W0) + lora_w)) - 0.5
    dec = jnp.exp(w_log)
    a = _sigmoid(prow(PR_A0) + lora_a)
    cum = _mm(cs.tri, -dec, "nn", *PREC_CUM)
    yield
    kk_raw = k * prow(PR_KK)
    k2 = k * (1.0 + (a - 1.0) * prow(PR_KA))
    rkr = r * k2 * prow(PR_RK)
    sls = [slice(p * PAIR, (p + 1) * PAIR) for p in range(N_PAIRS)]
    s0 = [_mm(jnp.concatenate([kk_raw[:, sl] * kk_raw[:, sl], rkr[:, sl]], axis=0), cs.seg, "nn", *PREC_SEG)
          for sl in sls]
    yield
    tot = jnp.concatenate(
        [jnp.broadcast_to(cum[(b + 1) * C - 1:(b + 1) * C, :], (C, D_A)) for b in range(NB)], axis=0)
    g_incl = jnp.exp(cum)
    g_excl = jnp.exp(cum + dec)
    g_inv = jnp.exp(-cum)
    g_rest = jnp.exp(tot - cum)
    g_tot = jnp.exp(tot)
    sg = _silu(g_raw)

    def stack(x):
        xb = x.astype(BF16)
        zb = jnp.zeros_like(xb)
        return jnp.concatenate([jnp.where(cs.lo_half, xb, zb), jnp.where(cs.lo_half, zb, xb)], axis=0)

    pp = {n: [] for n in _PP_BF16 + _PP_F32 + ("gtot",)}
    for sl, q in zip(sls, s0):
        kkp = kk_raw[:, sl] * lax.rsqrt(jnp.maximum(q[:rows], 1e-12))
        kka = kkp * a[:, sl]
        pp["at"].append(stack(-kkp * g_excl[:, sl]))
        pp["rt"].append(stack(r[:, sl] * g_incl[:, sl]))
        pp["kh"].append(stack(k2[:, sl] * g_inv[:, sl]))
        pp["bh"].append(stack(kka * g_inv[:, sl]))
        pp["kg"].append(stack(k2[:, sl] * g_rest[:, sl]))
        pp["bg"].append(stack(kka * g_rest[:, sl]))
        pp["vs"].append(stack(v[:, sl]))
        pp["bon"].append(q[rows:])
        pp["v"].append(v[:, sl])
        pp["sg"].append(sg[:, sl])
        pp["gtot"].append(jnp.concatenate([g_tot[b * C:b * C + 1, sl] for b in range(NB)], axis=0))
    return pp


def _scan_steps(pp, st_ref, o_ref, row0, prm, cs):
    C, NB, rows = cs.C, cs.NB, ROWS
    n_rounds = int(np.log2(C))
    n2 = 2 * rows
    nb2 = 2 * C
    npair = N_PAIRS

    def prow(i):
        return prm[i:i + 1, :]

    def seq_rows(x, b):
        if NB == 1:
            return x
        xf = x.astype(F32)
        out = jnp.concatenate([xf[b * C:(b + 1) * C], xf[rows + b * C:rows + (b + 1) * C]], axis=0)
        return out.astype(x.dtype)

    def unseq_rows(pieces):
        if NB == 1:
            return pieces[0]
        pf = [q.astype(F32) for q in pieces]
        out = jnp.concatenate([q[:C] for q in pf] + [q[C:] for q in pf], axis=0)
        return out.astype(pieces[0].dtype)

    at_s, rt_s, kh_s, bh_s, kg_s, bg_s, v_s = (pp[n] for n in _PP_BF16)

    pairs = range(npair)

    a_ak, a_r, pwb, t_inv = [], [], [], []
    for p in pairs:
        gm = _mm(jnp.concatenate([at_s[p], rt_s[p]], axis=0),
                 jnp.concatenate([kh_s[p], bh_s[p]], axis=0), "nt", *PREC_G)
        gmb = gm.astype(BF16)
        a_ak.append(_bmask(cs.m_strict, gmb[:n2, :n2]))
        a_r.append(_bmask(cs.m_incl2, gmb[n2:, :]))
        pwb.append(_bmask(cs.m_strict, gmb[:n2, n2:]))
        t_inv.append(cs.eye + jnp.where(cs.m_strict, gm[:n2, n2:], 0.0))
    yield

    x2 = []
    for p in pairs:
        x2.append(_mm(a_ak[p], v_s[p], "nn", *PREC_X2).astype(BF16))
        pwb[p] = _mm(pwb[p], pwb[p], "nn", *PREC_INV).astype(BF16)
    yield
    for rd in range(1, n_rounds):
        for p in pairs:
            tb = t_inv[p].astype(BF16)
            if rd < n_rounds - 1:
                res = _mm(jnp.concatenate([pwb[p], tb], axis=0), pwb[p], "nn", *PREC_INV)
                t_inv[p] = t_inv[p] + res[n2:]
                pwb[p] = res[:n2].astype(BF16)
            else:
                t_inv[p] = t_inv[p] + _mm(tb, pwb[p], "nn", *PREC_INV)
        yield

    w_t, u_t = [], []
    for p in pairs:
        wu = _mm(t_inv[p], jnp.concatenate([at_s[p], x2[p]], axis=1), "nn", *PREC_WU)
        w_t.append(wu[:, :PAIR].astype(BF16))
        u_t.append(wu[:, PAIR:])
    yield

    s_old = [[st_ref[p * NB + b] for b in range(NB)] for p in pairs]
    u_b, rs = [], []
    for p in pairs:
        us = [_mm(jnp.concatenate([seq_rows(w_t[p], b), seq_rows(rt_s[p], b)], axis=0),
                  s_old[p][b], "nt", *PREC_US) for b in range(NB)]
        u_b.append([(us[b][:nb2] + seq_rows(u_t[p], b)).astype(BF16) for b in range(NB)])
        rs.append(unseq_rows([us[b][nb2:] for b in range(NB)]))
    yield
    for p in pairs:
        for b in range(NB):
            upd = _mm(jnp.concatenate([seq_rows(v_s[p], b), u_b[p][b]], axis=0),
                      jnp.concatenate([seq_rows(kg_s[p], b), seq_rows(bg_s[p], b)], axis=0),
                      "tn", *PREC_UPD)
            st_ref[p * NB + b] = s_old[p][b] * pp["gtot"][p][b:b + 1, :] + upd
    yield
    y = []
    for p in pairs:
        y2 = rs[p] + _mm(a_r[p], jnp.concatenate([v_s[p], unseq_rows(u_b[p])], axis=0), "nn", *PREC_Y)
        y.append(y2[:rows] + y2[rows:])
    yield
    yc = []
    for p in pairs:
        yc.append(y[p] - _mm(y[p], cs.seg, "nn", *PREC_SEG) * (1.0 / HEAD_A))
    yield
    for p in pairs:
        sl = slice(p * PAIR, (p + 1) * PAIR)
        yv = _mm(yc[p] * yc[p], cs.seg, "nn", *PREC_SEG) * (1.0 / HEAD_A)
        yn = yc[p] * lax.rsqrt(yv + GN_EPS) * prow(PR_LNG)[:, sl] + prow(PR_LNB)[:, sl]
        o_ref[row0:row0 + rows, sl] = ((yn + pp["bon"][p] * pp["v"][p]) * pp["sg"][p]).astype(o_ref.dtype)


def _write_state(st_ref, sout_ref, cs):
    for b in range(cs.NB):
        for p in range(N_PAIRS):
            sm = jnp.where(cs.same_head, st_ref[p * cs.NB + b], 0.0)
            sout_ref[0, b, p] = sm[:, :HEAD_A] + sm[:, HEAD_A:]


def _wkv_kernel(r_ref, k_ref, v_ref, g_ref, l_ref, sh_ref, shl_ref, s0_ref, prm_ref, w2_ref,
                o_ref, sout_ref, shm_out_ref, shl_out_ref, st_ref, *, C, NB):
    cs = _ScanConsts(C, NB)
    for b in range(NB):
        for p in range(N_PAIRS):
            s = s0_ref[0, b, p]
            s2 = jnp.concatenate([s, s], axis=1)
            st_ref[p * NB + b] = jnp.where(cs.same_head, s2, 0.0)

    raw = tuple(q[...].astype(F32) for q in (r_ref, k_ref, v_ref, g_ref, l_ref))
    prev = ([sh_ref[0, 0, b] for b in range(NB)], [sh_ref[0, 1, b] for b in range(NB)],
            [sh_ref[0, 2, b] for b in range(NB)], [shl_ref[0, b] for b in range(NB)])
    pp = _run(_prep_steps(raw, prev, prm_ref[0], w2_ref[0], cs))
    _run(_scan_steps(pp, st_ref, o_ref, 0, prm_ref[0], cs))

    for b in range(NB):
        last = (b + 1) * C - 1
        shm_out_ref[0, b] = raw[0][last:last + 1, :]
        shm_out_ref[1, b] = raw[1][last:last + 1, :]
        shm_out_ref[2, b] = raw[2][last:last + 1, :]
        shl_out_ref[b] = raw[4][last:last + 1, :]
    _write_state(st_ref, sout_ref, cs)


def _wkv_sample(proj, n_seq, C, NB, sh_main, sh_lora, s0, prm, w2, layer):
    n_groups = n_seq // NB
    assert NB * C == ROWS

    def rmap(col):
        return lambda i: (i, col)

    state_spec = pl.BlockSpec((1, NB, N_PAIRS, PAIR, HEAD_A), lambda i: (layer, i, 0, 0, 0))
    return pl.pallas_call(
        functools.partial(_wkv_kernel, C=C, NB=NB),
        grid=(n_groups,),
        in_specs=[pl.BlockSpec((ROWS, D_A), rmap(P_R // D_A)),
                  pl.BlockSpec((ROWS, D_A), rmap(P_K // D_A)),
                  pl.BlockSpec((ROWS, D_A), rmap(P_V // D_A)),
                  pl.BlockSpec((ROWS, D_A), rmap(P_GA // D_A)),
                  pl.BlockSpec((ROWS, PAIR), rmap(P_LORA // PAIR)),
                  pl.BlockSpec((1, 3, NB, 1, D_A), lambda i: (layer, 0, i, 0, 0)),
                  pl.BlockSpec((1, NB, 1, PAIR), lambda i: (layer, i, 0, 0)),
                  state_spec,
                  pl.BlockSpec((1, 16, D_A), lambda i: (layer, 0, 0)),
                  pl.BlockSpec((1, PAIR, D_A), lambda i: (layer, 0, 0))],
        out_specs=[pl.BlockSpec((ROWS, D_A), lambda i: (i, 0)),
                   state_spec,
                   pl.BlockSpec((3, NB, 1, D_A), lambda i: (0, i, 0, 0)),
                   pl.BlockSpec((NB, 1, PAIR), lambda i: (i, 0, 0))],
        out_shape=[jax.ShapeDtypeStruct((n_seq * C, D_A), BF16),
                   jax.ShapeDtypeStruct(s0.shape, F32),
                   jax.ShapeDtypeStruct((3, n_seq, 1, D_A), F32),
                   jax.ShapeDtypeStruct((n_seq, 1, PAIR), F32)],
        scratch_shapes=[pltpu.VMEM((N_PAIRS * NB, PAIR, PAIR), F32)],
        input_output_aliases={7: 1},
        compiler_params=pltpu.CompilerParams(
            dimension_semantics=("parallel",), vmem_limit_bytes=VMEM_LIMIT),
        name="wkv_c%d" % C,
    )(proj, proj, proj, proj, proj, sh_main, sh_lora, s0, prm, w2)


_PIPE_PATTERN = "pmmpmmmmpmmp"


def _wkv_pipe_kernel(rc_ref, kc_ref, vc_ref, gc_ref, lc_ref, rn_ref, kn_ref, vn_ref, gn_ref, ln_ref,
                     prm_ref, w2_ref, o_ref, sout_ref, shm_out_ref, shl_out_ref,
                     st_ref, pb_ref, pf_ref, pg_ref, *, n_steps):
    k = pl.program_id(1)
    rows = ROWS
    cs = _ScanConsts(rows, 1)
    prm = prm_ref[0]
    w2 = w2_ref[0]

    def raw_rows(refs, lo):
        return tuple(q[lo:lo + rows, :].astype(F32) for q in refs)

    def prev_rows(refs, row):
        rr, kr, vr, _, lr = refs
        return tuple([q[row:row + 1, :].astype(F32)] for q in (rr, kr, vr, lr))

    def store_pp(pp):
        for p in range(N_PAIRS):
            for j, n in enumerate(_PP_BF16):
                pb_ref[p, j] = pp[n][p]
            for j, n in enumerate(_PP_F32):
                pf_ref[p, j] = pp[n][p]
            pg_ref[p] = pp["gtot"][p]

    def load_pp():
        pp = {n: [pb_ref[p, j] for p in range(N_PAIRS)] for j, n in enumerate(_PP_BF16)}
        pp.update({n: [pf_ref[p, j] for p in range(N_PAIRS)] for j, n in enumerate(_PP_F32)})
        pp["gtot"] = [pg_ref[p] for p in range(N_PAIRS)]
        return pp

    cur = (rc_ref, kc_ref, vc_ref, gc_ref, lc_ref)
    nxt = (rn_ref, kn_ref, vn_ref, gn_ref, ln_ref)

    @pl.when(k == 0)
    def _():
        st_ref[...] = jnp.zeros_like(st_ref)
        zero = ([jnp.zeros((1, D_A), F32)], [jnp.zeros((1, D_A), F32)], [jnp.zeros((1, D_A), F32)],
                [jnp.zeros((1, PAIR), F32)])
        store_pp(_run(_prep_steps(raw_rows(cur, 0), zero, prm, w2, cs)))

    pp_a = load_pp()
    _, pp_b = _zip_run(_scan_steps(pp_a, st_ref, o_ref, 0, prm, cs),
                       _prep_steps(raw_rows(cur, rows), prev_rows(cur, rows - 1), prm, w2, cs),
                       _PIPE_PATTERN)
    _, pp_n = _zip_run(_scan_steps(pp_b, st_ref, o_ref, rows, prm, cs),
                       _prep_steps(raw_rows(nxt, 0), prev_rows(cur, 2 * rows - 1), prm, w2, cs),
                       _PIPE_PATTERN)
    store_pp(pp_n)

    last = 2 * rows - 1
    shm_out_ref[0, 0] = rc_ref[last:last + 1, :].astype(F32)
    shm_out_ref[1, 0] = kc_ref[last:last + 1, :].astype(F32)
    shm_out_ref[2, 0] = vc_ref[last:last + 1, :].astype(F32)
    shl_out_ref[0] = lc_ref[last:last + 1, :].astype(F32)

    @pl.when(k == n_steps - 1)
    def _():
        _write_state(st_ref, sout_ref, cs)


def _wkv_prompt(proj, n_seq, seq_len, prm, w2, layer):
    n_steps = seq_len // (2 * ROWS)
    n_chunks = seq_len // ROWS

    def cmap(col):
        return lambda i, k: (i * n_steps + k, col)

    def nmap(col):
        return lambda i, k: (i * n_chunks + jnp.minimum(2 * k + 2, n_chunks - 1), col)

    def specs(rows_, m):
        return [pl.BlockSpec((rows_, D_A), m(P_R // D_A)),
                pl.BlockSpec((rows_, D_A), m(P_K // D_A)),
                pl.BlockSpec((rows_, D_A), m(P_V // D_A)),
                pl.BlockSpec((rows_, D_A), m(P_GA // D_A)),
                pl.BlockSpec((rows_, PAIR), m(P_LORA // PAIR))]

    return pl.pallas_call(
        functools.partial(_wkv_pipe_kernel, n_steps=n_steps),
        grid=(n_seq, n_steps),
        in_specs=specs(2 * ROWS, cmap) + specs(ROWS, nmap) + [
            pl.BlockSpec((1, 16, D_A), lambda i, k: (layer, 0, 0)),
            pl.BlockSpec((1, PAIR, D_A), lambda i, k: (layer, 0, 0))],
        out_specs=[pl.BlockSpec((2 * ROWS, D_A), lambda i, k: (i * n_steps + k, 0)),
                   pl.BlockSpec((1, 1, N_PAIRS, PAIR, HEAD_A), lambda i, k: (0, i, 0, 0, 0)),
                   pl.BlockSpec((3, 1, 1, D_A), lambda i, k: (0, i, 0, 0)),
                   pl.BlockSpec((1, 1, PAIR), lambda i, k: (i, 0, 0))],
        out_shape=[jax.ShapeDtypeStruct((n_seq * seq_len, D_A), BF16),
                   jax.ShapeDtypeStruct((1, n_seq, N_PAIRS, PAIR, HEAD_A), F32),
                   jax.ShapeDtypeStruct((3, n_seq, 1, D_A), F32),
                   jax.ShapeDtypeStruct((n_seq, 1, PAIR), F32)],
        scratch_shapes=[pltpu.VMEM((N_PAIRS, PAIR, PAIR), F32),
                        pltpu.VMEM((N_PAIRS, len(_PP_BF16), 2 * ROWS, PAIR), BF16),
                        pltpu.VMEM((N_PAIRS, len(_PP_F32), ROWS, PAIR), F32),
                        pltpu.VMEM((N_PAIRS, 1, PAIR), F32)],
        compiler_params=pltpu.CompilerParams(
            dimension_semantics=("parallel", "arbitrary"), vmem_limit_bytes=VMEM_LIMIT),
        name="wkv_pipe",
    )(*([proj] * 10), prm, w2)


HALO = 16


def _window_sums(ext):
    w2 = ext + pltpu.roll(ext, 1, 0)
    w4 = w2 + pltpu.roll(w2, 2, 0)
    w8 = w4 + pltpu.roll(w4, 4, 0)
    w16 = w8 + pltpu.roll(w8, 8, 0)
    return (w2, w4, w8, w16)


def _pool_gate(pooled_groups, gb, pw_ref, pscale):
    mixed = [_dot(pg.astype(BF16), pw_ref[g]) for g, pg in enumerate(pooled_groups)]
    yb = jnp.concatenate(mixed, axis=1) * pscale
    return yb * _silu(gb)


def _layer_norm_v(vc, ln_g):
    vm = jnp.mean(vc, axis=-1, keepdims=True)
    d = vc - vm
    vv = jnp.mean(d * d, axis=-1, keepdims=True)
    return d * lax.rsqrt(vv + LN_EPS) * ln_g


def _chunk_gate(vn, uc, gc, wm_ref, bm_ref):
    n_rows = vn.shape[0]
    vnb = vn.astype(BF16)
    outs = []
    for j in range(n_rows // CHUNK):
        rs = slice(j * CHUNK, (j + 1) * CHUNK)
        mix = [_dot(wm_ref[g], vnb[rs, g * GC:(g + 1) * GC]) + bm_ref[g] for g in range(N_GROUPS_C)]
        outs.append(jnp.concatenate(mix, axis=1))
    mix = outs[0] if len(outs) == 1 else jnp.concatenate(outs, axis=0)
    return uc * mix * _silu(gc)


def _bc_prompt_kernel(ub_ref, gb_ref, uc_ref, vc_ref, gc_ref, pw_ref, ps_ref, lng_ref, wm_ref, bm_ref,
                      o_ref, halo_ref, *, tt):
    j = pl.program_id(1)

    @pl.when(j == 0)
    def _():
        halo_ref[...] = jnp.zeros_like(halo_ref)

    u = ub_ref[...].astype(F32)
    ext = jnp.concatenate([halo_ref[...], u], axis=0)
    halo_ref[...] = u[tt - HALO:, :]
    sums = _window_sums(ext)
    pos = j * tt + lax.broadcasted_iota(jnp.int32, (tt, 1), 0)
    pooled = []
    for g, win in enumerate(POOL_WINDOWS):
        ls = slice(g * POOL_GC, (g + 1) * POOL_GC)
        cnt = jnp.minimum(pos + 1, win).astype(F32)
        pooled.append(sums[g][HALO:, ls] / cnt - u[:, ls])
    gb = gb_ref[...].astype(F32)
    o_ref[:, :D_B] = _pool_gate(pooled, gb, pw_ref, ps_ref[...]).astype(o_ref.dtype)
    vn = _layer_norm_v(vc_ref[...].astype(F32), lng_ref[...])
    uc = uc_ref[...].astype(F32)
    gc = gc_ref[...].astype(F32)
    o_ref[:, D_B:] = _chunk_gate(vn, uc, gc, wm_ref, bm_ref).astype(o_ref.dtype)


def _bc_prompt(proj, n_seq, seq_len, pw, pscale, ln_g, wm, bm, tt=256):
    n_t = seq_len // tt

    def rmap(col):
        return lambda i, j: (i * n_t + j, col)

    wspec = pl.BlockSpec((4, 128, 128), lambda i, j: (0, 0, 0))
    vspec = pl.BlockSpec((1, D_B), lambda i, j: (0, 0))
    return pl.pallas_call(
        functools.partial(_bc_prompt_kernel, tt=tt),
        grid=(n_seq, n_t),
        in_specs=[pl.BlockSpec((tt, D_B), rmap(P_UB // D_B)),
                  pl.BlockSpec((tt, D_B), rmap(P_GB // D_B)),
                  pl.BlockSpec((tt, D_B), rmap(P_UC // D_B)),
                  pl.BlockSpec((tt, D_B), rmap(P_VC // D_B)),
                  pl.BlockSpec((tt, D_B), rmap(P_GC // D_B)),
                  wspec, vspec, vspec, wspec, wspec],
        out_specs=pl.BlockSpec((tt, D_B + D_C), rmap(0)),
        out_shape=jax.ShapeDtypeStruct((n_seq * seq_len, D_B + D_C), BF16),
        scratch_shapes=[pltpu.VMEM((HALO, D_B), F32)],
        compiler_params=pltpu.CompilerParams(
            dimension_semantics=("parallel", "arbitrary"), vmem_limit_bytes=VMEM_LIMIT),
        name="bc_prompt",
    )(proj, proj, proj, proj, proj, pw, pscale, ln_g, wm, bm)


def _bc_sample_kernel(buf_ref, ub_ref, gb_ref, uc_ref, vc_ref, gc_ref, pw_ref, ps_ref, lng_ref,
                      wm_ref, bm_ref, o_ref, vn_ref, *, nb, t_len):
    u3 = ub_ref[...].astype(F32)
    ext = jnp.concatenate([buf_ref[...], u3], axis=1)
    per = HALO + t_len
    sums = _window_sums(ext.reshape(nb * per, D_B))
    u = u3.reshape(nb * t_len, D_B)
    pooled = []
    for g, win in enumerate(POOL_WINDOWS):
        ls = slice(g * POOL_GC, (g + 1) * POOL_GC)
        s3 = sums[g].reshape(nb, per, D_B)[:, HALO:, ls].reshape(nb * t_len, POOL_GC)
        cnt = float(min(PAST_LEN + 1, win))
        pooled.append(s3 / cnt - u[:, ls])
    rows = nb * t_len
    gb = gb_ref[...].astype(F32).reshape(rows, D_B)
    o_ref[:, :D_B] = _pool_gate(pooled, gb, pw_ref, ps_ref[...]).astype(o_ref.dtype)
    vn = _layer_norm_v(vc_ref[...].astype(F32).reshape(rows, D_C), lng_ref[...])
    vn_ref[...] = vn
    uc = uc_ref[...].astype(F32).reshape(rows, D_C)
    gc = gc_ref[...].astype(F32).reshape(rows, D_C)
    o_ref[:, D_B:] = _chunk_gate(vn, uc, gc, wm_ref, bm_ref).astype(o_ref.dtype)


def _bc_sample(proj3, seq0, n_seq, t_len, buf16, pw, pscale, ln_g, wm, bm):
    nb = CHUNK // t_len
    sb0 = seq0 // nb

    def rmap(col):
        return lambda i: (sb0 + i, 0, col)

    wspec = pl.BlockSpec((4, 128, 128), lambda i: (0, 0, 0))
    vspec = pl.BlockSpec((1, D_B), lambda i: (0, 0))
    rows = nb * t_len
    return pl.pallas_call(
        functools.partial(_bc_sample_kernel, nb=nb, t_len=t_len),
        grid=(n_seq // nb,),
        in_specs=[pl.BlockSpec((nb, HALO, D_B), lambda i: (i, 0, 0)),
                  pl.BlockSpec((nb, t_len, D_B), rmap(P_UB // D_B)),
                  pl.BlockSpec((nb, t_len, D_B), rmap(P_GB // D_B)),
                  pl.BlockSpec((nb, t_len, D_B), rmap(P_UC // D_B)),
                  pl.BlockSpec((nb, t_len, D_B), rmap(P_VC // D_B)),
                  pl.BlockSpec((nb, t_len, D_B), rmap(P_GC // D_B)),
                  wspec, vspec, vspec, wspec, wspec],
        out_specs=[pl.BlockSpec((rows, D_B + D_C), lambda i: (i, 0)),
                   pl.BlockSpec((rows, D_C), lambda i: (i, 0))],
        out_shape=[jax.ShapeDtypeStruct((n_seq * t_len, D_B + D_C), BF16),
                   jax.ShapeDtypeStruct((n_seq * t_len, D_C), F32)],
        compiler_params=pltpu.CompilerParams(
            dimension_semantics=("parallel",), vmem_limit_bytes=VMEM_LIMIT),
        name="bc_sample",
    )(buf16, proj3, proj3, proj3, proj3, proj3, pw, pscale, ln_g, wm, bm)


WPREP_ROWS = 256


def _wprep_kernel(w_ref, o_ref):
    o_ref[0, :, :3 * D_A] = w_ref[0, :, :3 * D_A].astype(BF16)
    o_ref[0, :, 3 * D_A:P_LORA] = w_ref[0, :, SHIFT_W:].astype(BF16)
    o_ref[0, :, P_LORA:] = w_ref[0, :, 3 * D_A:SHIFT_W].astype(BF16)


def _prep_w_in(w_in):
    spec = pl.BlockSpec((1, WPREP_ROWS, D_INP), lambda l, i: (l, i, 0))
    return pl.pallas_call(
        _wprep_kernel,
        grid=(DEPTH, D_MODEL // WPREP_ROWS),
        in_specs=[spec],
        out_specs=spec,
        out_shape=jax.ShapeDtypeStruct((DEPTH, D_MODEL, D_INP), BF16),
        compiler_params=pltpu.CompilerParams(
            dimension_semantics=("parallel", "parallel"), vmem_limit_bytes=VMEM_LIMIT),
        name="wprep",
    )(w_in)


def kernel(x_prompt, x_sample, state_shift, state_wkv, state_pool, norm_g, final_norm_g, w_in,
           shift_mu, w0, w_up, a0, a_up, k_k, k_a, r_k, lnx_g, lnx_b, pool_w, pool_scale,
           gmlp_ln_g, gmlp_ws, gmlp_b, w_out):
    bp, seq, _ = x_prompt.shape
    bs, dseq, _ = x_sample.shape
    n_p = bp * seq
    n_s = bs * dseq
    xp = x_prompt.reshape(n_p, D_MODEL)
    xs = x_sample.reshape(n_s, D_MODEL)
    hp = _norm_rows(xp, norm_g[0][None])
    hs = _norm_rows(xs, norm_g[0][None])

    w_in_p = _prep_w_in(w_in)
    w_out_h = w_out.astype(BF16)
    mu_l = jnp.pad(shift_mu[:, 3 * D_A:], ((0, 0), (0, D_A - 2 * LORA)))
    prm = jnp.stack([shift_mu[:, :D_A], shift_mu[:, D_A:2 * D_A], shift_mu[:, 2 * D_A:3 * D_A],
                     w0, a0, k_k, k_a, r_k.reshape(DEPTH, D_A), lnx_g, lnx_b, mu_l], axis=1)
    prm = jnp.pad(prm, ((0, 0), (0, 16 - prm.shape[1]), (0, 0)))
    w2 = jnp.concatenate([w_up, a_up], axis=1)
    pw = pool_w.astype(BF16)
    tril = jnp.tril(jnp.ones((CHUNK, CHUNK), F32))
    wm_p = (gmlp_ws * tril).astype(BF16)
    bm_p = jnp.broadcast_to(gmlp_b[:, :, :, None], (DEPTH, N_GROUPS_C, CHUNK, GC))
    nb_s = CHUNK // dseq
    eye_b = jnp.eye(nb_s, dtype=F32)
    ws_small = gmlp_ws[:, :, :dseq, :dseq] * tril[:dseq, :dseq]
    wm_s = jnp.einsum('ab,lgts->lgatbs', eye_b, ws_small).reshape(DEPTH, N_GROUPS_C, CHUNK, CHUNK)
    wm_s = wm_s.astype(BF16)
    bm_s = jnp.broadcast_to(jnp.tile(gmlp_b[:, :, :dseq], (1, 1, nb_s))[:, :, :, None],
                            (DEPTH, N_GROUPS_C, CHUNK, GC))

    ssh_main = state_shift[:, :, :3 * D_A].reshape(DEPTH, bs, 3, 1, D_A).transpose(0, 2, 1, 3, 4)
    ssh_lora = state_shift[:, :, 3 * D_A:].reshape(DEPTH, bs, 1, PAIR)
    wkv_s = state_wkv.reshape(DEPTH, bs, N_PAIRS, PAIR, HEAD_A)
    buf16 = jnp.pad(state_pool, ((0, 0), (0, 0), (HALO - POOL_BUF, 0), (0, 0)))

    p_shift, p_wkv, p_pool, s_shift, s_pool, s_v = [], [], [], [], [], []
    for l in range(DEPTH):
        final = l == DEPTH - 1
        g_out = final_norm_g[None] if final else norm_g[l + 1][None]
        bc_w = (pw[l], pool_scale[l][None], gmlp_ln_g[l][None])

        proj_p = _inproj_h(hp, w_in_p, l, 2048)
        ya_p, wk_p, shm_p, shl_p = _wkv_prompt(proj_p, bp, seq, prm, w2, l)
        cb_p = _bc_prompt(proj_p, bp, seq, *bc_w, wm_p[l], bm_p[l])
        xp = _outproj(ya_p, cb_p, w_out_h, l, xp, g_out, final)
        if not final:
            xp, hp = xp

        proj_s = _inproj_h(hs, w_in_p, l, 1024)
        ya_s, wkv_s, shm_s, shl_s = _wkv_sample(proj_s, bs, dseq, ROWS // dseq, ssh_main, ssh_lora,
                                                wkv_s, prm, w2, l)
        cb_s, vn_s = _bc_sample(proj_s.reshape(bs, dseq, D_INP), 0, bs, dseq, buf16[l], *bc_w,
                                wm_s[l], bm_s[l])
        xs = _outproj(ya_s, cb_s, w_out_h, l, xs, g_out, final)
        if not final:
            xs, hs = xs

        p_shift.append(jnp.concatenate([shm_p[0, :, 0], shm_p[1, :, 0], shm_p[2, :, 0], shl_p[:, 0]], axis=-1))
        s_shift.append(jnp.concatenate([shm_s[0, :, 0], shm_s[1, :, 0], shm_s[2, :, 0], shl_s[:, 0]], axis=-1))
        p_wkv.append(wk_p.reshape(bp, N_HEADS_A, HEAD_A, HEAD_A))
        p_pool.append(jnp.stack([proj_p[(b + 1) * seq - POOL_BUF:(b + 1) * seq, P_UB:P_UB + D_B]
                                 for b in range(bp)]).astype(F32))
        ub_s = proj_s[:, P_UB:P_UB + D_B].astype(F32).reshape(bs, dseq, D_B)
        s_pool.append(jnp.concatenate([state_pool[l], ub_s], axis=1)[:, -POOL_BUF:])
        s_v.append(vn_s.reshape(bs, dseq, D_C))

    y_prompt = xp.reshape(bp, seq, D_MODEL)
    y_sample = xs.reshape(bs, dseq, D_MODEL)
    s_wkv = wkv_s.reshape(DEPTH, bs, N_HEADS_A, HEAD_A, HEAD_A)
    return (y_prompt, y_sample, jnp.stack(p_shift), jnp.stack(p_wkv), jnp.stack(p_pool),
            jnp.stack(s_shift), s_wkv, jnp.stack(s_pool), jnp.stack(s_v))
```

```python
import functools

import jax
import jax.numpy as jnp
import numpy as np
from jax import lax
from jax.experimental import pallas as pl
from jax.experimental.pallas import tpu as pltpu

F32 = jnp.float32
BF16 = jnp.bfloat16

D_MODEL = 2048
DEPTH = 4
PAST_LEN = 16384
D_A = 1024
HEAD_A = 64
N_HEADS_A = 16
LORA = 64
D_B = 512
POOL_WINDOWS = (2, 4, 8, 16)
POOL_GC = 128
POOL_BUF = 15
D_C = 512
N_GROUPS_C = 4
GC = 128
CHUNK = 128
SHIFT_W = 3 * D_A + 2 * LORA
EPS = 1e-6
GN_EPS = HEAD_A * 1e-5
LN_EPS = 1e-5

P_R, P_K, P_V, P_GA = 0, 1024, 2048, 3072
P_UB, P_GB, P_UC, P_VC, P_GC = 4096, 4608, 5120, 5632, 6144
P_LORA = 6656
D_INP = 6784

PAIR = 128
N_PAIRS = D_A // PAIR
ROWS = 64

PROJ_DTYPE = BF16
VMEM_LIMIT = 52 * 1024 * 1024
HI = lax.Precision.HIGHEST


def _dot(a, b, prec=None):
    return jnp.dot(a, b, precision=prec, preferred_element_type=F32)


def _dot_nt(a, b, prec=None):
    return lax.dot_general(a, b, (((1,), (1,)), ((), ())), precision=prec,
                           preferred_element_type=F32)


def _dot_tn(a, b, prec=None):
    return lax.dot_general(a, b, (((0,), (0,)), ((), ())), precision=prec,
                           preferred_element_type=F32)


def _split(x, n):
    pieces = []
    rem = x
    for i in range(n):
        hi = rem.astype(BF16)
        pieces.append(hi)
        if i + 1 < n:
            rem = rem - hi.astype(F32)
    return pieces


_CONTRACT = {"nn": (1, 0), "nt": (1, 1), "tn": (0, 0)}


def _mm(a, b, mode="nn", pa=1, pb=1):
    ca, cb = _CONTRACT[mode]
    sa, sb = _split(a, pa), _split(b, pb)
    terms = [(i, j) for i in range(pa) for j in range(pb) if i + j < max(pa, pb)]
    lhs = jnp.concatenate([sa[i] for i, _ in terms], axis=ca) if len(terms) > 1 else sa[0]
    rhs = jnp.concatenate([sb[j] for _, j in terms], axis=cb) if len(terms) > 1 else sb[0]
    return lax.dot_general(lhs, rhs, (((ca,), (cb,)), ((), ())), preferred_element_type=F32)


PREC_LORA = (1, 1)
PREC_CUM = (1, 2)
PREC_SEG = (1, 1)
PREC_G = (1, 1)
PREC_INV = (1, 1)
PREC_X2 = (1, 1)
PREC_WU = (1, 1)
PREC_US = (1, 1)
PREC_UPD = (1, 1)
PREC_Y = (1, 1)


def _norm_kernel(x_ref, g_ref, o_ref):
    x = x_ref[...]
    ms = jnp.mean(x * x, axis=-1, keepdims=True)
    o_ref[...] = ((x * lax.rsqrt(ms + EPS)) * g_ref[...]).astype(BF16)


def _norm_rows(x, g, tm=1024):
    m = x.shape[0]
    return pl.pallas_call(
        _norm_kernel,
        grid=(m // tm,),
        in_specs=[pl.BlockSpec((tm, D_MODEL), lambda i: (i, 0)),
                  pl.BlockSpec((1, D_MODEL), lambda i: (0, 0))],
        out_specs=pl.BlockSpec((tm, D_MODEL), lambda i: (i, 0)),
        out_shape=jax.ShapeDtypeStruct((m, D_MODEL), BF16),
        compiler_params=pltpu.CompilerParams(
            dimension_semantics=("parallel",), vmem_limit_bytes=VMEM_LIMIT),
        name="norm_rows",
    )(x, g)


def _inproj_h_kernel(h_ref, w_ref, o_ref):
    o_ref[...] = _dot(h_ref[...], w_ref[0]).astype(o_ref.dtype)


def _inproj_h(h, w, layer, tm, tn=768):
    m = h.shape[0]
    return pl.pallas_call(
        _inproj_h_kernel,
        grid=(m // tm, pl.cdiv(D_INP, tn)),
        in_specs=[pl.BlockSpec((tm, D_MODEL), lambda i, j: (i, 0)),
                  pl.BlockSpec((1, D_MODEL, tn), lambda i, j: (layer, 0, j))],
        out_specs=pl.BlockSpec((tm, tn), lambda i, j: (i, j)),
        out_shape=jax.ShapeDtypeStruct((m, D_INP), PROJ_DTYPE),
        compiler_params=pltpu.CompilerParams(
            dimension_semantics=("parallel", "arbitrary"), vmem_limit_bytes=VMEM_LIMIT),
        name="inproj_h",
    )(h, w)


def _outproj_kernel(ca_ref, cb_ref, wa_ref, wb_ref, x_ref, g_ref, *o_refs, final):
    y = _dot(ca_ref[...], wa_ref[0]) + _dot(cb_ref[...], wb_ref[0])
    out = x_ref[...] + y
    ms = jnp.mean(out * out, axis=-1, keepdims=True)
    normed = (out * lax.rsqrt(ms + EPS)) * g_ref[...]
    if final:
        o_refs[0][...] = normed
    else:
        o_refs[0][...] = out
        o_refs[1][...] = normed.astype(BF16)


def _outproj(cat_a, cat_b, w, layer, x, g, final, tm=512):
    m = x.shape[0]
    half = D_MODEL // 2
    row_spec = pl.BlockSpec((tm, D_MODEL), lambda i: (i, 0))
    f32_out = jax.ShapeDtypeStruct((m, D_MODEL), F32)
    return pl.pallas_call(
        functools.partial(_outproj_kernel, final=final),
        grid=(m // tm,),
        in_specs=[pl.BlockSpec((tm, half), lambda i: (i, 0)),
                  pl.BlockSpec((tm, half), lambda i: (i, 0)),
                  pl.BlockSpec((1, half, D_MODEL), lambda i: (layer, 0, 0)),
                  pl.BlockSpec((1, half, D_MODEL), lambda i: (layer, 1, 0)),
                  pl.BlockSpec((tm, D_MODEL), lambda i: (i, 0)),
                  pl.BlockSpec((1, D_MODEL), lambda i: (0, 0))],
        out_specs=row_spec if final else [row_spec, row_spec],
        out_shape=f32_out if final else [f32_out, jax.ShapeDtypeStruct((m, D_MODEL), BF16)],
        compiler_params=pltpu.CompilerParams(
            dimension_semantics=("parallel",), vmem_limit_bytes=VMEM_LIMIT),
        name="outproj",
    )(cat_a, cat_b, w, w, x, g)


PR_MU_R, PR_MU_K, PR_MU_V, PR_W0, PR_A0, PR_KK, PR_KA, PR_RK, PR_LNG, PR_LNB, PR_MU_L = range(11)


def _softplus(z):
    return jnp.maximum(z, 0.0) + jnp.log(1.0 + jnp.exp(-jnp.abs(z)))


def _sigmoid(z):
    return 1.0 / (1.0 + jnp.exp(-z))


def _silu(z):
    return z * _sigmoid(z)


def _run(gen):
    try:
        while True:
            next(gen)
    except StopIteration as e:
        return e.value


def _zip_run(main, prep, pattern):
    done = {}

    def step(gen, key):
        if key not in done:
            try:
                next(gen)
            except StopIteration as e:
                done[key] = e.value

    for ch in pattern:
        step(main if ch == "m" else prep, ch)
    while "m" not in done:
        step(main, "m")
    while "p" not in done:
        step(prep, "p")
    return done["m"], done["p"]


class _ScanConsts:
    def __init__(self, C, NB):
        rows = ROWS
        self.C, self.NB = C, NB
        row_id = lax.broadcasted_iota(jnp.int32, (rows, 1), 0)
        self.first = (row_id % C) == 0
        self.lo_half = lax.broadcasted_iota(jnp.int32, (rows, PAIR), 1) < HEAD_A
        ri = lax.broadcasted_iota(jnp.int32, (rows, rows), 0)
        ci = lax.broadcasted_iota(jnp.int32, (rows, rows), 1)
        self.tri = jnp.where(((ri // C) == (ci // C)) & (ri >= ci), 1.0, 0.0)
        pr = lax.broadcasted_iota(jnp.int32, (PAIR, PAIR), 0)
        pc = lax.broadcasted_iota(jnp.int32, (PAIR, PAIR), 1)
        self.same_head = (pr // HEAD_A) == (pc // HEAD_A)
        self.seg = jnp.where(self.same_head, 1.0, 0.0)
        rp = lax.broadcasted_iota(jnp.int32, (rows, PAIR), 0)
        cp = lax.broadcasted_iota(jnp.int32, (rows, PAIR), 1) % rows
        same_seq = (rp // C) == (cp // C)
        self.m_strict = same_seq & (rp > cp)
        m_incl = same_seq & (rp >= cp)
        self.m_incl2 = jnp.concatenate([m_incl, m_incl], axis=1)
        self.eye = jnp.where(rp == cp, 1.0, 0.0)


def _bmask(m, xb):
    return jnp.where(m, xb, jnp.zeros_like(xb))


_PP_SIDE = ("at", "rt", "kg", "bg", "vb")
_PP_STACK = ("ats", "khs", "bhs", "vs")
_PP_F32 = ("bon", "v", "sg")


def _stack(xb, lo_half):
    zb = jnp.zeros_like(xb)
    return jnp.concatenate([jnp.where(lo_half, xb, zb), jnp.where(lo_half, zb, xb)], axis=0)


def _prep_steps(raw, prev, prm, w2, cs):
    C, NB, rows = cs.C, cs.NB, ROWS
    r_raw, k_raw, v_raw, g_raw, l_raw = raw
    prev_r, prev_k, prev_v, prev_l = prev

    def prow(i):
        return prm[i:i + 1, :]

    def shift_mix(x, prev_rows, mu):
        rolled = pltpu.roll(x, 1, 0)
        pr_ = jnp.concatenate([jnp.broadcast_to(q, (C, x.shape[1])) for q in prev_rows], axis=0)
        shifted = jnp.where(cs.first, pr_, rolled)
        return x + (shifted - x) * mu

    xl = shift_mix(l_raw, prev_l, prow(PR_MU_L)[:, :PAIR])
    lora_w = _mm(jnp.where(cs.lo_half, jnp.tanh(xl), 0.0), w2, "nn", *PREC_LORA)
    lora_a = _mm(jnp.where(cs.lo_half, 0.0, xl), w2, "nn", *PREC_LORA)
    yield
    r = shift_mix(r_raw, prev_r, prow(PR_MU_R))
    k = shift_mix(k_raw, prev_k, prow(PR_MU_K))
    v = shift_mix(v_raw, prev_v, prow(PR_MU_V))
    w_log = -_softplus(-(prow(PR_W0) + lora_w)) - 0.5
    dec = jnp.exp(w_log)
    a = _sigmoid(prow(PR_A0) + lora_a)
    cum = _mm(cs.tri, -dec, "nn", *PREC_CUM)
    yield
    kk_raw = k * prow(PR_KK)
    k2 = k * (1.0 + (a - 1.0) * prow(PR_KA))
    rkr = r * k2 * prow(PR_RK)
    sls = [slice(p * PAIR, (p + 1) * PAIR) for p in range(N_PAIRS)]
    s0 = [_mm(jnp.concatenate([kk_raw[:, sl] * kk_raw[:, sl], rkr[:, sl]], axis=0), cs.seg, "nn", *PREC_SEG)
          for sl in sls]
    yield
    tot = jnp.concatenate(
        [jnp.broadcast_to(cum[(b + 1) * C - 1:(b + 1) * C, :], (C, D_A)) for b in range(NB)], axis=0)
    g_incl = jnp.exp(cum)
    g_excl = jnp.exp(cum + dec)
    g_inv = jnp.exp(-cum)
    g_rest = jnp.exp(tot - cum)
    g_tot = jnp.exp(tot)
    sg = _silu(g_raw)

    pp = {n: [] for n in _PP_SIDE + _PP_STACK + _PP_F32 + ("gtot",)}
    for sl, q in zip(sls, s0):
        kkp = kk_raw[:, sl] * lax.rsqrt(jnp.maximum(q[:rows], 1e-12))
        kka = kkp * a[:, sl]
        at = (-kkp * g_excl[:, sl]).astype(BF16)
        vb = v[:, sl].astype(BF16)
        pp["at"].append(at)
        pp["rt"].append((r[:, sl] * g_incl[:, sl]).astype(BF16))
        pp["kg"].append((k2[:, sl] * g_rest[:, sl]).astype(BF16))
        pp["bg"].append((kka * g_rest[:, sl]).astype(BF16))
        pp["vb"].append(vb)
        pp["ats"].append(_stack(at, cs.lo_half))
        pp["khs"].append(_stack((k2[:, sl] * g_inv[:, sl]).astype(BF16), cs.lo_half))
        pp["bhs"].append(_stack((kka * g_inv[:, sl]).astype(BF16), cs.lo_half))
        pp["vs"].append(_stack(vb, cs.lo_half))
        pp["bon"].append(q[rows:])
        pp["v"].append(v[:, sl])
        pp["sg"].append(sg[:, sl])
        pp["gtot"].append(jnp.concatenate([g_tot[b * C:b * C + 1, sl] for b in range(NB)], axis=0))
    return pp


def _scan_steps(pp, st_ref, o_ref, row0, prm, cs):
    C, NB, rows = cs.C, cs.NB, ROWS
    n_rounds = int(np.log2(C))
    pairs = range(N_PAIRS)

    def prow(i):
        return prm[i:i + 1, :]

    def seq_rows(x, b):
        if NB == 1:
            return x
        return x.astype(F32)[b * C:(b + 1) * C]

    def unseq_rows(pieces):
        return pieces[0] if NB == 1 else jnp.concatenate(pieces, axis=0)

    a_ak, a_r, pwb, t_inv = [], [], [], []
    for p in pairs:
        gm = _mm(jnp.concatenate([pp["at"][p], pp["rt"][p]], axis=0),
                 jnp.concatenate([pp["khs"][p], pp["bhs"][p]], axis=0), "nt", *PREC_G)
        gmb = gm.astype(BF16)
        a_ak.append(_bmask(cs.m_strict, gmb[:rows, :PAIR]))
        a_r.append(_bmask(cs.m_incl2, gmb[rows:, :]))
        pwb.append(_bmask(cs.m_strict, gmb[:rows, PAIR:]))
        t_inv.append(cs.eye + jnp.where(cs.m_strict, gm[:rows, PAIR:], 0.0))
    yield

    x2 = []
    for p in pairs:
        x2.append(_mm(a_ak[p], pp["vs"][p], "nn", *PREC_X2).astype(BF16))
        pwb[p] = _mm(pwb[p], _stack(pwb[p], cs.lo_half), "nn", *PREC_INV).astype(BF16)
    yield
    for rd in range(1, n_rounds):
        for p in pairs:
            tb = t_inv[p].astype(BF16)
            pws = _stack(pwb[p], cs.lo_half)
            if rd < n_rounds - 1:
                res = _mm(jnp.concatenate([pwb[p], tb], axis=0), pws, "nn", *PREC_INV)
                t_inv[p] = t_inv[p] + res[rows:]
                pwb[p] = res[:rows].astype(BF16)
            else:
                t_inv[p] = t_inv[p] + _mm(tb, pws, "nn", *PREC_INV)
        yield

    w_t, u_t = [], []
    for p in pairs:
        wu = _mm(t_inv[p], jnp.concatenate([pp["ats"][p], _stack(x2[p], cs.lo_half)], axis=1),
                 "nn", *PREC_WU)
        w_t.append(wu[:, :PAIR].astype(BF16))
        u_t.append(wu[:, PAIR:])
    yield

    s_old = [[st_ref[p * NB + b] for b in range(NB)] for p in pairs]
    u_b, rs = [], []
    for p in pairs:
        us = [_mm(jnp.concatenate([seq_rows(w_t[p], b), seq_rows(pp["rt"][p], b)], axis=0),
                  s_old[p][b], "nt", *PREC_US) for b in range(NB)]
        u_b.append([us[b][:C] + seq_rows(u_t[p], b) for b in range(NB)])
        rs.append(unseq_rows([us[b][C:] for b in range(NB)]))
    yield
    for p in pairs:
        for b in range(NB):
            upd = _mm(jnp.concatenate([seq_rows(pp["vb"][p], b), u_b[p][b]], axis=0),
                      jnp.concatenate([seq_rows(pp["kg"][p], b), seq_rows(pp["bg"][p], b)], axis=0),
                      "tn", *PREC_UPD)
            st_ref[p * NB + b] = s_old[p][b] * pp["gtot"][p][b:b + 1, :] + jnp.where(cs.same_head, upd, 0.0)
    yield
    y = []
    for p in pairs:
        u_s = _stack(unseq_rows(u_b[p]).astype(BF16), cs.lo_half)
        y.append(rs[p] + _mm(a_r[p], jnp.concatenate([pp["vs"][p], u_s], axis=0), "nn", *PREC_Y))
    yield
    yc = []
    for p in pairs:
        yc.append(y[p] - _mm(y[p], cs.seg, "nn", *PREC_SEG) * (1.0 / HEAD_A))
    yield
    for p in pairs:
        sl = slice(p * PAIR, (p + 1) * PAIR)
        yv = _mm(yc[p] * yc[p], cs.seg, "nn", *PREC_SEG) * (1.0 / HEAD_A)
        yn = yc[p] * lax.rsqrt(yv + GN_EPS) * prow(PR_LNG)[:, sl] + prow(PR_LNB)[:, sl]
        o_ref[row0:row0 + rows, sl] = ((yn + pp["bon"][p] * pp["v"][p]) * pp["sg"][p]).astype(o_ref.dtype)


def _write_state(st_ref, sout_ref, cs):
    for b in range(cs.NB):
        for p in range(N_PAIRS):
            sm = jnp.where(cs.same_head, st_ref[p * cs.NB + b], 0.0)
            sout_ref[0, b, p] = sm[:, :HEAD_A] + sm[:, HEAD_A:]


def _wkv_kernel(r_ref, k_ref, v_ref, g_ref, l_ref, sh_ref, shl_ref, s0_ref, prm_ref, w2_ref,
                o_ref, sout_ref, shm_out_ref, shl_out_ref, st_ref, *, C, NB):
    cs = _ScanConsts(C, NB)
    for b in range(NB):
        for p in range(N_PAIRS):
            s = s0_ref[0, b, p]
            s2 = jnp.concatenate([s, s], axis=1)
            st_ref[p * NB + b] = jnp.where(cs.same_head, s2, 0.0)

    raw = tuple(q[...].astype(F32) for q in (r_ref, k_ref, v_ref, g_ref, l_ref))
    prev = ([sh_ref[0, 0, b] for b in range(NB)], [sh_ref[0, 1, b] for b in range(NB)],
            [sh_ref[0, 2, b] for b in range(NB)], [shl_ref[0, b] for b in range(NB)])
    pp = _run(_prep_steps(raw, prev, prm_ref[0], w2_ref[0], cs))
    _run(_scan_steps(pp, st_ref, o_ref, 0, prm_ref[0], cs))

    for b in range(NB):
        last = (b + 1) * C - 1
        shm_out_ref[0, b] = raw[0][last:last + 1, :]
        shm_out_ref[1, b] = raw[1][last:last + 1, :]
        shm_out_ref[2, b] = raw[2][last:last + 1, :]
        shl_out_ref[b] = raw[4][last:last + 1, :]
    _write_state(st_ref, sout_ref, cs)


def _wkv_sample(proj, n_seq, C, NB, sh_main, sh_lora, s0, prm, w2, layer):
    n_groups = n_seq // NB
    assert NB * C == ROWS

    def rmap(col):
        return lambda i: (i, col)

    state_spec = pl.BlockSpec((1, NB, N_PAIRS, PAIR, HEAD_A), lambda i: (layer, i, 0, 0, 0))
    return pl.pallas_call(
        functools.partial(_wkv_kernel, C=C, NB=NB),
        grid=(n_groups,),
        in_specs=[pl.BlockSpec((ROWS, D_A), rmap(P_R // D_A)),
                  pl.BlockSpec((ROWS, D_A), rmap(P_K // D_A)),
                  pl.BlockSpec((ROWS, D_A), rmap(P_V // D_A)),
                  pl.BlockSpec((ROWS, D_A), rmap(P_GA // D_A)),
                  pl.BlockSpec((ROWS, PAIR), rmap(P_LORA // PAIR)),
                  pl.BlockSpec((1, 3, NB, 1, D_A), lambda i: (layer, 0, i, 0, 0)),
                  pl.BlockSpec((1, NB, 1, PAIR), lambda i: (layer, i, 0, 0)),
                  state_spec,
                  pl.BlockSpec((1, 16, D_A), lambda i: (layer, 0, 0)),
                  pl.BlockSpec((1, PAIR, D_A), lambda i: (layer, 0, 0))],
        out_specs=[pl.BlockSpec((ROWS, D_A), lambda i: (i, 0)),
                   state_spec,
                   pl.BlockSpec((3, NB, 1, D_A), lambda i: (0, i, 0, 0)),
                   pl.BlockSpec((NB, 1, PAIR), lambda i: (i, 0, 0))],
        out_shape=[jax.ShapeDtypeStruct((n_seq * C, D_A), BF16),
                   jax.ShapeDtypeStruct(s0.shape, F32),
                   jax.ShapeDtypeStruct((3, n_seq, 1, D_A), F32),
                   jax.ShapeDtypeStruct((n_seq, 1, PAIR), F32)],
        scratch_shapes=[pltpu.VMEM((N_PAIRS * NB, PAIR, PAIR), F32)],
        input_output_aliases={7: 1},
        compiler_params=pltpu.CompilerParams(
            dimension_semantics=("parallel",), vmem_limit_bytes=VMEM_LIMIT),
        name="wkv_c%d" % C,
    )(proj, proj, proj, proj, proj, sh_main, sh_lora, s0, prm, w2)


_PIPE_PATTERN = "pmmpmmmmpmmp"


def _wkv_pipe_kernel(rc_ref, kc_ref, vc_ref, gc_ref, lc_ref, rn_ref, kn_ref, vn_ref, gn_ref, ln_ref,
                     prm_ref, w2_ref, o_ref, sout_ref, shm_out_ref, shl_out_ref,
                     st_ref, pa_ref, pb_ref, pf_ref, pg_ref, *, n_steps):
    k = pl.program_id(1)
    rows = ROWS
    cs = _ScanConsts(rows, 1)
    prm = prm_ref[0]
    w2 = w2_ref[0]

    def raw_rows(refs, lo):
        return tuple(q[lo:lo + rows, :].astype(F32) for q in refs)

    def prev_rows(refs, row):
        rr, kr, vr, _, lr = refs
        return tuple([q[row:row + 1, :].astype(F32)] for q in (rr, kr, vr, lr))

    def store_pp(pp):
        for p in range(N_PAIRS):
            for j, n in enumerate(_PP_SIDE):
                pa_ref[p, j] = pp[n][p]
            for j, n in enumerate(_PP_STACK):
                pb_ref[p, j] = pp[n][p]
            for j, n in enumerate(_PP_F32):
                pf_ref[p, j] = pp[n][p]
            pg_ref[p] = pp["gtot"][p]

    def load_pp():
        pp = {n: [pa_ref[p, j] for p in range(N_PAIRS)] for j, n in enumerate(_PP_SIDE)}
        pp.update({n: [pb_ref[p, j] for p in range(N_PAIRS)] for j, n in enumerate(_PP_STACK)})
        pp.update({n: [pf_ref[p, j] for p in range(N_PAIRS)] for j, n in enumerate(_PP_F32)})
        pp["gtot"] = [pg_ref[p] for p in range(N_PAIRS)]
        return pp

    cur = (rc_ref, kc_ref, vc_ref, gc_ref, lc_ref)
    nxt = (rn_ref, kn_ref, vn_ref, gn_ref, ln_ref)

    @pl.when(k == 0)
    def _():
        st_ref[...] = jnp.zeros_like(st_ref)
        zero = ([jnp.zeros((1, D_A), F32)], [jnp.zeros((1, D_A), F32)], [jnp.zeros((1, D_A), F32)],
                [jnp.zeros((1, PAIR), F32)])
        store_pp(_run(_prep_steps(raw_rows(cur, 0), zero, prm, w2, cs)))

    pp_a = load_pp()
    _, pp_b = _zip_run(_scan_steps(pp_a, st_ref, o_ref, 0, prm, cs),
                       _prep_steps(raw_rows(cur, rows), prev_rows(cur, rows - 1), prm, w2, cs),
                       _PIPE_PATTERN)
    _, pp_n = _zip_run(_scan_steps(pp_b, st_ref, o_ref, rows, prm, cs),
                       _prep_steps(raw_rows(nxt, 0), prev_rows(cur, 2 * rows - 1), prm, w2, cs),
                       _PIPE_PATTERN)
    store_pp(pp_n)

    last = 2 * rows - 1
    shm_out_ref[0, 0] = rc_ref[last:last + 1, :].astype(F32)
    shm_out_ref[1, 0] = kc_ref[last:last + 1, :].astype(F32)
    shm_out_ref[2, 0] = vc_ref[last:last + 1, :].astype(F32)
    shl_out_ref[0] = lc_ref[last:last + 1, :].astype(F32)

    @pl.when(k == n_steps - 1)
    def _():
        _write_state(st_ref, sout_ref, cs)


def _wkv_prompt(proj, n_seq, seq_len, prm, w2, layer):
    n_steps = seq_len // (2 * ROWS)
    n_chunks = seq_len // ROWS

    def cmap(col):
        return lambda i, k: (i * n_steps + k, col)

    def nmap(col):
        return lambda i, k: (i * n_chunks + jnp.minimum(2 * k + 2, n_chunks - 1), col)

    def specs(rows_, m):
        return [pl.BlockSpec((rows_, D_A), m(P_R // D_A)),
                pl.BlockSpec((rows_, D_A), m(P_K // D_A)),
                pl.BlockSpec((rows_, D_A), m(P_V // D_A)),
                pl.BlockSpec((rows_, D_A), m(P_GA // D_A)),
                pl.BlockSpec((rows_, PAIR), m(P_LORA // PAIR))]

    return pl.pallas_call(
        functools.partial(_wkv_pipe_kernel, n_steps=n_steps),
        grid=(n_seq, n_steps),
        in_specs=specs(2 * ROWS, cmap) + specs(ROWS, nmap) + [
            pl.BlockSpec((1, 16, D_A), lambda i, k: (layer, 0, 0)),
            pl.BlockSpec((1, PAIR, D_A), lambda i, k: (layer, 0, 0))],
        out_specs=[pl.BlockSpec((2 * ROWS, D_A), lambda i, k: (i * n_steps + k, 0)),
                   pl.BlockSpec((1, 1, N_PAIRS, PAIR, HEAD_A), lambda i, k: (0, i, 0, 0, 0)),
                   pl.BlockSpec((3, 1, 1, D_A), lambda i, k: (0, i, 0, 0)),
                   pl.BlockSpec((1, 1, PAIR), lambda i, k: (i, 0, 0))],
        out_shape=[jax.ShapeDtypeStruct((n_seq * seq_len, D_A), BF16),
                   jax.ShapeDtypeStruct((1, n_seq, N_PAIRS, PAIR, HEAD_A), F32),
                   jax.ShapeDtypeStruct((3, n_seq, 1, D_A), F32),
                   jax.ShapeDtypeStruct((n_seq, 1, PAIR), F32)],
        scratch_shapes=[pltpu.VMEM((N_PAIRS, PAIR, PAIR), F32),
                        pltpu.VMEM((N_PAIRS, len(_PP_SIDE), ROWS, PAIR), BF16),
                        pltpu.VMEM((N_PAIRS, len(_PP_STACK), 2 * ROWS, PAIR), BF16),
                        pltpu.VMEM((N_PAIRS, len(_PP_F32), ROWS, PAIR), F32),
                        pltpu.VMEM((N_PAIRS, 1, PAIR), F32)],
        compiler_params=pltpu.CompilerParams(
            dimension_semantics=("parallel", "arbitrary"), vmem_limit_bytes=VMEM_LIMIT),
        name="wkv_pipe",
    )(*([proj] * 10), prm, w2)


HALO = 16


def _window_sums(ext):
    w2 = ext + pltpu.roll(ext, 1, 0)
    w4 = w2 + pltpu.roll(w2, 2, 0)
    w8 = w4 + pltpu.roll(w4, 4, 0)
    w16 = w8 + pltpu.roll(w8, 8, 0)
    return (w2, w4, w8, w16)


def _pool_gate(pooled_groups, gb, pw_ref, pscale):
    mixed = [_dot(pg.astype(BF16), pw_ref[g]) for g, pg in enumerate(pooled_groups)]
    yb = jnp.concatenate(mixed, axis=1) * pscale
    return yb * _silu(gb)


def _layer_norm_v(vc, ln_g):
    vm = jnp.mean(vc, axis=-1, keepdims=True)
    d = vc - vm
    vv = jnp.mean(d * d, axis=-1, keepdims=True)
    return d * lax.rsqrt(vv + LN_EPS) * ln_g


def _chunk_gate(vn, uc, gc, wm_ref, bm_ref):
    n_rows = vn.shape[0]
    vnb = vn.astype(BF16)
    outs = []
    for j in range(n_rows // CHUNK):
        rs = slice(j * CHUNK, (j + 1) * CHUNK)
        mix = [_dot(wm_ref[g], vnb[rs, g * GC:(g + 1) * GC]) + bm_ref[g] for g in range(N_GROUPS_C)]
        outs.append(jnp.concatenate(mix, axis=1))
    mix = outs[0] if len(outs) == 1 else jnp.concatenate(outs, axis=0)
    return uc * mix * _silu(gc)


def _bc_prompt_kernel(ub_ref, gb_ref, uc_ref, vc_ref, gc_ref, pw_ref, ps_ref, lng_ref, wm_ref, bm_ref,
                      o_ref, halo_ref, *, tt):
    j = pl.program_id(1)

    @pl.when(j == 0)
    def _():
        halo_ref[...] = jnp.zeros_like(halo_ref)

    u = ub_ref[...].astype(F32)
    ext = jnp.concatenate([halo_ref[...], u], axis=0)
    halo_ref[...] = u[tt - HALO:, :]
    sums = _window_sums(ext)
    pos = j * tt + lax.broadcasted_iota(jnp.int32, (tt, 1), 0)
    pooled = []
    for g, win in enumerate(POOL_WINDOWS):
        ls = slice(g * POOL_GC, (g + 1) * POOL_GC)
        cnt = jnp.minimum(pos + 1, win).astype(F32)
        pooled.append(sums[g][HALO:, ls] / cnt - u[:, ls])
    gb = gb_ref[...].astype(F32)
    o_ref[:, :D_B] = _pool_gate(pooled, gb, pw_ref, ps_ref[...]).astype(o_ref.dtype)
    vn = _layer_norm_v(vc_ref[...].astype(F32), lng_ref[...])
    uc = uc_ref[...].astype(F32)
    gc = gc_ref[...].astype(F32)
    o_ref[:, D_B:] = _chunk_gate(vn, uc, gc, wm_ref, bm_ref).astype(o_ref.dtype)


def _bc_prompt(proj, n_seq, seq_len, pw, pscale, ln_g, wm, bm, tt=256):
    n_t = seq_len // tt

    def rmap(col):
        return lambda i, j: (i * n_t + j, col)

    wspec = pl.BlockSpec((4, 128, 128), lambda i, j: (0, 0, 0))
    vspec = pl.BlockSpec((1, D_B), lambda i, j: (0, 0))
    return pl.pallas_call(
        functools.partial(_bc_prompt_kernel, tt=tt),
        grid=(n_seq, n_t),
        in_specs=[pl.BlockSpec((tt, D_B), rmap(P_UB // D_B)),
                  pl.BlockSpec((tt, D_B), rmap(P_GB // D_B)),
                  pl.BlockSpec((tt, D_B), rmap(P_UC // D_B)),
                  pl.BlockSpec((tt, D_B), rmap(P_VC // D_B)),
                  pl.BlockSpec((tt, D_B), rmap(P_GC // D_B)),
                  wspec, vspec, vspec, wspec, wspec],
        out_specs=pl.BlockSpec((tt, D_B + D_C), rmap(0)),
        out_shape=jax.ShapeDtypeStruct((n_seq * seq_len, D_B + D_C), BF16),
        scratch_shapes=[pltpu.VMEM((HALO, D_B), F32)],
        compiler_params=pltpu.CompilerParams(
            dimension_semantics=("parallel", "arbitrary"), vmem_limit_bytes=VMEM_LIMIT),
        name="bc_prompt",
    )(proj, proj, proj, proj, proj, pw, pscale, ln_g, wm, bm)


def _bc_sample_kernel(buf_ref, ub_ref, gb_ref, uc_ref, vc_ref, gc_ref, pw_ref, ps_ref, lng_ref,
                      wm_ref, bm_ref, o_ref, vn_ref, *, nb, t_len):
    u3 = ub_ref[...].astype(F32)
    ext = jnp.concatenate([buf_ref[...], u3], axis=1)
    per = HALO + t_len
    sums = _window_sums(ext.reshape(nb * per, D_B))
    u = u3.reshape(nb * t_len, D_B)
    pooled = []
    for g, win in enumerate(POOL_WINDOWS):
        ls = slice(g * POOL_GC, (g + 1) * POOL_GC)
        s3 = sums[g].reshape(nb, per, D_B)[:, HALO:, ls].reshape(nb * t_len, POOL_GC)
        cnt = float(min(PAST_LEN + 1, win))
        pooled.append(s3 / cnt - u[:, ls])
    rows = nb * t_len
    gb = gb_ref[...].astype(F32).reshape(rows, D_B)
    o_ref[:, :D_B] = _pool_gate(pooled, gb, pw_ref, ps_ref[...]).astype(o_ref.dtype)
    vn = _layer_norm_v(vc_ref[...].astype(F32).reshape(rows, D_C), lng_ref[...])
    vn_ref[...] = vn
    uc = uc_ref[...].astype(F32).reshape(rows, D_C)
    gc = gc_ref[...].astype(F32).reshape(rows, D_C)
    o_ref[:, D_B:] = _chunk_gate(vn, uc, gc, wm_ref, bm_ref).astype(o_ref.dtype)


def _bc_sample(proj3, seq0, n_seq, t_len, buf16, pw, pscale, ln_g, wm, bm):
    nb = CHUNK // t_len
    sb0 = seq0 // nb

    def rmap(col):
        return lambda i: (sb0 + i, 0, col)

    wspec = pl.BlockSpec((4, 128, 128), lambda i: (0, 0, 0))
    vspec = pl.BlockSpec((1, D_B), lambda i: (0, 0))
    rows = nb * t_len
    return pl.pallas_call(
        functools.partial(_bc_sample_kernel, nb=nb, t_len=t_len),
        grid=(n_seq // nb,),
        in_specs=[pl.BlockSpec((nb, HALO, D_B), lambda i: (i, 0, 0)),
                  pl.BlockSpec((nb, t_len, D_B), rmap(P_UB // D_B)),
                  pl.BlockSpec((nb, t_len, D_B), rmap(P_GB // D_B)),
                  pl.BlockSpec((nb, t_len, D_B), rmap(P_UC // D_B)),
                  pl.BlockSpec((nb, t_len, D_B), rmap(P_VC // D_B)),
                  pl.BlockSpec((nb, t_len, D_B), rmap(P_GC // D_B)),
                  wspec, vspec, vspec, wspec, wspec],
        out_specs=[pl.BlockSpec((rows, D_B + D_C), lambda i: (i, 0)),
                   pl.BlockSpec((rows, D_C), lambda i: (i, 0))],
        out_shape=[jax.ShapeDtypeStruct((n_seq * t_len, D_B + D_C), BF16),
                   jax.ShapeDtypeStruct((n_seq * t_len, D_C), F32)],
        compiler_params=pltpu.CompilerParams(
            dimension_semantics=("parallel",), vmem_limit_bytes=VMEM_LIMIT),
        name="bc_sample",
    )(buf16, proj3, proj3, proj3, proj3, proj3, pw, pscale, ln_g, wm, bm)


WPREP_ROWS = 256


def _wprep_kernel(w_ref, o_ref):
    o_ref[0, :, :3 * D_A] = w_ref[0, :, :3 * D_A].astype(BF16)
    o_ref[0, :, 3 * D_A:P_LORA] = w_ref[0, :, SHIFT_W:].astype(BF16)
    o_ref[0, :, P_LORA:] = w_ref[0, :, 3 * D_A:SHIFT_W].astype(BF16)


def _prep_w_in(w_in):
    spec = pl.BlockSpec((1, WPREP_ROWS, D_INP), lambda l, i: (l, i, 0))
    return pl.pallas_call(
        _wprep_kernel,
        grid=(DEPTH, D_MODEL // WPREP_ROWS),
        in_specs=[spec],
        out_specs=spec,
        out_shape=jax.ShapeDtypeStruct((DEPTH, D_MODEL, D_INP), BF16),
        compiler_params=pltpu.CompilerParams(
            dimension_semantics=("parallel", "parallel"), vmem_limit_bytes=VMEM_LIMIT),
        name="wprep",
    )(w_in)


def kernel(x_prompt, x_sample, state_shift, state_wkv, state_pool, norm_g, final_norm_g, w_in,
           shift_mu, w0, w_up, a0, a_up, k_k, k_a, r_k, lnx_g, lnx_b, pool_w, pool_scale,
           gmlp_ln_g, gmlp_ws, gmlp_b, w_out):
    bp, seq, _ = x_prompt.shape
    bs, dseq, _ = x_sample.shape
    n_p = bp * seq
    n_s = bs * dseq
    xp = x_prompt.reshape(n_p, D_MODEL)
    xs = x_sample.reshape(n_s, D_MODEL)
    hp = _norm_rows(xp, norm_g[0][None])
    hs = _norm_rows(xs, norm_g[0][None])

    w_in_p = _prep_w_in(w_in)
    w_out_h = w_out.astype(BF16)
    mu_l = jnp.pad(shift_mu[:, 3 * D_A:], ((0, 0), (0, D_A - 2 * LORA)))
    prm = jnp.stack([shift_mu[:, :D_A], shift_mu[:, D_A:2 * D_A], shift_mu[:, 2 * D_A:3 * D_A],
                     w0, a0, k_k, k_a, r_k.reshape(DEPTH, D_A), lnx_g, lnx_b, mu_l], axis=1)
    prm = jnp.pad(prm, ((0, 0), (0, 16 - prm.shape[1]), (0, 0)))
    w2 = jnp.concatenate([w_up, a_up], axis=1)
    pw = pool_w.astype(BF16)
    tril = jnp.tril(jnp.ones((CHUNK, CHUNK), F32))
    wm_p = (gmlp_ws * tril).astype(BF16)
    bm_p = jnp.broadcast_to(gmlp_b[:, :, :, None], (DEPTH, N_GROUPS_C, CHUNK, GC))
    nb_s = CHUNK // dseq
    eye_b = jnp.eye(nb_s, dtype=F32)
    ws_small = gmlp_ws[:, :, :dseq, :dseq] * tril[:dseq, :dseq]
    wm_s = jnp.einsum('ab,lgts->lgatbs', eye_b, ws_small).reshape(DEPTH, N_GROUPS_C, CHUNK, CHUNK)
    wm_s = wm_s.astype(BF16)
    bm_s = jnp.broadcast_to(jnp.tile(gmlp_b[:, :, :dseq], (1, 1, nb_s))[:, :, :, None],
                            (DEPTH, N_GROUPS_C, CHUNK, GC))

    ssh_main = state_shift[:, :, :3 * D_A].reshape(DEPTH, bs, 3, 1, D_A).transpose(0, 2, 1, 3, 4)
    ssh_lora = state_shift[:, :, 3 * D_A:].reshape(DEPTH, bs, 1, PAIR)
    wkv_s = state_wkv.reshape(DEPTH, bs, N_PAIRS, PAIR, HEAD_A)
    buf16 = jnp.pad(state_pool, ((0, 0), (0, 0), (HALO - POOL_BUF, 0), (0, 0)))

    p_shift, p_wkv, p_pool, s_shift, s_pool, s_v = [], [], [], [], [], []
    for l in range(DEPTH):
        final = l == DEPTH - 1
        g_out = final_norm_g[None] if final else norm_g[l + 1][None]
        bc_w = (pw[l], pool_scale[l][None], gmlp_ln_g[l][None])

        proj_p = _inproj_h(hp, w_in_p, l, 2048)
        ya_p, wk_p, shm_p, shl_p = _wkv_prompt(proj_p, bp, seq, prm, w2, l)
        cb_p = _bc_prompt(proj_p, bp, seq, *bc_w, wm_p[l], bm_p[l])
        xp = _outproj(ya_p, cb_p, w_out_h, l, xp, g_out, final)
        if not final:
            xp, hp = xp

        proj_s = _inproj_h(hs, w_in_p, l, 1024)
        ya_s, wkv_s, shm_s, shl_s = _wkv_sample(proj_s, bs, dseq, ROWS // dseq, ssh_main, ssh_lora,
                                                wkv_s, prm, w2, l)
        cb_s, vn_s = _bc_sample(proj_s.reshape(bs, dseq, D_INP), 0, bs, dseq, buf16[l], *bc_w,
                                wm_s[l], bm_s[l])
        xs = _outproj(ya_s, cb_s, w_out_h, l, xs, g_out, final)
        if not final:
            xs, hs = xs

        p_shift.append(jnp.concatenate([shm_p[0, :, 0], shm_p[1, :, 0], shm_p[2, :, 0], shl_p[:, 0]], axis=-1))
        s_shift.append(jnp.concatenate([shm_s[0, :, 0], shm_s[1, :, 0], shm_s[2, :, 0], shl_s[:, 0]], axis=-1))
        p_wkv.append(wk_p.reshape(bp, N_HEADS_A, HEAD_A, HEAD_A))
        p_pool.append(jnp.stack([proj_p[(b + 1) * seq - POOL_BUF:(b + 1) * seq, P_UB:P_UB + D_B]
                                 for b in range(bp)]).astype(F32))
        ub_s = proj_s[:, P_UB:P_UB + D_B].astype(F32).reshape(bs, dseq, D_B)
        s_pool.append(jnp.concatenate([state_pool[l], ub_s], axis=1)[:, -POOL_BUF:])
        s_v.append(vn_s.reshape(bs, dseq, D_C))

    y_prompt = xp.reshape(bp, seq, D_MODEL)
    y_sample = xs.reshape(bs, dseq, D_MODEL)
    s_wkv = wkv_s.reshape(DEPTH, bs, N_HEADS_A, HEAD_A, HEAD_A)
    return (y_prompt, y_sample, jnp.stack(p_shift), jnp.stack(p_wkv), jnp.stack(p_pool),
            jnp.stack(s_shift), s_wkv, jnp.stack(s_pool), jnp.stack(s_v))
```

```python
import functools

import jax
import jax.numpy as jnp
import numpy as np
from jax import lax
from jax.experimental import pallas as pl
from jax.experimental.pallas import tpu as pltpu

F32 = jnp.float32
BF16 = jnp.bfloat16

D_MODEL = 2048
DEPTH = 4
PAST_LEN = 16384
D_A = 1024
HEAD_A = 64
N_HEADS_A = 16
LORA = 64
D_B = 512
POOL_WINDOWS = (2, 4, 8, 16)
POOL_GC = 128
POOL_BUF = 15
D_C = 512
N_GROUPS_C = 4
GC = 128
CHUNK = 128
SHIFT_W = 3 * D_A + 2 * LORA
EPS = 1e-6
GN_EPS = HEAD_A * 1e-5
LN_EPS = 1e-5

P_R, P_K, P_V, P_GA = 0, 1024, 2048, 3072
P_UB, P_GB, P_UC, P_VC, P_GC = 4096, 4608, 5120, 5632, 6144
P_LORA = 6656
D_INP = 6784

PAIR = 128
N_PAIRS = D_A // PAIR
ROWS = 64

PROJ_DTYPE = BF16
VMEM_LIMIT = 52 * 1024 * 1024
HI = lax.Precision.HIGHEST


def _dot(a, b, prec=None):
    return jnp.dot(a, b, precision=prec, preferred_element_type=F32)


def _dot_nt(a, b, prec=None):
    return lax.dot_general(a, b, (((1,), (1,)), ((), ())), precision=prec,
                           preferred_element_type=F32)


def _dot_tn(a, b, prec=None):
    return lax.dot_general(a, b, (((0,), (0,)), ((), ())), precision=prec,
                           preferred_element_type=F32)


def _split(x, n):
    pieces = []
    rem = x
    for i in range(n):
        hi = rem.astype(BF16)
        pieces.append(hi)
        if i + 1 < n:
            rem = rem - hi.astype(F32)
    return pieces


_CONTRACT = {"nn": (1, 0), "nt": (1, 1), "tn": (0, 0)}


def _mm(a, b, mode="nn", pa=1, pb=1):
    ca, cb = _CONTRACT[mode]
    sa, sb = _split(a, pa), _split(b, pb)
    terms = [(i, j) for i in range(pa) for j in range(pb) if i + j < max(pa, pb)]
    lhs = jnp.concatenate([sa[i] for i, _ in terms], axis=ca) if len(terms) > 1 else sa[0]
    rhs = jnp.concatenate([sb[j] for _, j in terms], axis=cb) if len(terms) > 1 else sb[0]
    return lax.dot_general(lhs, rhs, (((ca,), (cb,)), ((), ())), preferred_element_type=F32)


PREC_LORA = (1, 1)
PREC_CUM = (1, 2)
PREC_SEG = (1, 1)
PREC_G = (1, 1)
PREC_INV = (1, 1)
PREC_X2 = (1, 1)
PREC_WU = (1, 1)
PREC_US = (1, 1)
PREC_UPD = (1, 1)
PREC_Y = (1, 1)


def _norm_kernel(x_ref, g_ref, o_ref):
    x = x_ref[...]
    ms = jnp.mean(x * x, axis=-1, keepdims=True)
    o_ref[...] = ((x * lax.rsqrt(ms + EPS)) * g_ref[...]).astype(BF16)


def _norm_rows(x, g, tm=512):
    m = x.shape[0]
    return pl.pallas_call(
        _norm_kernel,
        grid=(m // tm,),
        in_specs=[pl.BlockSpec((tm, D_MODEL), lambda i: (i, 0)),
                  pl.BlockSpec((1, D_MODEL), lambda i: (0, 0))],
        out_specs=pl.BlockSpec((tm, D_MODEL), lambda i: (i, 0)),
        out_shape=jax.ShapeDtypeStruct((m, D_MODEL), BF16),
        compiler_params=pltpu.CompilerParams(
            dimension_semantics=("parallel",), vmem_limit_bytes=VMEM_LIMIT),
        name="norm_rows",
    )(x, g)


def _inproj_h_kernel(h_ref, w_ref, o_ref):
    o_ref[...] = _dot(h_ref[...], w_ref[0]).astype(o_ref.dtype)


def _inproj_h(h, w, layer, tm, tn=768):
    m = h.shape[0]
    return pl.pallas_call(
        _inproj_h_kernel,
        grid=(m // tm, pl.cdiv(D_INP, tn)),
        in_specs=[pl.BlockSpec((tm, D_MODEL), lambda i, j: (i, 0)),
                  pl.BlockSpec((1, D_MODEL, tn), lambda i, j: (layer, 0, j))],
        out_specs=pl.BlockSpec((tm, tn), lambda i, j: (i, j)),
        out_shape=jax.ShapeDtypeStruct((m, D_INP), PROJ_DTYPE),
        compiler_params=pltpu.CompilerParams(
            dimension_semantics=("parallel", "arbitrary"), vmem_limit_bytes=VMEM_LIMIT),
        name="inproj_h",
    )(h, w)


def _outproj_kernel(ca_ref, cb_ref, wa_ref, wb_ref, x_ref, g_ref, *o_refs, final):
    y = _dot(ca_ref[...], wa_ref[0]) + _dot(cb_ref[...], wb_ref[0])
    out = x_ref[...] + y
    ms = jnp.mean(out * out, axis=-1, keepdims=True)
    normed = (out * lax.rsqrt(ms + EPS)) * g_ref[...]
    if final:
        o_refs[0][...] = normed
    else:
        o_refs[0][...] = out
        o_refs[1][...] = normed.astype(BF16)


def _outproj(cat_a, cat_b, w, layer, x, g, final, tm=512):
    m = x.shape[0]
    half = D_MODEL // 2
    row_spec = pl.BlockSpec((tm, D_MODEL), lambda i: (i, 0))
    f32_out = jax.ShapeDtypeStruct((m, D_MODEL), F32)
    return pl.pallas_call(
        functools.partial(_outproj_kernel, final=final),
        grid=(m // tm,),
        in_specs=[pl.BlockSpec((tm, half), lambda i: (i, 0)),
                  pl.BlockSpec((tm, half), lambda i: (i, 0)),
                  pl.BlockSpec((1, half, D_MODEL), lambda i: (layer, 0, 0)),
                  pl.BlockSpec((1, half, D_MODEL), lambda i: (layer, 1, 0)),
                  pl.BlockSpec((tm, D_MODEL), lambda i: (i, 0)),
                  pl.BlockSpec((1, D_MODEL), lambda i: (0, 0))],
        out_specs=row_spec if final else [row_spec, row_spec],
        out_shape=f32_out if final else [f32_out, jax.ShapeDtypeStruct((m, D_MODEL), BF16)],
        compiler_params=pltpu.CompilerParams(
            dimension_semantics=("parallel",), vmem_limit_bytes=VMEM_LIMIT),
        name="outproj",
    )(cat_a, cat_b, w, w, x, g)


PR_MU_R, PR_MU_K, PR_MU_V, PR_W0, PR_A0, PR_KK, PR_KA, PR_RK, PR_LNG, PR_LNB, PR_MU_L = range(11)


def _softplus(z):
    return jnp.maximum(z, 0.0) + jnp.log(1.0 + jnp.exp(-jnp.abs(z)))


def _sigmoid(z):
    return 1.0 / (1.0 + jnp.exp(-z))


def _silu(z):
    return z * _sigmoid(z)


def _run(gen):
    try:
        while True:
            next(gen)
    except StopIteration as e:
        return e.value


def _zip_run(main, prep, pattern):
    done = {}

    def step(gen, key):
        if key not in done:
            try:
                next(gen)
            except StopIteration as e:
                done[key] = e.value

    for ch in pattern:
        step(main if ch == "m" else prep, ch)
    while "m" not in done:
        step(main, "m")
    while "p" not in done:
        step(prep, "p")
    return done["m"], done["p"]


class _ScanConsts:
    def __init__(self, C, NB):
        rows = ROWS
        self.C, self.NB = C, NB
        row_id = lax.broadcasted_iota(jnp.int32, (rows, 1), 0)
        self.first = (row_id % C) == 0
        self.lo_half = lax.broadcasted_iota(jnp.int32, (rows, PAIR), 1) < HEAD_A
        ri = lax.broadcasted_iota(jnp.int32, (rows, rows), 0)
        ci = lax.broadcasted_iota(jnp.int32, (rows, rows), 1)
        self.tri = jnp.where(((ri // C) == (ci // C)) & (ri >= ci), 1.0, 0.0)
        pr = lax.broadcasted_iota(jnp.int32, (PAIR, PAIR), 0)
        pc = lax.broadcasted_iota(jnp.int32, (PAIR, PAIR), 1)
        self.same_head = (pr // HEAD_A) == (pc // HEAD_A)
        self.seg = jnp.where(self.same_head, 1.0, 0.0)
        rp = lax.broadcasted_iota(jnp.int32, (rows, PAIR), 0)
        cp = lax.broadcasted_iota(jnp.int32, (rows, PAIR), 1) % rows
        same_seq = (rp // C) == (cp // C)
        self.m_strict = same_seq & (rp > cp)
        m_incl = same_seq & (rp >= cp)
        self.m_incl2 = jnp.concatenate([m_incl, m_incl], axis=1)
        self.eye = jnp.where(rp == cp, 1.0, 0.0)


def _bmask(m, xb):
    return jnp.where(m, xb, jnp.zeros_like(xb))


_PP_SIDE = ("at", "rt", "kg", "bg", "vb")
_PP_STACK = ("ats", "khs", "bhs", "vs")
_PP_F32 = ("bon", "v", "sg")


def _stack(xb, lo_half):
    zb = jnp.zeros_like(xb)
    return jnp.concatenate([jnp.where(lo_half, xb, zb), jnp.where(lo_half, zb, xb)], axis=0)


def _prep_steps(raw, prev, prm, w2, cs):
    C, NB, rows = cs.C, cs.NB, ROWS
    r_raw, k_raw, v_raw, g_raw, l_raw = raw
    prev_r, prev_k, prev_v, prev_l = prev

    def prow(i):
        return prm[i:i + 1, :]

    def shift_mix(x, prev_rows, mu):
        rolled = pltpu.roll(x, 1, 0)
        pr_ = jnp.concatenate([jnp.broadcast_to(q, (C, x.shape[1])) for q in prev_rows], axis=0)
        shifted = jnp.where(cs.first, pr_, rolled)
        return x + (shifted - x) * mu

    xl = shift_mix(l_raw, prev_l, prow(PR_MU_L)[:, :PAIR])
    lora_w = _mm(jnp.where(cs.lo_half, jnp.tanh(xl), 0.0), w2, "nn", *PREC_LORA)
    lora_a = _mm(jnp.where(cs.lo_half, 0.0, xl), w2, "nn", *PREC_LORA)
    yield
    r = shift_mix(r_raw, prev_r, prow(PR_MU_R))
    k = shift_mix(k_raw, prev_k, prow(PR_MU_K))
    v = shift_mix(v_raw, prev_v, prow(PR_MU_V))
    w_log = -_softplus(-(prow(PR_W0) + lora_w)) - 0.5
    dec = jnp.exp(w_log)
    a = _sigmoid(prow(PR_A0) + lora_a)
    cum = _mm(cs.tri, -dec, "nn", *PREC_CUM)
    yield
    kk_raw = k * prow(PR_KK)
    k2 = k * (1.0 + (a - 1.0) * prow(PR_KA))
    rkr = r * k2 * prow(PR_RK)
    sls = [slice(p * PAIR, (p + 1) * PAIR) for p in range(N_PAIRS)]
    s0 = [_mm(jnp.concatenate([kk_raw[:, sl] * kk_raw[:, sl], rkr[:, sl]], axis=0), cs.seg, "nn", *PREC_SEG)
          for sl in sls]
    yield
    g_incl = jnp.exp(cum)
    g_excl = jnp.exp(cum + dec)
    g_inv = jnp.exp(-cum)
    g_tot_rows = [jnp.exp(cum[(b + 1) * C - 1:(b + 1) * C, :]) for b in range(NB)]
    g_rest = g_inv * jnp.concatenate([jnp.broadcast_to(q, (C, D_A)) for q in g_tot_rows], axis=0)
    sg = _silu(g_raw)

    pp = {n: [] for n in _PP_SIDE + _PP_STACK + _PP_F32 + ("gtot",)}
    for sl, q in zip(sls, s0):
        kkp = kk_raw[:, sl] * lax.rsqrt(jnp.maximum(q[:rows], 1e-12))
        kka = kkp * a[:, sl]
        at = (-kkp * g_excl[:, sl]).astype(BF16)
        vb = v[:, sl].astype(BF16)
        pp["at"].append(at)
        pp["rt"].append((r[:, sl] * g_incl[:, sl]).astype(BF16))
        pp["kg"].append((k2[:, sl] * g_rest[:, sl]).astype(BF16))
        pp["bg"].append((kka * g_rest[:, sl]).astype(BF16))
        pp["vb"].append(vb)
        pp["ats"].append(_stack(at, cs.lo_half))
        pp["khs"].append(_stack((k2[:, sl] * g_inv[:, sl]).astype(BF16), cs.lo_half))
        pp["bhs"].append(_stack((kka * g_inv[:, sl]).astype(BF16), cs.lo_half))
        pp["vs"].append(_stack(vb, cs.lo_half))
        pp["bon"].append(q[rows:])
        pp["v"].append(v[:, sl])
        pp["sg"].append(sg[:, sl])
        pp["gtot"].append(jnp.concatenate([q[:, sl] for q in g_tot_rows], axis=0))
    return pp


def _scan_steps(pp, st_ref, o_ref, row0, prm, cs):
    C, NB, rows = cs.C, cs.NB, ROWS
    n_rounds = int(np.log2(C))
    pairs = range(N_PAIRS)

    def prow(i):
        return prm[i:i + 1, :]

    def seq_rows(x, b):
        if NB == 1:
            return x
        return x.astype(F32)[b * C:(b + 1) * C]

    def unseq_rows(pieces):
        return pieces[0] if NB == 1 else jnp.concatenate(pieces, axis=0)

    a_ak, a_r, pwb, t_inv = [], [], [], []
    for p in pairs:
        gm = _mm(jnp.concatenate([pp["at"][p], pp["rt"][p]], axis=0),
                 jnp.concatenate([pp["khs"][p], pp["bhs"][p]], axis=0), "nt", *PREC_G)
        gmb = gm.astype(BF16)
        a_ak.append(_bmask(cs.m_strict, gmb[:rows, :PAIR]))
        a_r.append(_bmask(cs.m_incl2, gmb[rows:, :]))
        pwb.append(_bmask(cs.m_strict, gmb[:rows, PAIR:]))
        t_inv.append(cs.eye + jnp.where(cs.m_strict, gm[:rows, PAIR:], 0.0))
    yield

    x2 = []
    for p in pairs:
        x2.append(_mm(a_ak[p], pp["vs"][p], "nn", *PREC_X2).astype(BF16))
        pwb[p] = _mm(pwb[p], _stack(pwb[p], cs.lo_half), "nn", *PREC_INV).astype(BF16)
    yield
    for rd in range(1, n_rounds):
        for p in pairs:
            tb = t_inv[p].astype(BF16)
            pws = _stack(pwb[p], cs.lo_half)
            if rd < n_rounds - 1:
                res = _mm(jnp.concatenate([pwb[p], tb], axis=0), pws, "nn", *PREC_INV)
                t_inv[p] = t_inv[p] + res[rows:]
                pwb[p] = res[:rows].astype(BF16)
            else:
                t_inv[p] = t_inv[p] + _mm(tb, pws, "nn", *PREC_INV)
        yield

    w_t, u_t = [], []
    for p in pairs:
        wu = _mm(t_inv[p], jnp.concatenate([pp["ats"][p], _stack(x2[p], cs.lo_half)], axis=1),
                 "nn", *PREC_WU)
        w_t.append(wu[:, :PAIR].astype(BF16))
        u_t.append(wu[:, PAIR:])
    yield

    s_old = [[st_ref[p * NB + b] for b in range(NB)] for p in pairs]
    u_b, rs = [], []
    for p in pairs:
        us = [_mm(jnp.concatenate([seq_rows(w_t[p], b), seq_rows(pp["rt"][p], b)], axis=0),
                  s_old[p][b], "nt", *PREC_US) for b in range(NB)]
        u_b.append([us[b][:C] + seq_rows(u_t[p], b) for b in range(NB)])
        rs.append(unseq_rows([us[b][C:] for b in range(NB)]))
    yield
    for p in pairs:
        for b in range(NB):
            upd = _mm(jnp.concatenate([seq_rows(pp["vb"][p], b), u_b[p][b]], axis=0),
                      jnp.concatenate([seq_rows(pp["kg"][p], b), seq_rows(pp["bg"][p], b)], axis=0),
                      "tn", *PREC_UPD)
            st_ref[p * NB + b] = s_old[p][b] * pp["gtot"][p][b:b + 1, :] + jnp.where(cs.same_head, upd, 0.0)
    yield
    y = []
    for p in pairs:
        u_s = _stack(unseq_rows(u_b[p]).astype(BF16), cs.lo_half)
        y.append(rs[p] + _mm(a_r[p], jnp.concatenate([pp["vs"][p], u_s], axis=0), "nn", *PREC_Y))
    yield
    yc = []
    for p in pairs:
        yc.append(y[p] - _mm(y[p], cs.seg, "nn", *PREC_SEG) * (1.0 / HEAD_A))
    yield
    for p in pairs:
        sl = slice(p * PAIR, (p + 1) * PAIR)
        yv = _mm(yc[p] * yc[p], cs.seg, "nn", *PREC_SEG) * (1.0 / HEAD_A)
        yn = yc[p] * lax.rsqrt(yv + GN_EPS) * prow(PR_LNG)[:, sl] + prow(PR_LNB)[:, sl]
        o_ref[row0:row0 + rows, sl] = ((yn + pp["bon"][p] * pp["v"][p]) * pp["sg"][p]).astype(o_ref.dtype)


def _write_state(st_ref, sout_ref, cs, b0=0):
    for b in range(cs.NB):
        for p in range(N_PAIRS):
            sm = jnp.where(cs.same_head, st_ref[p * cs.NB + b], 0.0)
            sout_ref[0, b0 + b, p] = sm[:, :HEAD_A] + sm[:, HEAD_A:]


_PIPE_PATTERN = "pmmpmmmmpmmp"


def _wkv_kernel(r_ref, k_ref, v_ref, g_ref, l_ref, sh_ref, shl_ref, s0_ref, prm_ref, w2_ref,
                o_ref, sout_ref, shm_out_ref, shl_out_ref, st_ref, *, C, NB, NBLK):
    cs = _ScanConsts(C, NB)
    prm = prm_ref[0]
    w2 = w2_ref[0]
    refs = (r_ref, k_ref, v_ref, g_ref, l_ref)
    for q in range(NBLK * NB):
        for p in range(N_PAIRS):
            s = s0_ref[0, q, p]
            s2 = jnp.concatenate([s, s], axis=1)
            st_ref[q // NB, p * NB + q % NB] = jnp.where(cs.same_head, s2, 0.0)

    def prep(blk):
        raw = tuple(q[blk * ROWS:(blk + 1) * ROWS, :].astype(F32) for q in refs)
        seqs = range(blk * NB, (blk + 1) * NB)
        prev = ([sh_ref[0, 0, q] for q in seqs], [sh_ref[0, 1, q] for q in seqs],
                [sh_ref[0, 2, q] for q in seqs], [shl_ref[0, q] for q in seqs])
        return _prep_steps(raw, prev, prm, w2, cs)

    pp = _run(prep(0))
    for blk in range(NBLK):
        scan = _scan_steps(pp, st_ref.at[blk], o_ref, blk * ROWS, prm, cs)
        if blk + 1 < NBLK:
            _, pp = _zip_run(scan, prep(blk + 1), _PIPE_PATTERN)
        else:
            _run(scan)

    for q in range(NBLK * NB):
        last = (q + 1) * C - 1
        shm_out_ref[0, q] = r_ref[last:last + 1, :].astype(F32)
        shm_out_ref[1, q] = k_ref[last:last + 1, :].astype(F32)
        shm_out_ref[2, q] = v_ref[last:last + 1, :].astype(F32)
        shl_out_ref[q] = l_ref[last:last + 1, :].astype(F32)
    for blk in range(NBLK):
        _write_state(st_ref.at[blk], sout_ref, cs, blk * NB)


def _wkv_sample(proj, n_seq, C, NB, sh_main, sh_lora, s0, prm, w2, layer, nblk=2):
    assert NB * C == ROWS
    nsq = NB * nblk
    rows = ROWS * nblk

    def rmap(col):
        return lambda i: (i, col)

    state_spec = pl.BlockSpec((1, nsq, N_PAIRS, PAIR, HEAD_A), lambda i: (layer, i, 0, 0, 0))
    return pl.pallas_call(
        functools.partial(_wkv_kernel, C=C, NB=NB, NBLK=nblk),
        grid=(n_seq // nsq,),
        in_specs=[pl.BlockSpec((rows, D_A), rmap(P_R // D_A)),
                  pl.BlockSpec((rows, D_A), rmap(P_K // D_A)),
                  pl.BlockSpec((rows, D_A), rmap(P_V // D_A)),
                  pl.BlockSpec((rows, D_A), rmap(P_GA // D_A)),
                  pl.BlockSpec((rows, PAIR), rmap(P_LORA // PAIR)),
                  pl.BlockSpec((1, 3, nsq, 1, D_A), lambda i: (layer, 0, i, 0, 0)),
                  pl.BlockSpec((1, nsq, 1, PAIR), lambda i: (layer, i, 0, 0)),
                  state_spec,
                  pl.BlockSpec((1, 16, D_A), lambda i: (layer, 0, 0)),
                  pl.BlockSpec((1, PAIR, D_A), lambda i: (layer, 0, 0))],
        out_specs=[pl.BlockSpec((rows, D_A), lambda i: (i, 0)),
                   state_spec,
                   pl.BlockSpec((3, nsq, 1, D_A), lambda i: (0, i, 0, 0)),
                   pl.BlockSpec((nsq, 1, PAIR), lambda i: (i, 0, 0))],
        out_shape=[jax.ShapeDtypeStruct((n_seq * C, D_A), BF16),
                   jax.ShapeDtypeStruct(s0.shape, F32),
                   jax.ShapeDtypeStruct((3, n_seq, 1, D_A), F32),
                   jax.ShapeDtypeStruct((n_seq, 1, PAIR), F32)],
        scratch_shapes=[pltpu.VMEM((nblk, N_PAIRS * NB, PAIR, PAIR), F32)],
        input_output_aliases={7: 1},
        compiler_params=pltpu.CompilerParams(
            dimension_semantics=("parallel",), vmem_limit_bytes=VMEM_LIMIT),
        name="wkv_c%d" % C,
    )(proj, proj, proj, proj, proj, sh_main, sh_lora, s0, prm, w2)


def _wkv_pipe_kernel(rc_ref, kc_ref, vc_ref, gc_ref, lc_ref, rn_ref, kn_ref, vn_ref, gn_ref, ln_ref,
                     prm_ref, w2_ref, o_ref, sout_ref, shm_out_ref, shl_out_ref,
                     st_ref, pa_ref, pb_ref, pf_ref, pg_ref, *, n_steps):
    k = pl.program_id(1)
    rows = ROWS
    cs = _ScanConsts(rows, 1)
    prm = prm_ref[0]
    w2 = w2_ref[0]

    def raw_rows(refs, lo):
        return tuple(q[lo:lo + rows, :].astype(F32) for q in refs)

    def prev_rows(refs, row):
        rr, kr, vr, _, lr = refs
        return tuple([q[row:row + 1, :].astype(F32)] for q in (rr, kr, vr, lr))

    def store_pp(pp):
        for p in range(N_PAIRS):
            for j, n in enumerate(_PP_SIDE):
                pa_ref[p, j] = pp[n][p]
            for j, n in enumerate(_PP_STACK):
                pb_ref[p, j] = pp[n][p]
            for j, n in enumerate(_PP_F32):
                pf_ref[p, j] = pp[n][p]
            pg_ref[p] = pp["gtot"][p]

    def load_pp():
        pp = {n: [pa_ref[p, j] for p in range(N_PAIRS)] for j, n in enumerate(_PP_SIDE)}
        pp.update({n: [pb_ref[p, j] for p in range(N_PAIRS)] for j, n in enumerate(_PP_STACK)})
        pp.update({n: [pf_ref[p, j] for p in range(N_PAIRS)] for j, n in enumerate(_PP_F32)})
        pp["gtot"] = [pg_ref[p] for p in range(N_PAIRS)]
        return pp

    cur = (rc_ref, kc_ref, vc_ref, gc_ref, lc_ref)
    nxt = (rn_ref, kn_ref, vn_ref, gn_ref, ln_ref)

    @pl.when(k == 0)
    def _():
        st_ref[...] = jnp.zeros_like(st_ref)
        zero = ([jnp.zeros((1, D_A), F32)], [jnp.zeros((1, D_A), F32)], [jnp.zeros((1, D_A), F32)],
                [jnp.zeros((1, PAIR), F32)])
        store_pp(_run(_prep_steps(raw_rows(cur, 0), zero, prm, w2, cs)))

    pp_a = load_pp()
    _, pp_b = _zip_run(_scan_steps(pp_a, st_ref, o_ref, 0, prm, cs),
                       _prep_steps(raw_rows(cur, rows), prev_rows(cur, rows - 1), prm, w2, cs),
                       _PIPE_PATTERN)
    _, pp_n = _zip_run(_scan_steps(pp_b, st_ref, o_ref, rows, prm, cs),
                       _prep_steps(raw_rows(nxt, 0), prev_rows(cur, 2 * rows - 1), prm, w2, cs),
                       _PIPE_PATTERN)
    store_pp(pp_n)

    last = 2 * rows - 1
    shm_out_ref[0, 0] = rc_ref[last:last + 1, :].astype(F32)
    shm_out_ref[1, 0] = kc_ref[last:last + 1, :].astype(F32)
    shm_out_ref[2, 0] = vc_ref[last:last + 1, :].astype(F32)
    shl_out_ref[0] = lc_ref[last:last + 1, :].astype(F32)

    @pl.when(k == n_steps - 1)
    def _():
        _write_state(st_ref, sout_ref, cs)


def _wkv_prompt(proj, n_seq, seq_len, prm, w2, layer):
    n_steps = seq_len // (2 * ROWS)
    n_chunks = seq_len // ROWS

    def cmap(col):
        return lambda i, k: (i * n_steps + k, col)

    def nmap(col):
        return lambda i, k: (i * n_chunks + jnp.minimum(2 * k + 2, n_chunks - 1), col)

    def specs(rows_, m):
        return [pl.BlockSpec((rows_, D_A), m(P_R // D_A)),
                pl.BlockSpec((rows_, D_A), m(P_K // D_A)),
                pl.BlockSpec((rows_, D_A), m(P_V // D_A)),
                pl.BlockSpec((rows_, D_A), m(P_GA // D_A)),
                pl.BlockSpec((rows_, PAIR), m(P_LORA // PAIR))]

    return pl.pallas_call(
        functools.partial(_wkv_pipe_kernel, n_steps=n_steps),
        grid=(n_seq, n_steps),
        in_specs=specs(2 * ROWS, cmap) + specs(ROWS, nmap) + [
            pl.BlockSpec((1, 16, D_A), lambda i, k: (layer, 0, 0)),
            pl.BlockSpec((1, PAIR, D_A), lambda i, k: (layer, 0, 0))],
        out_specs=[pl.BlockSpec((2 * ROWS, D_A), lambda i, k: (i * n_steps + k, 0)),
                   pl.BlockSpec((1, 1, N_PAIRS, PAIR, HEAD_A), lambda i, k: (0, i, 0, 0, 0)),
                   pl.BlockSpec((3, 1, 1, D_A), lambda i, k: (0, i, 0, 0)),
                   pl.BlockSpec((1, 1, PAIR), lambda i, k: (i, 0, 0))],
        out_shape=[jax.ShapeDtypeStruct((n_seq * seq_len, D_A), BF16),
                   jax.ShapeDtypeStruct((1, n_seq, N_PAIRS, PAIR, HEAD_A), F32),
                   jax.ShapeDtypeStruct((3, n_seq, 1, D_A), F32),
                   jax.ShapeDtypeStruct((n_seq, 1, PAIR), F32)],
        scratch_shapes=[pltpu.VMEM((N_PAIRS, PAIR, PAIR), F32),
                        pltpu.VMEM((N_PAIRS, len(_PP_SIDE), ROWS, PAIR), BF16),
                        pltpu.VMEM((N_PAIRS, len(_PP_STACK), 2 * ROWS, PAIR), BF16),
                        pltpu.VMEM((N_PAIRS, len(_PP_F32), ROWS, PAIR), F32),
                        pltpu.VMEM((N_PAIRS, 1, PAIR), F32)],
        compiler_params=pltpu.CompilerParams(
            dimension_semantics=("parallel", "arbitrary"), vmem_limit_bytes=VMEM_LIMIT),
        name="wkv_pipe",
    )(*([proj] * 10), prm, w2)


HALO = 16


def _window_sums(ext):
    w2 = ext + pltpu.roll(ext, 1, 0)
    w4 = w2 + pltpu.roll(w2, 2, 0)
    w8 = w4 + pltpu.roll(w4, 4, 0)
    w16 = w8 + pltpu.roll(w8, 8, 0)
    return (w2, w4, w8, w16)


def _pool_gate(pooled_groups, gb, pw_ref, pscale):
    mixed = [_dot(pg.astype(BF16), pw_ref[g]) for g, pg in enumerate(pooled_groups)]
    yb = jnp.concatenate(mixed, axis=1) * pscale
    return yb * _silu(gb)


def _layer_norm_v(vc, ln_g):
    vm = jnp.mean(vc, axis=-1, keepdims=True)
    d = vc - vm
    vv = jnp.mean(d * d, axis=-1, keepdims=True)
    return d * lax.rsqrt(vv + LN_EPS) * ln_g


def _chunk_gate(vn, uc, gc, wm_ref, bm_ref):
    n_rows = vn.shape[0]
    vnb = vn.astype(BF16)
    outs = []
    for j in range(n_rows // CHUNK):
        rs = slice(j * CHUNK, (j + 1) * CHUNK)
        mix = [_dot(wm_ref[g], vnb[rs, g * GC:(g + 1) * GC]) + bm_ref[g] for g in range(N_GROUPS_C)]
        outs.append(jnp.concatenate(mix, axis=1))
    mix = outs[0] if len(outs) == 1 else jnp.concatenate(outs, axis=0)
    return uc * mix * _silu(gc)


def _bc_prompt_kernel(ub_ref, gb_ref, uc_ref, vc_ref, gc_ref, pw_ref, ps_ref, lng_ref, wm_ref, bm_ref,
                      o_ref, halo_ref, *, tt):
    j = pl.program_id(1)

    @pl.when(j == 0)
    def _():
        halo_ref[...] = jnp.zeros_like(halo_ref)

    u = ub_ref[...].astype(F32)
    ext = jnp.concatenate([halo_ref[...], u], axis=0)
    halo_ref[...] = u[tt - HALO:, :]
    sums = _window_sums(ext)
    pos = j * tt + lax.broadcasted_iota(jnp.int32, (tt, 1), 0)
    pooled = []
    for g, win in enumerate(POOL_WINDOWS):
        ls = slice(g * POOL_GC, (g + 1) * POOL_GC)
        cnt = jnp.minimum(pos + 1, win).astype(F32)
        pooled.append(sums[g][HALO:, ls] / cnt - u[:, ls])
    gb = gb_ref[...].astype(F32)
    o_ref[:, :D_B] = _pool_gate(pooled, gb, pw_ref, ps_ref[...]).astype(o_ref.dtype)
    vn = _layer_norm_v(vc_ref[...].astype(F32), lng_ref[...])
    uc = uc_ref[...].astype(F32)
    gc = gc_ref[...].astype(F32)
    o_ref[:, D_B:] = _chunk_gate(vn, uc, gc, wm_ref, bm_ref).astype(o_ref.dtype)


def _bc_prompt(proj, n_seq, seq_len, pw, pscale, ln_g, wm, bm, tt=256):
    n_t = seq_len // tt

    def rmap(col):
        return lambda i, j: (i * n_t + j, col)

    wspec = pl.BlockSpec((4, 128, 128), lambda i, j: (0, 0, 0))
    vspec = pl.BlockSpec((1, D_B), lambda i, j: (0, 0))
    return pl.pallas_call(
        functools.partial(_bc_prompt_kernel, tt=tt),
        grid=(n_seq, n_t),
        in_specs=[pl.BlockSpec((tt, D_B), rmap(P_UB // D_B)),
                  pl.BlockSpec((tt, D_B), rmap(P_GB // D_B)),
                  pl.BlockSpec((tt, D_B), rmap(P_UC // D_B)),
                  pl.BlockSpec((tt, D_B), rmap(P_VC // D_B)),
                  pl.BlockSpec((tt, D_B), rmap(P_GC // D_B)),
                  wspec, vspec, vspec, wspec, wspec],
        out_specs=pl.BlockSpec((tt, D_B + D_C), rmap(0)),
        out_shape=jax.ShapeDtypeStruct((n_seq * seq_len, D_B + D_C), BF16),
        scratch_shapes=[pltpu.VMEM((HALO, D_B), F32)],
        compiler_params=pltpu.CompilerParams(
            dimension_semantics=("parallel", "arbitrary"), vmem_limit_bytes=VMEM_LIMIT),
        name="bc_prompt",
    )(proj, proj, proj, proj, proj, pw, pscale, ln_g, wm, bm)


def _bc_sample_kernel(buf_ref, ub_ref, gb_ref, uc_ref, vc_ref, gc_ref, pw_ref, ps_ref, lng_ref,
                      wm_ref, bm_ref, o_ref, vn_ref, *, nb, t_len):
    u3 = ub_ref[...].astype(F32)
    ext = jnp.concatenate([buf_ref[...], u3], axis=1)
    per = HALO + t_len
    sums = _window_sums(ext.reshape(nb * per, D_B))
    u = u3.reshape(nb * t_len, D_B)
    pooled = []
    for g, win in enumerate(POOL_WINDOWS):
        ls = slice(g * POOL_GC, (g + 1) * POOL_GC)
        s3 = sums[g].reshape(nb, per, D_B)[:, HALO:, ls].reshape(nb * t_len, POOL_GC)
        cnt = float(min(PAST_LEN + 1, win))
        pooled.append(s3 / cnt - u[:, ls])
    rows = nb * t_len
    gb = gb_ref[...].astype(F32).reshape(rows, D_B)
    o_ref[:, :D_B] = _pool_gate(pooled, gb, pw_ref, ps_ref[...]).astype(o_ref.dtype)
    vn = _layer_norm_v(vc_ref[...].astype(F32).reshape(rows, D_C), lng_ref[...])
    vn_ref[...] = vn
    uc = uc_ref[...].astype(F32).reshape(rows, D_C)
    gc = gc_ref[...].astype(F32).reshape(rows, D_C)
    o_ref[:, D_B:] = _chunk_gate(vn, uc, gc, wm_ref, bm_ref).astype(o_ref.dtype)


def _bc_sample(proj3, seq0, n_seq, t_len, buf16, pw, pscale, ln_g, wm, bm):
    nb = CHUNK // t_len
    sb0 = seq0 // nb

    def rmap(col):
        return lambda i: (sb0 + i, 0, col)

    wspec = pl.BlockSpec((4, 128, 128), lambda i: (0, 0, 0))
    vspec = pl.BlockSpec((1, D_B), lambda i: (0, 0))
    rows = nb * t_len
    return pl.pallas_call(
        functools.partial(_bc_sample_kernel, nb=nb, t_len=t_len),
        grid=(n_seq // nb,),
        in_specs=[pl.BlockSpec((nb, HALO, D_B), lambda i: (i, 0, 0)),
                  pl.BlockSpec((nb, t_len, D_B), rmap(P_UB // D_B)),
                  pl.BlockSpec((nb, t_len, D_B), rmap(P_GB // D_B)),
                  pl.BlockSpec((nb, t_len, D_B), rmap(P_UC // D_B)),
                  pl.BlockSpec((nb, t_len, D_B), rmap(P_VC // D_B)),
                  pl.BlockSpec((nb, t_len, D_B), rmap(P_GC // D_B)),
                  wspec, vspec, vspec, wspec, wspec],
        out_specs=[pl.BlockSpec((rows, D_B + D_C), lambda i: (i, 0)),
                   pl.BlockSpec((rows, D_C), lambda i: (i, 0))],
        out_shape=[jax.ShapeDtypeStruct((n_seq * t_len, D_B + D_C), BF16),
                   jax.ShapeDtypeStruct((n_seq * t_len, D_C), F32)],
        compiler_params=pltpu.CompilerParams(
            dimension_semantics=("parallel",), vmem_limit_bytes=VMEM_LIMIT),
        name="bc_sample",
    )(buf16, proj3, proj3, proj3, proj3, proj3, pw, pscale, ln_g, wm, bm)


WPREP_ROWS = 256


def _wprep_kernel(w_ref, o_ref):
    o_ref[0, :, :3 * D_A] = w_ref[0, :, :3 * D_A].astype(BF16)
    o_ref[0, :, 3 * D_A:P_LORA] = w_ref[0, :, SHIFT_W:].astype(BF16)
    o_ref[0, :, P_LORA:] = w_ref[0, :, 3 * D_A:SHIFT_W].astype(BF16)


def _prep_w_in(w_in):
    spec = pl.BlockSpec((1, WPREP_ROWS, D_INP), lambda l, i: (l, i, 0))
    return pl.pallas_call(
        _wprep_kernel,
        grid=(DEPTH, D_MODEL // WPREP_ROWS),
        in_specs=[spec],
        out_specs=spec,
        out_shape=jax.ShapeDtypeStruct((DEPTH, D_MODEL, D_INP), BF16),
        compiler_params=pltpu.CompilerParams(
            dimension_semantics=("parallel", "parallel"), vmem_limit_bytes=VMEM_LIMIT),
        name="wprep",
    )(w_in)


def kernel(x_prompt, x_sample, state_shift, state_wkv, state_pool, norm_g, final_norm_g, w_in,
           shift_mu, w0, w_up, a0, a_up, k_k, k_a, r_k, lnx_g, lnx_b, pool_w, pool_scale,
           gmlp_ln_g, gmlp_ws, gmlp_b, w_out):
    bp, seq, _ = x_prompt.shape
    bs, dseq, _ = x_sample.shape
    n_p = bp * seq
    n_s = bs * dseq
    xp = x_prompt.reshape(n_p, D_MODEL)
    xs = x_sample.reshape(n_s, D_MODEL)
    hp = _norm_rows(xp, norm_g[0][None])
    hs = _norm_rows(xs, norm_g[0][None])

    w_in_p = _prep_w_in(w_in)
    w_out_h = w_out.astype(BF16)
    mu_l = jnp.pad(shift_mu[:, 3 * D_A:], ((0, 0), (0, D_A - 2 * LORA)))
    prm = jnp.stack([shift_mu[:, :D_A], shift_mu[:, D_A:2 * D_A], shift_mu[:, 2 * D_A:3 * D_A],
                     w0, a0, k_k, k_a, r_k.reshape(DEPTH, D_A), lnx_g, lnx_b, mu_l], axis=1)
    prm = jnp.pad(prm, ((0, 0), (0, 16 - prm.shape[1]), (0, 0)))
    w2 = jnp.concatenate([w_up, a_up], axis=1)
    pw = pool_w.astype(BF16)
    tril = jnp.tril(jnp.ones((CHUNK, CHUNK), F32))
    wm_p = (gmlp_ws * tril).astype(BF16)
    bm_p = jnp.broadcast_to(gmlp_b[:, :, :, None], (DEPTH, N_GROUPS_C, CHUNK, GC))
    nb_s = CHUNK // dseq
    eye_b = jnp.eye(nb_s, dtype=F32)
    ws_small = gmlp_ws[:, :, :dseq, :dseq] * tril[:dseq, :dseq]
    wm_s = jnp.einsum('ab,lgts->lgatbs', eye_b, ws_small).reshape(DEPTH, N_GROUPS_C, CHUNK, CHUNK)
    wm_s = wm_s.astype(BF16)
    bm_s = jnp.broadcast_to(jnp.tile(gmlp_b[:, :, :dseq], (1, 1, nb_s))[:, :, :, None],
                            (DEPTH, N_GROUPS_C, CHUNK, GC))

    ssh_main = state_shift[:, :, :3 * D_A].reshape(DEPTH, bs, 3, 1, D_A).transpose(0, 2, 1, 3, 4)
    ssh_lora = state_shift[:, :, 3 * D_A:].reshape(DEPTH, bs, 1, PAIR)
    wkv_s = state_wkv.reshape(DEPTH, bs, N_PAIRS, PAIR, HEAD_A)
    buf16 = jnp.pad(state_pool, ((0, 0), (0, 0), (HALO - POOL_BUF, 0), (0, 0)))

    p_shift, p_wkv, p_pool, s_shift, s_pool, s_v = [], [], [], [], [], []
    for l in range(DEPTH):
        final = l == DEPTH - 1
        g_out = final_norm_g[None] if final else norm_g[l + 1][None]
        bc_w = (pw[l], pool_scale[l][None], gmlp_ln_g[l][None])

        proj_p = _inproj_h(hp, w_in_p, l, 2048)
        ya_p, wk_p, shm_p, shl_p = _wkv_prompt(proj_p, bp, seq, prm, w2, l)
        cb_p = _bc_prompt(proj_p, bp, seq, *bc_w, wm_p[l], bm_p[l])
        xp = _outproj(ya_p, cb_p, w_out_h, l, xp, g_out, final)
        if not final:
            xp, hp = xp

        proj_s = _inproj_h(hs, w_in_p, l, 1024)
        ya_s, wkv_s, shm_s, shl_s = _wkv_sample(proj_s, bs, dseq, ROWS // dseq, ssh_main, ssh_lora,
                                                wkv_s, prm, w2, l)
        cb_s, vn_s = _bc_sample(proj_s.reshape(bs, dseq, D_INP), 0, bs, dseq, buf16[l], *bc_w,
                                wm_s[l], bm_s[l])
        xs = _outproj(ya_s, cb_s, w_out_h, l, xs, g_out, final)
        if not final:
            xs, hs = xs

        p_shift.append(jnp.concatenate([shm_p[0, :, 0], shm_p[1, :, 0], shm_p[2, :, 0], shl_p[:, 0]], axis=-1))
        s_shift.append(jnp.concatenate([shm_s[0, :, 0], shm_s[1, :, 0], shm_s[2, :, 0], shl_s[:, 0]], axis=-1))
        p_wkv.append(wk_p.reshape(bp, N_HEADS_A, HEAD_A, HEAD_A))
        p_pool.append(jnp.stack([proj_p[(b + 1) * seq - POOL_BUF:(b + 1) * seq, P_UB:P_UB + D_B]
                                 for b in range(bp)]).astype(F32))
        ub_s = proj_s[:, P_UB:P_UB + D_B].astype(F32).reshape(bs, dseq, D_B)
        s_pool.append(jnp.concatenate([state_pool[l], ub_s], axis=1)[:, -POOL_BUF:])
        s_v.append(vn_s.reshape(bs, dseq, D_C))

    y_prompt = xp.reshape(bp, seq, D_MODEL)
    y_sample = xs.reshape(bs, dseq, D_MODEL)
    s_wkv = wkv_s.reshape(DEPTH, bs, N_HEADS_A, HEAD_A, HEAD_A)
    return (y_prompt, y_sample, jnp.stack(p_shift), jnp.stack(p_wkv), jnp.stack(p_pool),
            jnp.stack(s_shift), s_wkv, jnp.stack(s_pool), jnp.stack(s_v))
```

```python
import functools

import jax
import jax.numpy as jnp
import numpy as np
from jax import lax
from jax.experimental import pallas as pl
from jax.experimental.pallas import tpu as pltpu

F32 = jnp.float32
BF16 = jnp.bfloat16

D_MODEL = 2048
DEPTH = 4
PAST_LEN = 16384
D_A = 1024
HEAD_A = 64
N_HEADS_A = 16
LORA = 64
D_B = 512
POOL_WINDOWS = (2, 4, 8, 16)
POOL_GC = 128
POOL_BUF = 15
D_C = 512
N_GROUPS_C = 4
GC = 128
CHUNK = 128
SHIFT_W = 3 * D_A + 2 * LORA
EPS = 1e-6
GN_EPS = HEAD_A * 1e-5
LN_EPS = 1e-5

P_R, P_K, P_V, P_GA = 0, 1024, 2048, 3072
P_UB, P_GB, P_UC, P_VC, P_GC = 4096, 4608, 5120, 5632, 6144
P_LORA = 6656
D_INP = 6784

PAIR = 128
N_PAIRS = D_A // PAIR
ROWS = 64

PROJ_DTYPE = BF16
VMEM_LIMIT = 52 * 1024 * 1024
HI = lax.Precision.HIGHEST


def _dot(a, b, prec=None):
    return jnp.dot(a, b, precision=prec, preferred_element_type=F32)


def _dot_nt(a, b, prec=None):
    return lax.dot_general(a, b, (((1,), (1,)), ((), ())), precision=prec,
                           preferred_element_type=F32)


def _dot_tn(a, b, prec=None):
    return lax.dot_general(a, b, (((0,), (0,)), ((), ())), precision=prec,
                           preferred_element_type=F32)


def _split(x, n):
    pieces = []
    rem = x
    for i in range(n):
        hi = rem.astype(BF16)
        pieces.append(hi)
        if i + 1 < n:
            rem = rem - hi.astype(F32)
    return pieces


_CONTRACT = {"nn": (1, 0), "nt": (1, 1), "tn": (0, 0)}


def _mm(a, b, mode="nn", pa=1, pb=1):
    ca, cb = _CONTRACT[mode]
    sa, sb = _split(a, pa), _split(b, pb)
    terms = [(i, j) for i in range(pa) for j in range(pb) if i + j < max(pa, pb)]
    lhs = jnp.concatenate([sa[i] for i, _ in terms], axis=ca) if len(terms) > 1 else sa[0]
    rhs = jnp.concatenate([sb[j] for _, j in terms], axis=cb) if len(terms) > 1 else sb[0]
    return lax.dot_general(lhs, rhs, (((ca,), (cb,)), ((), ())), preferred_element_type=F32)


PREC_LORA = (1, 1)
PREC_CUM = (1, 2)
PREC_SEG = (1, 1)
PREC_G = (1, 1)
PREC_INV = (1, 1)
PREC_X2 = (1, 1)
PREC_WU = (1, 1)
PREC_US = (1, 1)
PREC_UPD = (1, 1)
PREC_Y = (1, 1)


def _norm_kernel(x_ref, g_ref, o_ref):
    x = x_ref[...]
    ms = jnp.mean(x * x, axis=-1, keepdims=True)
    o_ref[...] = ((x * lax.rsqrt(ms + EPS)) * g_ref[...]).astype(BF16)


def _norm_rows(x, g, tm=512):
    m = x.shape[0]
    return pl.pallas_call(
        _norm_kernel,
        grid=(m // tm,),
        in_specs=[pl.BlockSpec((tm, D_MODEL), lambda i: (i, 0)),
                  pl.BlockSpec((1, D_MODEL), lambda i: (0, 0))],
        out_specs=pl.BlockSpec((tm, D_MODEL), lambda i: (i, 0)),
        out_shape=jax.ShapeDtypeStruct((m, D_MODEL), BF16),
        compiler_params=pltpu.CompilerParams(
            dimension_semantics=("parallel",), vmem_limit_bytes=VMEM_LIMIT),
        name="norm_rows",
    )(x, g)


def _inproj_h_kernel(h_ref, w_ref, o_ref):
    o_ref[...] = _dot(h_ref[...], w_ref[0]).astype(o_ref.dtype)


def _inproj_h(h, w, layer, tm, tn=768):
    m = h.shape[0]
    return pl.pallas_call(
        _inproj_h_kernel,
        grid=(m // tm, pl.cdiv(D_INP, tn)),
        in_specs=[pl.BlockSpec((tm, D_MODEL), lambda i, j: (i, 0)),
                  pl.BlockSpec((1, D_MODEL, tn), lambda i, j: (layer, 0, j))],
        out_specs=pl.BlockSpec((tm, tn), lambda i, j: (i, j)),
        out_shape=jax.ShapeDtypeStruct((m, D_INP), PROJ_DTYPE),
        compiler_params=pltpu.CompilerParams(
            dimension_semantics=("parallel", "arbitrary"), vmem_limit_bytes=VMEM_LIMIT),
        name="inproj_h",
    )(h, w)


def _outproj_kernel(ca_ref, cb_ref, wa_ref, wb_ref, x_ref, g_ref, *o_refs, final):
    y = _dot(ca_ref[...], wa_ref[0]) + _dot(cb_ref[...], wb_ref[0])
    out = x_ref[...] + y
    ms = jnp.mean(out * out, axis=-1, keepdims=True)
    normed = (out * lax.rsqrt(ms + EPS)) * g_ref[...]
    if final:
        o_refs[0][...] = normed
    else:
        o_refs[0][...] = out
        o_refs[1][...] = normed.astype(BF16)


def _outproj(cat_a, cat_b, w, layer, x, g, final, tm=512):
    m = x.shape[0]
    half = D_MODEL // 2
    row_spec = pl.BlockSpec((tm, D_MODEL), lambda i: (i, 0))
    f32_out = jax.ShapeDtypeStruct((m, D_MODEL), F32)
    return pl.pallas_call(
        functools.partial(_outproj_kernel, final=final),
        grid=(m // tm,),
        in_specs=[pl.BlockSpec((tm, half), lambda i: (i, 0)),
                  pl.BlockSpec((tm, half), lambda i: (i, 0)),
                  pl.BlockSpec((1, half, D_MODEL), lambda i: (layer, 0, 0)),
                  pl.BlockSpec((1, half, D_MODEL), lambda i: (layer, 1, 0)),
                  pl.BlockSpec((tm, D_MODEL), lambda i: (i, 0)),
                  pl.BlockSpec((1, D_MODEL), lambda i: (0, 0))],
        out_specs=row_spec if final else [row_spec, row_spec],
        out_shape=f32_out if final else [f32_out, jax.ShapeDtypeStruct((m, D_MODEL), BF16)],
        compiler_params=pltpu.CompilerParams(
            dimension_semantics=("parallel",), vmem_limit_bytes=VMEM_LIMIT),
        name="outproj",
    )(cat_a, cat_b, w, w, x, g)


PR_MU_R, PR_MU_K, PR_MU_V, PR_W0, PR_A0, PR_KK, PR_KA, PR_RK, PR_LNG, PR_LNB, PR_MU_L = range(11)


EXP_M_HALF = float(np.exp(-0.5))


def _sigmoid(z):
    return 1.0 / (1.0 + jnp.exp(-z))


def _silu(z):
    return z * _sigmoid(z)


def _run(gen):
    try:
        while True:
            next(gen)
    except StopIteration as e:
        return e.value


def _zip_run(main, prep, pattern):
    done = {}

    def step(gen, key):
        if key not in done:
            try:
                next(gen)
            except StopIteration as e:
                done[key] = e.value

    for ch in pattern:
        step(main if ch == "m" else prep, ch)
    while "m" not in done:
        step(main, "m")
    while "p" not in done:
        step(prep, "p")
    return done["m"], done["p"]


class _ScanConsts:
    def __init__(self, C, NB):
        rows = ROWS
        self.C, self.NB = C, NB
        row_id = lax.broadcasted_iota(jnp.int32, (rows, 1), 0)
        self.first = (row_id % C) == 0
        self.lo_half = lax.broadcasted_iota(jnp.int32, (rows, PAIR), 1) < HEAD_A
        ri = lax.broadcasted_iota(jnp.int32, (rows, rows), 0)
        ci = lax.broadcasted_iota(jnp.int32, (rows, rows), 1)
        self.tri = jnp.where(((ri // C) == (ci // C)) & (ri >= ci), 1.0, 0.0)
        pr = lax.broadcasted_iota(jnp.int32, (PAIR, PAIR), 0)
        pc = lax.broadcasted_iota(jnp.int32, (PAIR, PAIR), 1)
        self.same_head = (pr // HEAD_A) == (pc // HEAD_A)
        self.seg = jnp.where(self.same_head, 1.0, 0.0)
        rp = lax.broadcasted_iota(jnp.int32, (rows, PAIR), 0)
        cp = lax.broadcasted_iota(jnp.int32, (rows, PAIR), 1) % rows
        same_seq = (rp // C) == (cp // C)
        self.m_strict = same_seq & (rp > cp)
        m_incl = same_seq & (rp >= cp)
        self.m_incl2 = jnp.concatenate([m_incl, m_incl], axis=1)
        self.eye = jnp.where(rp == cp, 1.0, 0.0)


def _bmask(m, xb):
    return jnp.where(m, xb, jnp.zeros_like(xb))


_PP_SIDE = ("at", "rt", "kg", "bg", "vb")
_PP_STACK = ("ats", "khs", "bhs", "vs")
_PP_F32 = ("bon", "v", "sg")


def _stack(xb, lo_half):
    zb = jnp.zeros_like(xb)
    return jnp.concatenate([jnp.where(lo_half, xb, zb), jnp.where(lo_half, zb, xb)], axis=0)


def _prep_steps(raw, prev, prm, w2, cs):
    C, NB, rows = cs.C, cs.NB, ROWS
    r_raw, k_raw, v_raw, g_raw, l_raw = raw
    prev_r, prev_k, prev_v, prev_l = prev

    def prow(i):
        return prm[i:i + 1, :]

    def shift_mix(x, prev_rows, mu):
        rolled = pltpu.roll(x, 1, 0)
        parts = []
        for b, q in enumerate(prev_rows):
            head = jnp.where(cs.first[:8], jnp.broadcast_to(q, (8, x.shape[1])), rolled[b * C:b * C + 8])
            parts += [head] if C == 8 else [head, rolled[b * C + 8:(b + 1) * C]]
        shifted = jnp.concatenate(parts, axis=0)
        return x + (shifted - x) * mu

    xl = shift_mix(l_raw, prev_l, prow(PR_MU_L)[:, :PAIR])
    lora_w = _mm(jnp.where(cs.lo_half, jnp.tanh(xl), 0.0), w2, "nn", *PREC_LORA)
    lora_a = _mm(jnp.where(cs.lo_half, 0.0, xl), w2, "nn", *PREC_LORA)
    yield
    r = shift_mix(r_raw, prev_r, prow(PR_MU_R))
    k = shift_mix(k_raw, prev_k, prow(PR_MU_K))
    v = shift_mix(v_raw, prev_v, prow(PR_MU_V))
    dec = EXP_M_HALF * _sigmoid(prow(PR_W0) + lora_w)
    a = _sigmoid(prow(PR_A0) + lora_a)
    cum = _mm(cs.tri, -dec, "nn", *PREC_CUM)
    yield
    kk_raw = k * prow(PR_KK)
    k2 = k * (1.0 + (a - 1.0) * prow(PR_KA))
    rkr = r * k2 * prow(PR_RK)
    sls = [slice(p * PAIR, (p + 1) * PAIR) for p in range(N_PAIRS)]
    s0 = [_mm(jnp.concatenate([kk_raw[:, sl] * kk_raw[:, sl], rkr[:, sl]], axis=0), cs.seg, "nn", *PREC_SEG)
          for sl in sls]
    yield
    g_incl = jnp.exp(cum)
    g_excl = jnp.exp(cum + dec)
    g_inv = jnp.exp(-cum)
    g_tot_rows = [jnp.exp(cum[(b + 1) * C - 1:(b + 1) * C, :]) for b in range(NB)]
    g_rest = g_inv * jnp.concatenate([jnp.broadcast_to(q, (C, D_A)) for q in g_tot_rows], axis=0)
    sg = _silu(g_raw)

    pp = {n: [] for n in _PP_SIDE + _PP_STACK + _PP_F32 + ("gtot",)}
    for sl, q in zip(sls, s0):
        kkp = kk_raw[:, sl] * lax.rsqrt(jnp.maximum(q[:rows], 1e-12))
        kka = kkp * a[:, sl]
        at = (-kkp * g_excl[:, sl]).astype(BF16)
        vb = v[:, sl].astype(BF16)
        pp["at"].append(at)
        pp["rt"].append((r[:, sl] * g_incl[:, sl]).astype(BF16))
        pp["kg"].append((k2[:, sl] * g_rest[:, sl]).astype(BF16))
        pp["bg"].append((kka * g_rest[:, sl]).astype(BF16))
        pp["vb"].append(vb)
        pp["ats"].append(_stack(at, cs.lo_half))
        pp["khs"].append(_stack((k2[:, sl] * g_inv[:, sl]).astype(BF16), cs.lo_half))
        pp["bhs"].append(_stack((kka * g_inv[:, sl]).astype(BF16), cs.lo_half))
        pp["vs"].append(_stack(vb, cs.lo_half))
        pp["bon"].append(q[rows:])
        pp["v"].append(v[:, sl])
        pp["sg"].append(sg[:, sl])
        pp["gtot"].append(jnp.concatenate([q[:, sl] for q in g_tot_rows], axis=0))
    return pp


def _scan_steps(pp, st_ref, o_ref, row0, prm, cs):
    C, NB, rows = cs.C, cs.NB, ROWS
    n_rounds = int(np.log2(C))
    pairs = range(N_PAIRS)

    def prow(i):
        return prm[i:i + 1, :]

    def seq_rows(x, b):
        if NB == 1:
            return x
        return x.astype(F32)[b * C:(b + 1) * C]

    def unseq_rows(pieces):
        return pieces[0] if NB == 1 else jnp.concatenate(pieces, axis=0)

    a_ak, a_r, pwb, t_inv = [], [], [], []
    for p in pairs:
        gm = _mm(jnp.concatenate([pp["at"][p], pp["rt"][p]], axis=0),
                 jnp.concatenate([pp["khs"][p], pp["bhs"][p]], axis=0), "nt", *PREC_G)
        gmb = gm.astype(BF16)
        a_ak.append(_bmask(cs.m_strict, gmb[:rows, :PAIR]))
        a_r.append(_bmask(cs.m_incl2, gmb[rows:, :]))
        pwb.append(_bmask(cs.m_strict, gmb[:rows, PAIR:]))
        t_inv.append(cs.eye + jnp.where(cs.m_strict, gm[:rows, PAIR:], 0.0))
    yield

    x2 = []
    for p in pairs:
        x2.append(_mm(a_ak[p], pp["vs"][p], "nn", *PREC_X2).astype(BF16))
        pwb[p] = _mm(pwb[p], _stack(pwb[p], cs.lo_half), "nn", *PREC_INV).astype(BF16)
    yield
    for rd in range(1, n_rounds):
        for p in pairs:
            tb = t_inv[p].astype(BF16)
            pws = _stack(pwb[p], cs.lo_half)
            if rd < n_rounds - 1:
                res = _mm(jnp.concatenate([pwb[p], tb], axis=0), pws, "nn", *PREC_INV)
                t_inv[p] = t_inv[p] + res[rows:]
                pwb[p] = res[:rows].astype(BF16)
            else:
                t_inv[p] = t_inv[p] + _mm(tb, pws, "nn", *PREC_INV)
        yield

    w_t, u_t = [], []
    for p in pairs:
        wu = _mm(t_inv[p], jnp.concatenate([pp["ats"][p], _stack(x2[p], cs.lo_half)], axis=1),
                 "nn", *PREC_WU)
        w_t.append(wu[:, :PAIR].astype(BF16))
        u_t.append(wu[:, PAIR:])
    yield

    s_old = [[st_ref[p * NB + b] for b in range(NB)] for p in pairs]
    u_b, rs = [], []
    for p in pairs:
        us = [_mm(jnp.concatenate([seq_rows(w_t[p], b), seq_rows(pp["rt"][p], b)], axis=0),
                  s_old[p][b], "nt", *PREC_US) for b in range(NB)]
        u_b.append([us[b][:C] + seq_rows(u_t[p], b) for b in range(NB)])
        rs.append(unseq_rows([us[b][C:] for b in range(NB)]))
    yield
    for p in pairs:
        for b in range(NB):
            upd = _mm(jnp.concatenate([seq_rows(pp["vb"][p], b), u_b[p][b]], axis=0),
                      jnp.concatenate([seq_rows(pp["kg"][p], b), seq_rows(pp["bg"][p], b)], axis=0),
                      "tn", *PREC_UPD)
            st_ref[p * NB + b] = s_old[p][b] * pp["gtot"][p][b:b + 1, :] + jnp.where(cs.same_head, upd, 0.0)
    yield
    y = []
    for p in pairs:
        u_s = _stack(unseq_rows(u_b[p]).astype(BF16), cs.lo_half)
        y.append(rs[p] + _mm(a_r[p], jnp.concatenate([pp["vs"][p], u_s], axis=0), "nn", *PREC_Y))
    yield
    yc = []
    for p in pairs:
        yc.append(y[p] - _mm(y[p], cs.seg, "nn", *PREC_SEG) * (1.0 / HEAD_A))
    yield
    for p in pairs:
        sl = slice(p * PAIR, (p + 1) * PAIR)
        yv = _mm(yc[p] * yc[p], cs.seg, "nn", *PREC_SEG) * (1.0 / HEAD_A)
        yn = yc[p] * lax.rsqrt(yv + GN_EPS) * prow(PR_LNG)[:, sl] + prow(PR_LNB)[:, sl]
        o_ref[row0:row0 + rows, sl] = ((yn + pp["bon"][p] * pp["v"][p]) * pp["sg"][p]).astype(o_ref.dtype)


def _write_state(st_ref, sout_ref, cs, b0=0):
    for b in range(cs.NB):
        for p in range(N_PAIRS):
            sm = jnp.where(cs.same_head, st_ref[p * cs.NB + b], 0.0)
            sout_ref[0, b0 + b, p] = sm[:, :HEAD_A] + sm[:, HEAD_A:]


_PIPE_PATTERN = "pmmpmmmmpmmp"


def _wkv_kernel(r_ref, k_ref, v_ref, g_ref, l_ref, sh_ref, shl_ref, s0_ref, prm_ref, w2_ref,
                o_ref, sout_ref, shm_out_ref, shl_out_ref, st_ref, *, C, NB, NBLK):
    cs = _ScanConsts(C, NB)
    prm = prm_ref[0]
    w2 = w2_ref[0]
    refs = (r_ref, k_ref, v_ref, g_ref, l_ref)
    for q in range(NBLK * NB):
        for p in range(N_PAIRS):
            s = s0_ref[0, q, p]
            s2 = jnp.concatenate([s, s], axis=1)
            st_ref[q // NB, p * NB + q % NB] = jnp.where(cs.same_head, s2, 0.0)

    def prep(blk):
        raw = tuple(q[blk * ROWS:(blk + 1) * ROWS, :].astype(F32) for q in refs)
        seqs = range(blk * NB, (blk + 1) * NB)
        prev = ([sh_ref[0, 0, q] for q in seqs], [sh_ref[0, 1, q] for q in seqs],
                [sh_ref[0, 2, q] for q in seqs], [shl_ref[0, q] for q in seqs])
        return _prep_steps(raw, prev, prm, w2, cs)

    pp = _run(prep(0))
    for blk in range(NBLK):
        scan = _scan_steps(pp, st_ref.at[blk], o_ref, blk * ROWS, prm, cs)
        if blk + 1 < NBLK:
            _, pp = _zip_run(scan, prep(blk + 1), _PIPE_PATTERN)
        else:
            _run(scan)

    for q in range(NBLK * NB):
        last = (q + 1) * C - 1
        shm_out_ref[0, q] = r_ref[last:last + 1, :].astype(F32)
        shm_out_ref[1, q] = k_ref[last:last + 1, :].astype(F32)
        shm_out_ref[2, q] = v_ref[last:last + 1, :].astype(F32)
        shl_out_ref[q] = l_ref[last:last + 1, :].astype(F32)
    for blk in range(NBLK):
        _write_state(st_ref.at[blk], sout_ref, cs, blk * NB)


def _wkv_sample(proj, n_seq, C, NB, sh_main, sh_lora, s0, prm, w2, layer, nblk=2):
    assert NB * C == ROWS
    nsq = NB * nblk
    rows = ROWS * nblk

    def rmap(col):
        return lambda i: (i, col)

    state_spec = pl.BlockSpec((1, nsq, N_PAIRS, PAIR, HEAD_A), lambda i: (layer, i, 0, 0, 0))
    return pl.pallas_call(
        functools.partial(_wkv_kernel, C=C, NB=NB, NBLK=nblk),
        grid=(n_seq // nsq,),
        in_specs=[pl.BlockSpec((rows, D_A), rmap(P_R // D_A)),
                  pl.BlockSpec((rows, D_A), rmap(P_K // D_A)),
                  pl.BlockSpec((rows, D_A), rmap(P_V // D_A)),
                  pl.BlockSpec((rows, D_A), rmap(P_GA // D_A)),
                  pl.BlockSpec((rows, PAIR), rmap(P_LORA // PAIR)),
                  pl.BlockSpec((1, 3, nsq, 1, D_A), lambda i: (layer, 0, i, 0, 0)),
                  pl.BlockSpec((1, nsq, 1, PAIR), lambda i: (layer, i, 0, 0)),
                  state_spec,
                  pl.BlockSpec((1, 16, D_A), lambda i: (layer, 0, 0)),
                  pl.BlockSpec((1, PAIR, D_A), lambda i: (layer, 0, 0))],
        out_specs=[pl.BlockSpec((rows, D_A), lambda i: (i, 0)),
                   state_spec,
                   pl.BlockSpec((3, nsq, 1, D_A), lambda i: (0, i, 0, 0)),
                   pl.BlockSpec((nsq, 1, PAIR), lambda i: (i, 0, 0))],
        out_shape=[jax.ShapeDtypeStruct((n_seq * C, D_A), BF16),
                   jax.ShapeDtypeStruct(s0.shape, F32),
                   jax.ShapeDtypeStruct((3, n_seq, 1, D_A), F32),
                   jax.ShapeDtypeStruct((n_seq, 1, PAIR), F32)],
        scratch_shapes=[pltpu.VMEM((nblk, N_PAIRS * NB, PAIR, PAIR), F32)],
        input_output_aliases={7: 1},
        compiler_params=pltpu.CompilerParams(
            dimension_semantics=("parallel",), vmem_limit_bytes=VMEM_LIMIT),
        name="wkv_c%d" % C,
    )(proj, proj, proj, proj, proj, sh_main, sh_lora, s0, prm, w2)


def _wkv_pipe_kernel(rc_ref, kc_ref, vc_ref, gc_ref, lc_ref, rn_ref, kn_ref, vn_ref, gn_ref, ln_ref,
                     prm_ref, w2_ref, o_ref, sout_ref, shm_out_ref, shl_out_ref,
                     st_ref, pa_ref, pb_ref, pf_ref, pg_ref, *, n_steps):
    k = pl.program_id(1)
    rows = ROWS
    cs = _ScanConsts(rows, 1)
    prm = prm_ref[0]
    w2 = w2_ref[0]

    def raw_rows(refs, lo):
        return tuple(q[lo:lo + rows, :].astype(F32) for q in refs)

    def prev_rows(refs, row):
        rr, kr, vr, _, lr = refs
        return tuple([q[row:row + 1, :].astype(F32)] for q in (rr, kr, vr, lr))

    def store_pp(pp):
        for p in range(N_PAIRS):
            for j, n in enumerate(_PP_SIDE):
                pa_ref[p, j] = pp[n][p]
            for j, n in enumerate(_PP_STACK):
                pb_ref[p, j] = pp[n][p]
            for j, n in enumerate(_PP_F32):
                pf_ref[p, j] = pp[n][p]
            pg_ref[p] = pp["gtot"][p]

    def load_pp():
        pp = {n: [pa_ref[p, j] for p in range(N_PAIRS)] for j, n in enumerate(_PP_SIDE)}
        pp.update({n: [pb_ref[p, j] for p in range(N_PAIRS)] for j, n in enumerate(_PP_STACK)})
        pp.update({n: [pf_ref[p, j] for p in range(N_PAIRS)] for j, n in enumerate(_PP_F32)})
        pp["gtot"] = [pg_ref[p] for p in range(N_PAIRS)]
        return pp

    cur = (rc_ref, kc_ref, vc_ref, gc_ref, lc_ref)
    nxt = (rn_ref, kn_ref, vn_ref, gn_ref, ln_ref)

    @pl.when(k == 0)
    def _():
        st_ref[...] = jnp.zeros_like(st_ref)
        zero = ([jnp.zeros((1, D_A), F32)], [jnp.zeros((1, D_A), F32)], [jnp.zeros((1, D_A), F32)],
                [jnp.zeros((1, PAIR), F32)])
        store_pp(_run(_prep_steps(raw_rows(cur, 0), zero, prm, w2, cs)))

    pp_a = load_pp()
    _, pp_b = _zip_run(_scan_steps(pp_a, st_ref, o_ref, 0, prm, cs),
                       _prep_steps(raw_rows(cur, rows), prev_rows(cur, rows - 1), prm, w2, cs),
                       _PIPE_PATTERN)
    _, pp_n = _zip_run(_scan_steps(pp_b, st_ref, o_ref, rows, prm, cs),
                       _prep_steps(raw_rows(nxt, 0), prev_rows(cur, 2 * rows - 1), prm, w2, cs),
                       _PIPE_PATTERN)
    store_pp(pp_n)

    last = 2 * rows - 1
    shm_out_ref[0, 0] = rc_ref[last:last + 1, :].astype(F32)
    shm_out_ref[1, 0] = kc_ref[last:last + 1, :].astype(F32)
    shm_out_ref[2, 0] = vc_ref[last:last + 1, :].astype(F32)
    shl_out_ref[0] = lc_ref[last:last + 1, :].astype(F32)

    @pl.when(k == n_steps - 1)
    def _():
        _write_state(st_ref, sout_ref, cs)


def _wkv_prompt(proj, n_seq, seq_len, prm, w2, layer):
    n_steps = seq_len // (2 * ROWS)
    n_chunks = seq_len // ROWS

    def cmap(col):
        return lambda i, k: (i * n_steps + k, col)

    def nmap(col):
        return lambda i, k: (i * n_chunks + jnp.minimum(2 * k + 2, n_chunks - 1), col)

    def specs(rows_, m):
        return [pl.BlockSpec((rows_, D_A), m(P_R // D_A)),
                pl.BlockSpec((rows_, D_A), m(P_K // D_A)),
                pl.BlockSpec((rows_, D_A), m(P_V // D_A)),
                pl.BlockSpec((rows_, D_A), m(P_GA // D_A)),
                pl.BlockSpec((rows_, PAIR), m(P_LORA // PAIR))]

    return pl.pallas_call(
        functools.partial(_wkv_pipe_kernel, n_steps=n_steps),
        grid=(n_seq, n_steps),
        in_specs=specs(2 * ROWS, cmap) + specs(ROWS, nmap) + [
            pl.BlockSpec((1, 16, D_A), lambda i, k: (layer, 0, 0)),
            pl.BlockSpec((1, PAIR, D_A), lambda i, k: (layer, 0, 0))],
        out_specs=[pl.BlockSpec((2 * ROWS, D_A), lambda i, k: (i * n_steps + k, 0)),
                   pl.BlockSpec((1, 1, N_PAIRS, PAIR, HEAD_A), lambda i, k: (0, i, 0, 0, 0)),
                   pl.BlockSpec((3, 1, 1, D_A), lambda i, k: (0, i, 0, 0)),
                   pl.BlockSpec((1, 1, PAIR), lambda i, k: (i, 0, 0))],
        out_shape=[jax.ShapeDtypeStruct((n_seq * seq_len, D_A), BF16),
                   jax.ShapeDtypeStruct((1, n_seq, N_PAIRS, PAIR, HEAD_A), F32),
                   jax.ShapeDtypeStruct((3, n_seq, 1, D_A), F32),
                   jax.ShapeDtypeStruct((n_seq, 1, PAIR), F32)],
        scratch_shapes=[pltpu.VMEM((N_PAIRS, PAIR, PAIR), F32),
                        pltpu.VMEM((N_PAIRS, len(_PP_SIDE), ROWS, PAIR), BF16),
                        pltpu.VMEM((N_PAIRS, len(_PP_STACK), 2 * ROWS, PAIR), BF16),
                        pltpu.VMEM((N_PAIRS, len(_PP_F32), ROWS, PAIR), F32),
                        pltpu.VMEM((N_PAIRS, 1, PAIR), F32)],
        compiler_params=pltpu.CompilerParams(
            dimension_semantics=("parallel", "arbitrary"), vmem_limit_bytes=VMEM_LIMIT),
        name="wkv_pipe",
    )(*([proj] * 10), prm, w2)


HALO = 16


def _window_sums(ext):
    w2 = ext + pltpu.roll(ext, 1, 0)
    w4 = w2 + pltpu.roll(w2, 2, 0)
    w8 = w4 + pltpu.roll(w4, 4, 0)
    w16 = w8 + pltpu.roll(w8, 8, 0)
    return (w2, w4, w8, w16)


def _pool_gate(pooled_groups, gb, pw_ref, pscale):
    mixed = [_dot(pg.astype(BF16), pw_ref[g]) for g, pg in enumerate(pooled_groups)]
    yb = jnp.concatenate(mixed, axis=1) * pscale
    return yb * _silu(gb)


def _layer_norm_v(vc, ln_g):
    vm = jnp.mean(vc, axis=-1, keepdims=True)
    d = vc - vm
    vv = jnp.mean(d * d, axis=-1, keepdims=True)
    return d * lax.rsqrt(vv + LN_EPS) * ln_g


def _chunk_gate(vn, uc, gc, wm_ref, bm_ref):
    n_rows = vn.shape[0]
    vnb = vn.astype(BF16)
    outs = []
    for j in range(n_rows // CHUNK):
        rs = slice(j * CHUNK, (j + 1) * CHUNK)
        mix = [_dot(wm_ref[g], vnb[rs, g * GC:(g + 1) * GC]) + bm_ref[g] for g in range(N_GROUPS_C)]
        outs.append(jnp.concatenate(mix, axis=1))
    mix = outs[0] if len(outs) == 1 else jnp.concatenate(outs, axis=0)
    return uc * mix * _silu(gc)


def _bc_prompt_kernel(ub_ref, gb_ref, uc_ref, vc_ref, gc_ref, pw_ref, ps_ref, lng_ref, wm_ref, bm_ref,
                      o_ref, halo_ref, *, tt):
    j = pl.program_id(1)

    @pl.when(j == 0)
    def _():
        halo_ref[...] = jnp.zeros_like(halo_ref)

    u = ub_ref[...].astype(F32)
    ext = jnp.concatenate([halo_ref[...], u], axis=0)
    halo_ref[...] = u[tt - HALO:, :]
    sums = _window_sums(ext)
    pos = j * tt + lax.broadcasted_iota(jnp.int32, (tt, 1), 0)
    pooled = []
    for g, win in enumerate(POOL_WINDOWS):
        ls = slice(g * POOL_GC, (g + 1) * POOL_GC)
        cnt = jnp.minimum(pos + 1, win).astype(F32)
        pooled.append(sums[g][HALO:, ls] / cnt - u[:, ls])
    gb = gb_ref[...].astype(F32)
    o_ref[:, :D_B] = _pool_gate(pooled, gb, pw_ref, ps_ref[...]).astype(o_ref.dtype)
    vn = _layer_norm_v(vc_ref[...].astype(F32), lng_ref[...])
    uc = uc_ref[...].astype(F32)
    gc = gc_ref[...].astype(F32)
    o_ref[:, D_B:] = _chunk_gate(vn, uc, gc, wm_ref, bm_ref).astype(o_ref.dtype)


def _bc_prompt(proj, n_seq, seq_len, pw, pscale, ln_g, wm, bm, tt=256):
    n_t = seq_len // tt

    def rmap(col):
        return lambda i, j: (i * n_t + j, col)

    wspec = pl.BlockSpec((4, 128, 128), lambda i, j: (0, 0, 0))
    vspec = pl.BlockSpec((1, D_B), lambda i, j: (0, 0))
    return pl.pallas_call(
        functools.partial(_bc_prompt_kernel, tt=tt),
        grid=(n_seq, n_t),
        in_specs=[pl.BlockSpec((tt, D_B), rmap(P_UB // D_B)),
                  pl.BlockSpec((tt, D_B), rmap(P_GB // D_B)),
                  pl.BlockSpec((tt, D_B), rmap(P_UC // D_B)),
                  pl.BlockSpec((tt, D_B), rmap(P_VC // D_B)),
                  pl.BlockSpec((tt, D_B), rmap(P_GC // D_B)),
                  wspec, vspec, vspec, wspec, wspec],
        out_specs=pl.BlockSpec((tt, D_B + D_C), rmap(0)),
        out_shape=jax.ShapeDtypeStruct((n_seq * seq_len, D_B + D_C), BF16),
        scratch_shapes=[pltpu.VMEM((HALO, D_B), F32)],
        compiler_params=pltpu.CompilerParams(
            dimension_semantics=("parallel", "arbitrary"), vmem_limit_bytes=VMEM_LIMIT),
        name="bc_prompt",
    )(proj, proj, proj, proj, proj, pw, pscale, ln_g, wm, bm)


def _bc_sample_kernel(buf_ref, ub_ref, gb_ref, uc_ref, vc_ref, gc_ref, pw_ref, ps_ref, lng_ref,
                      wm_ref, bm_ref, o_ref, vn_ref, *, nb, t_len):
    u3 = ub_ref[...].astype(F32)
    ext = jnp.concatenate([buf_ref[...], u3], axis=1)
    per = HALO + t_len
    sums = _window_sums(ext.reshape(nb * per, D_B))
    u = u3.reshape(nb * t_len, D_B)
    pooled = []
    for g, win in enumerate(POOL_WINDOWS):
        ls = slice(g * POOL_GC, (g + 1) * POOL_GC)
        s3 = sums[g].reshape(nb, per, D_B)[:, HALO:, ls].reshape(nb * t_len, POOL_GC)
        cnt = float(min(PAST_LEN + 1, win))
        pooled.append(s3 / cnt - u[:, ls])
    rows = nb * t_len
    gb = gb_ref[...].astype(F32).reshape(rows, D_B)
    o_ref[:, :D_B] = _pool_gate(pooled, gb, pw_ref, ps_ref[...]).astype(o_ref.dtype)
    vn = _layer_norm_v(vc_ref[...].astype(F32).reshape(rows, D_C), lng_ref[...])
    vn_ref[...] = vn
    uc = uc_ref[...].astype(F32).reshape(rows, D_C)
    gc = gc_ref[...].astype(F32).reshape(rows, D_C)
    o_ref[:, D_B:] = _chunk_gate(vn, uc, gc, wm_ref, bm_ref).astype(o_ref.dtype)


def _bc_sample(proj3, seq0, n_seq, t_len, buf16, pw, pscale, ln_g, wm, bm):
    nb = CHUNK // t_len
    sb0 = seq0 // nb

    def rmap(col):
        return lambda i: (sb0 + i, 0, col)

    wspec = pl.BlockSpec((4, 128, 128), lambda i: (0, 0, 0))
    vspec = pl.BlockSpec((1, D_B), lambda i: (0, 0))
    rows = nb * t_len
    return pl.pallas_call(
        functools.partial(_bc_sample_kernel, nb=nb, t_len=t_len),
        grid=(n_seq // nb,),
        in_specs=[pl.BlockSpec((nb, HALO, D_B), lambda i: (i, 0, 0)),
                  pl.BlockSpec((nb, t_len, D_B), rmap(P_UB // D_B)),
                  pl.BlockSpec((nb, t_len, D_B), rmap(P_GB // D_B)),
                  pl.BlockSpec((nb, t_len, D_B), rmap(P_UC // D_B)),
                  pl.BlockSpec((nb, t_len, D_B), rmap(P_VC // D_B)),
                  pl.BlockSpec((nb, t_len, D_B), rmap(P_GC // D_B)),
                  wspec, vspec, vspec, wspec, wspec],
        out_specs=[pl.BlockSpec((rows, D_B + D_C), lambda i: (i, 0)),
                   pl.BlockSpec((rows, D_C), lambda i: (i, 0))],
        out_shape=[jax.ShapeDtypeStruct((n_seq * t_len, D_B + D_C), BF16),
                   jax.ShapeDtypeStruct((n_seq * t_len, D_C), F32)],
        compiler_params=pltpu.CompilerParams(
            dimension_semantics=("parallel",), vmem_limit_bytes=VMEM_LIMIT),
        name="bc_sample",
    )(buf16, proj3, proj3, proj3, proj3, proj3, pw, pscale, ln_g, wm, bm)


WPREP_ROWS = 256


def _wprep_kernel(w_ref, o_ref):
    o_ref[0, :, :3 * D_A] = w_ref[0, :, :3 * D_A].astype(BF16)
    o_ref[0, :, 3 * D_A:P_LORA] = w_ref[0, :, SHIFT_W:].astype(BF16)
    o_ref[0, :, P_LORA:] = w_ref[0, :, 3 * D_A:SHIFT_W].astype(BF16)


def _prep_w_in(w_in):
    spec = pl.BlockSpec((1, WPREP_ROWS, D_INP), lambda l, i: (l, i, 0))
    return pl.pallas_call(
        _wprep_kernel,
        grid=(DEPTH, D_MODEL // WPREP_ROWS),
        in_specs=[spec],
        out_specs=spec,
        out_shape=jax.ShapeDtypeStruct((DEPTH, D_MODEL, D_INP), BF16),
        compiler_params=pltpu.CompilerParams(
            dimension_semantics=("parallel", "parallel"), vmem_limit_bytes=VMEM_LIMIT),
        name="wprep",
    )(w_in)


def kernel(x_prompt, x_sample, state_shift, state_wkv, state_pool, norm_g, final_norm_g, w_in,
           shift_mu, w0, w_up, a0, a_up, k_k, k_a, r_k, lnx_g, lnx_b, pool_w, pool_scale,
           gmlp_ln_g, gmlp_ws, gmlp_b, w_out):
    bp, seq, _ = x_prompt.shape
    bs, dseq, _ = x_sample.shape
    n_p = bp * seq
    n_s = bs * dseq
    xp = x_prompt.reshape(n_p, D_MODEL)
    xs = x_sample.reshape(n_s, D_MODEL)
    hp = _norm_rows(xp, norm_g[0][None])
    hs = _norm_rows(xs, norm_g[0][None])

    w_in_p = _prep_w_in(w_in)
    w_out_h = w_out.astype(BF16)
    mu_l = jnp.pad(shift_mu[:, 3 * D_A:], ((0, 0), (0, D_A - 2 * LORA)))
    prm = jnp.stack([shift_mu[:, :D_A], shift_mu[:, D_A:2 * D_A], shift_mu[:, 2 * D_A:3 * D_A],
                     w0, a0, k_k, k_a, r_k.reshape(DEPTH, D_A), lnx_g, lnx_b, mu_l], axis=1)
    prm = jnp.pad(prm, ((0, 0), (0, 16 - prm.shape[1]), (0, 0)))
    w2 = jnp.concatenate([w_up, a_up], axis=1)
    pw = pool_w.astype(BF16)
    tril = jnp.tril(jnp.ones((CHUNK, CHUNK), F32))
    wm_p = (gmlp_ws * tril).astype(BF16)
    bm_p = jnp.broadcast_to(gmlp_b[:, :, :, None], (DEPTH, N_GROUPS_C, CHUNK, GC))
    nb_s = CHUNK // dseq
    eye_b = jnp.eye(nb_s, dtype=F32)
    ws_small = gmlp_ws[:, :, :dseq, :dseq] * tril[:dseq, :dseq]
    wm_s = jnp.einsum('ab,lgts->lgatbs', eye_b, ws_small).reshape(DEPTH, N_GROUPS_C, CHUNK, CHUNK)
    wm_s = wm_s.astype(BF16)
    bm_s = jnp.broadcast_to(jnp.tile(gmlp_b[:, :, :dseq], (1, 1, nb_s))[:, :, :, None],
                            (DEPTH, N_GROUPS_C, CHUNK, GC))

    ssh_main = state_shift[:, :, :3 * D_A].reshape(DEPTH, bs, 3, 1, D_A).transpose(0, 2, 1, 3, 4)
    ssh_lora = state_shift[:, :, 3 * D_A:].reshape(DEPTH, bs, 1, PAIR)
    wkv_s = state_wkv.reshape(DEPTH, bs, N_PAIRS, PAIR, HEAD_A)
    buf16 = jnp.pad(state_pool, ((0, 0), (0, 0), (HALO - POOL_BUF, 0), (0, 0)))

    p_shift, p_wkv, p_pool, s_shift, s_pool, s_v = [], [], [], [], [], []
    for l in range(DEPTH):
        final = l == DEPTH - 1
        g_out = final_norm_g[None] if final else norm_g[l + 1][None]
        bc_w = (pw[l], pool_scale[l][None], gmlp_ln_g[l][None])

        proj_p = _inproj_h(hp, w_in_p, l, 2048)
        ya_p, wk_p, shm_p, shl_p = _wkv_prompt(proj_p, bp, seq, prm, w2, l)
        cb_p = _bc_prompt(proj_p, bp, seq, *bc_w, wm_p[l], bm_p[l])
        xp = _outproj(ya_p, cb_p, w_out_h, l, xp, g_out, final)
        if not final:
            xp, hp = xp

        proj_s = _inproj_h(hs, w_in_p, l, 1024)
        ya_s, wkv_s, shm_s, shl_s = _wkv_sample(proj_s, bs, dseq, ROWS // dseq, ssh_main, ssh_lora,
                                                wkv_s, prm, w2, l)
        cb_s, vn_s = _bc_sample(proj_s.reshape(bs, dseq, D_INP), 0, bs, dseq, buf16[l], *bc_w,
                                wm_s[l], bm_s[l])
        xs = _outproj(ya_s, cb_s, w_out_h, l, xs, g_out, final)
        if not final:
            xs, hs = xs

        p_shift.append(jnp.concatenate([shm_p[0, :, 0], shm_p[1, :, 0], shm_p[2, :, 0], shl_p[:, 0]], axis=-1))
        s_shift.append(jnp.concatenate([shm_s[0, :, 0], shm_s[1, :, 0], shm_s[2, :, 0], shl_s[:, 0]], axis=-1))
        p_wkv.append(wk_p.reshape(bp, N_HEADS_A, HEAD_A, HEAD_A))
        p_pool.append(jnp.stack([proj_p[(b + 1) * seq - POOL_BUF:(b + 1) * seq, P_UB:P_UB + D_B]
                                 for b in range(bp)]).astype(F32))
        ub_s = proj_s[:, P_UB:P_UB + D_B].astype(F32).reshape(bs, dseq, D_B)
        s_pool.append(jnp.concatenate([state_pool[l], ub_s], axis=1)[:, -POOL_BUF:])
        s_v.append(vn_s.reshape(bs, dseq, D_C))

    y_prompt = xp.reshape(bp, seq, D_MODEL)
    y_sample = xs.reshape(bs, dseq, D_MODEL)
    s_wkv = wkv_s.reshape(DEPTH, bs, N_HEADS_A, HEAD_A, HEAD_A)
    return (y_prompt, y_sample, jnp.stack(p_shift), jnp.stack(p_wkv), jnp.stack(p_pool),
            jnp.stack(s_shift), s_wkv, jnp.stack(s_pool), jnp.stack(s_v))
```

```python
import functools

import jax
import jax.numpy as jnp
import numpy as np
from jax import lax
from jax.experimental import pallas as pl
from jax.experimental.pallas import tpu as pltpu

F32 = jnp.float32
BF16 = jnp.bfloat16

D_MODEL = 2048
DEPTH = 4
PAST_LEN = 16384
D_A = 1024
HEAD_A = 64
N_HEADS_A = 16
LORA = 64
D_B = 512
POOL_WINDOWS = (2, 4, 8, 16)
POOL_GC = 128
POOL_BUF = 15
D_C = 512
N_GROUPS_C = 4
GC = 128
CHUNK = 128
SHIFT_W = 3 * D_A + 2 * LORA
EPS = 1e-6
GN_EPS = HEAD_A * 1e-5
LN_EPS = 1e-5

P_R, P_K, P_V, P_GA = 0, 1024, 2048, 3072
P_UB, P_GB, P_UC, P_VC, P_GC = 4096, 4608, 5120, 5632, 6144
P_LORA = 6656
D_INP = 6784

PAIR = 128
N_PAIRS = D_A // PAIR
ROWS = 64

PROJ_DTYPE = BF16
VMEM_LIMIT = 52 * 1024 * 1024
HI = lax.Precision.HIGHEST


def _dot(a, b, prec=None):
    return jnp.dot(a, b, precision=prec, preferred_element_type=F32)


def _dot_nt(a, b, prec=None):
    return lax.dot_general(a, b, (((1,), (1,)), ((), ())), precision=prec,
                           preferred_element_type=F32)


def _dot_tn(a, b, prec=None):
    return lax.dot_general(a, b, (((0,), (0,)), ((), ())), precision=prec,
                           preferred_element_type=F32)


def _split(x, n):
    pieces = []
    rem = x
    for i in range(n):
        hi = rem.astype(BF16)
        pieces.append(hi)
        if i + 1 < n:
            rem = rem - hi.astype(F32)
    return pieces


_CONTRACT = {"nn": (1, 0), "nt": (1, 1), "tn": (0, 0)}


def _mm(a, b, mode="nn", pa=1, pb=1):
    ca, cb = _CONTRACT[mode]
    sa, sb = _split(a, pa), _split(b, pb)
    terms = [(i, j) for i in range(pa) for j in range(pb) if i + j < max(pa, pb)]
    lhs = jnp.concatenate([sa[i] for i, _ in terms], axis=ca) if len(terms) > 1 else sa[0]
    rhs = jnp.concatenate([sb[j] for _, j in terms], axis=cb) if len(terms) > 1 else sb[0]
    return lax.dot_general(lhs, rhs, (((ca,), (cb,)), ((), ())), preferred_element_type=F32)


PREC_LORA = (1, 1)
PREC_CUM = (1, 2)
PREC_SEG = (1, 1)
PREC_G = (1, 1)
PREC_INV = (1, 1)
PREC_X2 = (1, 1)
PREC_WU = (1, 1)
PREC_US = (1, 1)
PREC_UPD = (1, 1)
PREC_Y = (1, 1)


def _norm_kernel(x_ref, g_ref, o_ref):
    x = x_ref[...]
    ms = jnp.mean(x * x, axis=-1, keepdims=True)
    o_ref[...] = ((x * lax.rsqrt(ms + EPS)) * g_ref[...]).astype(BF16)


def _norm_rows(x, g, tm=512):
    m = x.shape[0]
    return pl.pallas_call(
        _norm_kernel,
        grid=(m // tm,),
        in_specs=[pl.BlockSpec((tm, D_MODEL), lambda i: (i, 0)),
                  pl.BlockSpec((1, D_MODEL), lambda i: (0, 0))],
        out_specs=pl.BlockSpec((tm, D_MODEL), lambda i: (i, 0)),
        out_shape=jax.ShapeDtypeStruct((m, D_MODEL), BF16),
        compiler_params=pltpu.CompilerParams(
            dimension_semantics=("parallel",), vmem_limit_bytes=VMEM_LIMIT),
        name="norm_rows",
    )(x, g)


def _inproj_h_kernel(h_ref, w_ref, o_ref):
    o_ref[...] = _dot(h_ref[...], w_ref[0]).astype(o_ref.dtype)


def _inproj_h(h, w, layer, tm, tn=1152):
    m = h.shape[0]
    return pl.pallas_call(
        _inproj_h_kernel,
        grid=(m // tm, pl.cdiv(D_INP, tn)),
        in_specs=[pl.BlockSpec((tm, D_MODEL), lambda i, j: (i, 0)),
                  pl.BlockSpec((1, D_MODEL, tn), lambda i, j: (layer, 0, j))],
        out_specs=pl.BlockSpec((tm, tn), lambda i, j: (i, j)),
        out_shape=jax.ShapeDtypeStruct((m, D_INP), PROJ_DTYPE),
        compiler_params=pltpu.CompilerParams(
            dimension_semantics=("parallel", "arbitrary"), vmem_limit_bytes=VMEM_LIMIT),
        name="inproj_h",
    )(h, w)


def _outproj_kernel(ca_ref, cb_ref, wa_ref, wb_ref, x_ref, g_ref, *o_refs, final):
    y = _dot(ca_ref[...], wa_ref[0]) + _dot(cb_ref[...], wb_ref[0])
    out = x_ref[...] + y
    ms = jnp.mean(out * out, axis=-1, keepdims=True)
    normed = (out * lax.rsqrt(ms + EPS)) * g_ref[...]
    if final:
        o_refs[0][...] = normed
    else:
        o_refs[0][...] = out
        o_refs[1][...] = normed.astype(BF16)


def _outproj(cat_a, cat_b, w, layer, x, g, final, tm=512):
    m = x.shape[0]
    half = D_MODEL // 2
    row_spec = pl.BlockSpec((tm, D_MODEL), lambda i: (i, 0))
    f32_out = jax.ShapeDtypeStruct((m, D_MODEL), F32)
    return pl.pallas_call(
        functools.partial(_outproj_kernel, final=final),
        grid=(m // tm,),
        in_specs=[pl.BlockSpec((tm, half), lambda i: (i, 0)),
                  pl.BlockSpec((tm, half), lambda i: (i, 0)),
                  pl.BlockSpec((1, half, D_MODEL), lambda i: (layer, 0, 0)),
                  pl.BlockSpec((1, half, D_MODEL), lambda i: (layer, 1, 0)),
                  pl.BlockSpec((tm, D_MODEL), lambda i: (i, 0)),
                  pl.BlockSpec((1, D_MODEL), lambda i: (0, 0))],
        out_specs=row_spec if final else [row_spec, row_spec],
        out_shape=f32_out if final else [f32_out, jax.ShapeDtypeStruct((m, D_MODEL), BF16)],
        compiler_params=pltpu.CompilerParams(
            dimension_semantics=("parallel",), vmem_limit_bytes=VMEM_LIMIT),
        name="outproj",
    )(cat_a, cat_b, w, w, x, g)


PR_MU_R, PR_MU_K, PR_MU_V, PR_W0, PR_A0, PR_KK, PR_KA, PR_RK, PR_LNG, PR_LNB, PR_MU_L = range(11)


EXP_M_HALF = float(np.exp(-0.5))


NEG_LOG2E = -float(np.log2(np.e))


def _exp_neg(z):
    return jnp.exp2(z * NEG_LOG2E)


def _sigmoid(z):
    return 1.0 / (1.0 + _exp_neg(z))


def _silu(z):
    return z * _sigmoid(z)


def _run(gen):
    try:
        while True:
            next(gen)
    except StopIteration as e:
        return e.value


def _zip_run(main, prep, pattern):
    done = {}

    def step(gen, key):
        if key not in done:
            try:
                next(gen)
            except StopIteration as e:
                done[key] = e.value

    for ch in pattern:
        step(main if ch == "m" else prep, ch)
    while "m" not in done:
        step(main, "m")
    while "p" not in done:
        step(prep, "p")
    return done["m"], done["p"]


class _ScanConsts:
    def __init__(self, C, NB):
        rows = ROWS
        self.C, self.NB = C, NB
        row_id = lax.broadcasted_iota(jnp.int32, (rows, 1), 0)
        self.first = (row_id % C) == 0
        self.lo_half = lax.broadcasted_iota(jnp.int32, (rows, PAIR), 1) < HEAD_A
        ri = lax.broadcasted_iota(jnp.int32, (rows, rows), 0)
        ci = lax.broadcasted_iota(jnp.int32, (rows, rows), 1)
        self.tri = jnp.where(((ri // C) == (ci // C)) & (ri >= ci), 1.0, 0.0)
        pr = lax.broadcasted_iota(jnp.int32, (PAIR, PAIR), 0)
        pc = lax.broadcasted_iota(jnp.int32, (PAIR, PAIR), 1)
        self.same_head = (pr // HEAD_A) == (pc // HEAD_A)
        self.seg = jnp.where(self.same_head, 1.0, 0.0)
        rp = lax.broadcasted_iota(jnp.int32, (rows, PAIR), 0)
        cp = lax.broadcasted_iota(jnp.int32, (rows, PAIR), 1) % rows
        same_seq = (rp // C) == (cp // C)
        self.m_strict = same_seq & (rp > cp)
        m_incl = same_seq & (rp >= cp)
        self.m_incl2 = jnp.concatenate([m_incl, m_incl], axis=1)
        self.eye = jnp.where(rp == cp, 1.0, 0.0)


def _bmask(m, xb):
    return jnp.where(m, xb, jnp.zeros_like(xb))


_PP_SIDE = ("at", "rt", "kg", "bg", "vb")
_PP_STACK = ("ats", "khs", "bhs", "vs")
_PP_F32 = ("bon", "v", "sg")


def _stack(xb, lo_half):
    zb = jnp.zeros_like(xb)
    return jnp.concatenate([jnp.where(lo_half, xb, zb), jnp.where(lo_half, zb, xb)], axis=0)


def _prep_steps(raw, prev, prm, w2, cs):
    C, NB, rows = cs.C, cs.NB, ROWS
    r_raw, k_raw, v_raw, g_raw, l_raw = raw
    prev_r, prev_k, prev_v, prev_l = prev

    def prow(i):
        return prm[i:i + 1, :]

    def shift_mix(x, prev_rows, mu):
        rolled = pltpu.roll(x, 1, 0)
        parts = []
        for b, q in enumerate(prev_rows):
            head = jnp.where(cs.first[:8], jnp.broadcast_to(q, (8, x.shape[1])), rolled[b * C:b * C + 8])
            parts += [head] if C == 8 else [head, rolled[b * C + 8:(b + 1) * C]]
        shifted = jnp.concatenate(parts, axis=0)
        return x + (shifted - x) * mu

    xl = shift_mix(l_raw, prev_l, prow(PR_MU_L)[:, :PAIR])
    lora_w = _mm(jnp.where(cs.lo_half, jnp.tanh(xl), 0.0), w2, "nn", *PREC_LORA)
    lora_a = _mm(jnp.where(cs.lo_half, 0.0, xl), w2, "nn", *PREC_LORA)
    yield
    r = shift_mix(r_raw, prev_r, prow(PR_MU_R))
    k = shift_mix(k_raw, prev_k, prow(PR_MU_K))
    v = shift_mix(v_raw, prev_v, prow(PR_MU_V))
    dec = EXP_M_HALF * _sigmoid(prow(PR_W0) + lora_w)
    a = _sigmoid(prow(PR_A0) + lora_a)
    cum = _mm(cs.tri, -dec, "nn", *PREC_CUM)
    yield
    kk_raw = k * prow(PR_KK)
    k2 = k * (1.0 + (a - 1.0) * prow(PR_KA))
    rkr = r * k2 * prow(PR_RK)
    sls = [slice(p * PAIR, (p + 1) * PAIR) for p in range(N_PAIRS)]
    s0 = [_mm(jnp.concatenate([kk_raw[:, sl] * kk_raw[:, sl], rkr[:, sl]], axis=0), cs.seg, "nn", *PREC_SEG)
          for sl in sls]
    yield
    g_incl = jnp.exp(cum)
    g_excl = jnp.exp(cum + dec)
    g_inv = _exp_neg(cum)
    g_tot_rows = [jnp.exp(cum[(b + 1) * C - 1:(b + 1) * C, :]) for b in range(NB)]
    g_rest = g_inv * jnp.concatenate([jnp.broadcast_to(q, (C, D_A)) for q in g_tot_rows], axis=0)
    sg = _silu(g_raw)

    pp = {n: [] for n in _PP_SIDE + _PP_STACK + _PP_F32 + ("gtot",)}
    for sl, q in zip(sls, s0):
        kkp = kk_raw[:, sl] * lax.rsqrt(jnp.maximum(q[:rows], 1e-12))
        kka = kkp * a[:, sl]
        at = (-kkp * g_excl[:, sl]).astype(BF16)
        vb = v[:, sl].astype(BF16)
        pp["at"].append(at)
        pp["rt"].append((r[:, sl] * g_incl[:, sl]).astype(BF16))
        pp["kg"].append((k2[:, sl] * g_rest[:, sl]).astype(BF16))
        pp["bg"].append((kka * g_rest[:, sl]).astype(BF16))
        pp["vb"].append(vb)
        pp["ats"].append(_stack(at, cs.lo_half))
        pp["khs"].append(_stack((k2[:, sl] * g_inv[:, sl]).astype(BF16), cs.lo_half))
        pp["bhs"].append(_stack((kka * g_inv[:, sl]).astype(BF16), cs.lo_half))
        pp["vs"].append(_stack(vb, cs.lo_half))
        pp["bon"].append(q[rows:])
        pp["v"].append(v[:, sl])
        pp["sg"].append(sg[:, sl])
        pp["gtot"].append(jnp.concatenate([q[:, sl] for q in g_tot_rows], axis=0))
    return pp


def _scan_steps(pp, st_ref, o_ref, row0, prm, cs):
    C, NB, rows = cs.C, cs.NB, ROWS
    n_rounds = int(np.log2(C))
    pairs = range(N_PAIRS)

    def prow(i):
        return prm[i:i + 1, :]

    def seq_rows(x, b):
        if NB == 1:
            return x
        return x.astype(F32)[b * C:(b + 1) * C]

    def unseq_rows(pieces):
        return pieces[0] if NB == 1 else jnp.concatenate(pieces, axis=0)

    a_ak, a_r, pwb, t_inv = [], [], [], []
    for p in pairs:
        gm = _mm(jnp.concatenate([pp["at"][p], pp["rt"][p]], axis=0),
                 jnp.concatenate([pp["khs"][p], pp["bhs"][p]], axis=0), "nt", *PREC_G)
        gmb = gm.astype(BF16)
        a_ak.append(_bmask(cs.m_strict, gmb[:rows, :PAIR]))
        a_r.append(_bmask(cs.m_incl2, gmb[rows:, :]))
        pwb.append(_bmask(cs.m_strict, gmb[:rows, PAIR:]))
        t_inv.append(cs.eye + jnp.where(cs.m_strict, gm[:rows, PAIR:], 0.0))
    yield

    x2 = []
    for p in pairs:
        x2.append(_mm(a_ak[p], pp["vs"][p], "nn", *PREC_X2).astype(BF16))
        pwb[p] = _mm(pwb[p], _stack(pwb[p], cs.lo_half), "nn", *PREC_INV).astype(BF16)
    yield
    for rd in range(1, n_rounds):
        for p in pairs:
            tb = t_inv[p].astype(BF16)
            pws = _stack(pwb[p], cs.lo_half)
            if rd < n_rounds - 1:
                res = _mm(jnp.concatenate([pwb[p], tb], axis=0), pws, "nn", *PREC_INV)
                t_inv[p] = t_inv[p] + res[rows:]
                pwb[p] = res[:rows].astype(BF16)
            else:
                t_inv[p] = t_inv[p] + _mm(tb, pws, "nn", *PREC_INV)
        yield

    w_t, u_t = [], []
    for p in pairs:
        wu = _mm(t_inv[p], jnp.concatenate([pp["ats"][p], _stack(x2[p], cs.lo_half)], axis=1),
                 "nn", *PREC_WU)
        w_t.append(wu[:, :PAIR].astype(BF16))
        u_t.append(wu[:, PAIR:])
    yield

    s_old = [[st_ref[p * NB + b] for b in range(NB)] for p in pairs]
    u_b, rs = [], []
    for p in pairs:
        us = [_mm(jnp.concatenate([seq_rows(w_t[p], b), seq_rows(pp["rt"][p], b)], axis=0),
                  s_old[p][b], "nt", *PREC_US) for b in range(NB)]
        u_b.append([us[b][:C] + seq_rows(u_t[p], b) for b in range(NB)])
        rs.append(unseq_rows([us[b][C:] for b in range(NB)]))
    yield
    for p in pairs:
        for b in range(NB):
            upd = _mm(jnp.concatenate([seq_rows(pp["vb"][p], b), u_b[p][b]], axis=0),
                      jnp.concatenate([seq_rows(pp["kg"][p], b), seq_rows(pp["bg"][p], b)], axis=0),
                      "tn", *PREC_UPD)
            st_ref[p * NB + b] = s_old[p][b] * pp["gtot"][p][b:b + 1, :] + jnp.where(cs.same_head, upd, 0.0)
    yield
    y = []
    for p in pairs:
        u_s = _stack(unseq_rows(u_b[p]).astype(BF16), cs.lo_half)
        y.append(rs[p] + _mm(a_r[p], jnp.concatenate([pp["vs"][p], u_s], axis=0), "nn", *PREC_Y))
    yield
    yc = []
    for p in pairs:
        yc.append(y[p] - _mm(y[p], cs.seg, "nn", *PREC_SEG) * (1.0 / HEAD_A))
    yield
    for p in pairs:
        sl = slice(p * PAIR, (p + 1) * PAIR)
        yv = _mm(yc[p] * yc[p], cs.seg, "nn", *PREC_SEG) * (1.0 / HEAD_A)
        yn = yc[p] * lax.rsqrt(yv + GN_EPS) * prow(PR_LNG)[:, sl] + prow(PR_LNB)[:, sl]
        o_ref[row0:row0 + rows, sl] = ((yn + pp["bon"][p] * pp["v"][p]) * pp["sg"][p]).astype(o_ref.dtype)


def _write_state(st_ref, sout_ref, cs, b0=0):
    for b in range(cs.NB):
        for p in range(N_PAIRS):
            sm = jnp.where(cs.same_head, st_ref[p * cs.NB + b], 0.0)
            sout_ref[0, b0 + b, p] = sm[:, :HEAD_A] + sm[:, HEAD_A:]


_PIPE_PATTERN = "pmmpmmmmpmmp"


def _wkv_kernel(r_ref, k_ref, v_ref, g_ref, l_ref, sh_ref, shl_ref, s0_ref, prm_ref, w2_ref,
                o_ref, sout_ref, shm_out_ref, shl_out_ref, st_ref, *, C, NB, NBLK):
    cs = _ScanConsts(C, NB)
    prm = prm_ref[0]
    w2 = w2_ref[0]
    refs = (r_ref, k_ref, v_ref, g_ref, l_ref)
    for q in range(NBLK * NB):
        for p in range(N_PAIRS):
            s = s0_ref[0, q, p]
            s2 = jnp.concatenate([s, s], axis=1)
            st_ref[q // NB, p * NB + q % NB] = jnp.where(cs.same_head, s2, 0.0)

    def prep(blk):
        raw = tuple(q[blk * ROWS:(blk + 1) * ROWS, :].astype(F32) for q in refs)
        seqs = range(blk * NB, (blk + 1) * NB)
        prev = ([sh_ref[0, 0, q] for q in seqs], [sh_ref[0, 1, q] for q in seqs],
                [sh_ref[0, 2, q] for q in seqs], [shl_ref[0, q] for q in seqs])
        return _prep_steps(raw, prev, prm, w2, cs)

    pp = _run(prep(0))
    for blk in range(NBLK):
        scan = _scan_steps(pp, st_ref.at[blk], o_ref, blk * ROWS, prm, cs)
        if blk + 1 < NBLK:
            _, pp = _zip_run(scan, prep(blk + 1), _PIPE_PATTERN)
        else:
            _run(scan)

    for q in range(NBLK * NB):
        last = (q + 1) * C - 1
        shm_out_ref[0, q] = r_ref[last:last + 1, :].astype(F32)
        shm_out_ref[1, q] = k_ref[last:last + 1, :].astype(F32)
        shm_out_ref[2, q] = v_ref[last:last + 1, :].astype(F32)
        shl_out_ref[q] = l_ref[last:last + 1, :].astype(F32)
    for blk in range(NBLK):
        _write_state(st_ref.at[blk], sout_ref, cs, blk * NB)


def _wkv_sample(proj, n_seq, C, NB, sh_main, sh_lora, s0, prm, w2, layer, nblk=2):
    assert NB * C == ROWS
    nsq = NB * nblk
    rows = ROWS * nblk

    def rmap(col):
        return lambda i: (i, col)

    state_spec = pl.BlockSpec((1, nsq, N_PAIRS, PAIR, HEAD_A), lambda i: (layer, i, 0, 0, 0))
    return pl.pallas_call(
        functools.partial(_wkv_kernel, C=C, NB=NB, NBLK=nblk),
        grid=(n_seq // nsq,),
        in_specs=[pl.BlockSpec((rows, D_A), rmap(P_R // D_A)),
                  pl.BlockSpec((rows, D_A), rmap(P_K // D_A)),
                  pl.BlockSpec((rows, D_A), rmap(P_V // D_A)),
                  pl.BlockSpec((rows, D_A), rmap(P_GA // D_A)),
                  pl.BlockSpec((rows, PAIR), rmap(P_LORA // PAIR)),
                  pl.BlockSpec((1, 3, nsq, 1, D_A), lambda i: (layer, 0, i, 0, 0)),
                  pl.BlockSpec((1, nsq, 1, PAIR), lambda i: (layer, i, 0, 0)),
                  state_spec,
                  pl.BlockSpec((1, 16, D_A), lambda i: (layer, 0, 0)),
                  pl.BlockSpec((1, PAIR, D_A), lambda i: (layer, 0, 0))],
        out_specs=[pl.BlockSpec((rows, D_A), lambda i: (i, 0)),
                   state_spec,
                   pl.BlockSpec((3, nsq, 1, D_A), lambda i: (0, i, 0, 0)),
                   pl.BlockSpec((nsq, 1, PAIR), lambda i: (i, 0, 0))],
        out_shape=[jax.ShapeDtypeStruct((n_seq * C, D_A), BF16),
                   jax.ShapeDtypeStruct(s0.shape, F32),
                   jax.ShapeDtypeStruct((3, n_seq, 1, D_A), F32),
                   jax.ShapeDtypeStruct((n_seq, 1, PAIR), F32)],
        scratch_shapes=[pltpu.VMEM((nblk, N_PAIRS * NB, PAIR, PAIR), F32)],
        input_output_aliases={7: 1},
        compiler_params=pltpu.CompilerParams(
            dimension_semantics=("parallel",), vmem_limit_bytes=VMEM_LIMIT),
        name="wkv_c%d" % C,
    )(proj, proj, proj, proj, proj, sh_main, sh_lora, s0, prm, w2)


def _wkv_pipe_kernel(rc_ref, kc_ref, vc_ref, gc_ref, lc_ref, rn_ref, kn_ref, vn_ref, gn_ref, ln_ref,
                     prm_ref, w2_ref, o_ref, sout_ref, shm_out_ref, shl_out_ref,
                     st_ref, pa_ref, pb_ref, pf_ref, pg_ref, *, n_steps):
    k = pl.program_id(1)
    rows = ROWS
    cs = _ScanConsts(rows, 1)
    prm = prm_ref[0]
    w2 = w2_ref[0]

    def raw_rows(refs, lo):
        return tuple(q[lo:lo + rows, :].astype(F32) for q in refs)

    def prev_rows(refs, row):
        rr, kr, vr, _, lr = refs
        return tuple([q[row:row + 1, :].astype(F32)] for q in (rr, kr, vr, lr))

    def store_pp(pp):
        for p in range(N_PAIRS):
            for j, n in enumerate(_PP_SIDE):
                pa_ref[p, j] = pp[n][p]
            for j, n in enumerate(_PP_STACK):
                pb_ref[p, j] = pp[n][p]
            for j, n in enumerate(_PP_F32):
                pf_ref[p, j] = pp[n][p]
            pg_ref[p] = pp["gtot"][p]

    def load_pp():
        pp = {n: [pa_ref[p, j] for p in range(N_PAIRS)] for j, n in enumerate(_PP_SIDE)}
        pp.update({n: [pb_ref[p, j] for p in range(N_PAIRS)] for j, n in enumerate(_PP_STACK)})
        pp.update({n: [pf_ref[p, j] for p in range(N_PAIRS)] for j, n in enumerate(_PP_F32)})
        pp["gtot"] = [pg_ref[p] for p in range(N_PAIRS)]
        return pp

    cur = (rc_ref, kc_ref, vc_ref, gc_ref, lc_ref)
    nxt = (rn_ref, kn_ref, vn_ref, gn_ref, ln_ref)

    @pl.when(k == 0)
    def _():
        st_ref[...] = jnp.zeros_like(st_ref)
        zero = ([jnp.zeros((1, D_A), F32)], [jnp.zeros((1, D_A), F32)], [jnp.zeros((1, D_A), F32)],
                [jnp.zeros((1, PAIR), F32)])
        store_pp(_run(_prep_steps(raw_rows(cur, 0), zero, prm, w2, cs)))

    pp_a = load_pp()
    _, pp_b = _zip_run(_scan_steps(pp_a, st_ref, o_ref, 0, prm, cs),
                       _prep_steps(raw_rows(cur, rows), prev_rows(cur, rows - 1), prm, w2, cs),
                       _PIPE_PATTERN)
    _, pp_n = _zip_run(_scan_steps(pp_b, st_ref, o_ref, rows, prm, cs),
                       _prep_steps(raw_rows(nxt, 0), prev_rows(cur, 2 * rows - 1), prm, w2, cs),
                       _PIPE_PATTERN)
    store_pp(pp_n)

    last = 2 * rows - 1
    shm_out_ref[0, 0] = rc_ref[last:last + 1, :].astype(F32)
    shm_out_ref[1, 0] = kc_ref[last:last + 1, :].astype(F32)
    shm_out_ref[2, 0] = vc_ref[last:last + 1, :].astype(F32)
    shl_out_ref[0] = lc_ref[last:last + 1, :].astype(F32)

    @pl.when(k == n_steps - 1)
    def _():
        _write_state(st_ref, sout_ref, cs)


def _wkv_prompt(proj, n_seq, seq_len, prm, w2, layer):
    n_steps = seq_len // (2 * ROWS)
    n_chunks = seq_len // ROWS

    def cmap(col):
        return lambda i, k: (i * n_steps + k, col)

    def nmap(col):
        return lambda i, k: (i * n_chunks + jnp.minimum(2 * k + 2, n_chunks - 1), col)

    def specs(rows_, m):
        return [pl.BlockSpec((rows_, D_A), m(P_R // D_A)),
                pl.BlockSpec((rows_, D_A), m(P_K // D_A)),
                pl.BlockSpec((rows_, D_A), m(P_V // D_A)),
                pl.BlockSpec((rows_, D_A), m(P_GA // D_A)),
                pl.BlockSpec((rows_, PAIR), m(P_LORA // PAIR))]

    return pl.pallas_call(
        functools.partial(_wkv_pipe_kernel, n_steps=n_steps),
        grid=(n_seq, n_steps),
        in_specs=specs(2 * ROWS, cmap) + specs(ROWS, nmap) + [
            pl.BlockSpec((1, 16, D_A), lambda i, k: (layer, 0, 0)),
            pl.BlockSpec((1, PAIR, D_A), lambda i, k: (layer, 0, 0))],
        out_specs=[pl.BlockSpec((2 * ROWS, D_A), lambda i, k: (i * n_steps + k, 0)),
                   pl.BlockSpec((1, 1, N_PAIRS, PAIR, HEAD_A), lambda i, k: (0, i, 0, 0, 0)),
                   pl.BlockSpec((3, 1, 1, D_A), lambda i, k: (0, i, 0, 0)),
                   pl.BlockSpec((1, 1, PAIR), lambda i, k: (i, 0, 0))],
        out_shape=[jax.ShapeDtypeStruct((n_seq * seq_len, D_A), BF16),
                   jax.ShapeDtypeStruct((1, n_seq, N_PAIRS, PAIR, HEAD_A), F32),
                   jax.ShapeDtypeStruct((3, n_seq, 1, D_A), F32),
                   jax.ShapeDtypeStruct((n_seq, 1, PAIR), F32)],
        scratch_shapes=[pltpu.VMEM((N_PAIRS, PAIR, PAIR), F32),
                        pltpu.VMEM((N_PAIRS, len(_PP_SIDE), ROWS, PAIR), BF16),
                        pltpu.VMEM((N_PAIRS, len(_PP_STACK), 2 * ROWS, PAIR), BF16),
                        pltpu.VMEM((N_PAIRS, len(_PP_F32), ROWS, PAIR), F32),
                        pltpu.VMEM((N_PAIRS, 1, PAIR), F32)],
        compiler_params=pltpu.CompilerParams(
            dimension_semantics=("parallel", "arbitrary"), vmem_limit_bytes=VMEM_LIMIT),
        name="wkv_pipe",
    )(*([proj] * 10), prm, w2)


HALO = 16


def _window_sums(ext):
    w2 = ext + pltpu.roll(ext, 1, 0)
    w4 = w2 + pltpu.roll(w2, 2, 0)
    w8 = w4 + pltpu.roll(w4, 4, 0)
    w16 = w8 + pltpu.roll(w8, 8, 0)
    return (w2, w4, w8, w16)


def _pool_gate(pooled_groups, gb, pw_ref, pscale):
    mixed = [_dot(pg.astype(BF16), pw_ref[g]) for g, pg in enumerate(pooled_groups)]
    yb = jnp.concatenate(mixed, axis=1) * pscale
    return yb * _silu(gb)


def _layer_norm_v(vc, ln_g):
    vm = jnp.mean(vc, axis=-1, keepdims=True)
    d = vc - vm
    vv = jnp.mean(d * d, axis=-1, keepdims=True)
    return d * lax.rsqrt(vv + LN_EPS) * ln_g


def _chunk_gate(vn, uc, gc, wm_ref, bm_ref):
    n_rows = vn.shape[0]
    vnb = vn.astype(BF16)
    outs = []
    for j in range(n_rows // CHUNK):
        rs = slice(j * CHUNK, (j + 1) * CHUNK)
        mix = [_dot(wm_ref[g], vnb[rs, g * GC:(g + 1) * GC]) + bm_ref[g] for g in range(N_GROUPS_C)]
        outs.append(jnp.concatenate(mix, axis=1))
    mix = outs[0] if len(outs) == 1 else jnp.concatenate(outs, axis=0)
    return uc * mix * _silu(gc)


def _bc_prompt_kernel(ub_ref, gb_ref, uc_ref, vc_ref, gc_ref, pw_ref, ps_ref, lng_ref, wm_ref, bm_ref,
                      o_ref, halo_ref, *, tt):
    j = pl.program_id(1)

    @pl.when(j == 0)
    def _():
        halo_ref[...] = jnp.zeros_like(halo_ref)

    u = ub_ref[...].astype(F32)
    ext = jnp.concatenate([halo_ref[...], u], axis=0)
    halo_ref[...] = u[tt - HALO:, :]
    sums = _window_sums(ext)
    pos = j * tt + lax.broadcasted_iota(jnp.int32, (tt, 1), 0)
    pooled = []
    for g, win in enumerate(POOL_WINDOWS):
        ls = slice(g * POOL_GC, (g + 1) * POOL_GC)
        cnt = jnp.minimum(pos + 1, win).astype(F32)
        pooled.append(sums[g][HALO:, ls] / cnt - u[:, ls])
    gb = gb_ref[...].astype(F32)
    o_ref[:, :D_B] = _pool_gate(pooled, gb, pw_ref, ps_ref[...]).astype(o_ref.dtype)
    vn = _layer_norm_v(vc_ref[...].astype(F32), lng_ref[...])
    uc = uc_ref[...].astype(F32)
    gc = gc_ref[...].astype(F32)
    o_ref[:, D_B:] = _chunk_gate(vn, uc, gc, wm_ref, bm_ref).astype(o_ref.dtype)


def _bc_prompt(proj, n_seq, seq_len, pw, pscale, ln_g, wm, bm, tt=512):
    n_t = seq_len // tt

    def rmap(col):
        return lambda i, j: (i * n_t + j, col)

    wspec = pl.BlockSpec((4, 128, 128), lambda i, j: (0, 0, 0))
    vspec = pl.BlockSpec((1, D_B), lambda i, j: (0, 0))
    return pl.pallas_call(
        functools.partial(_bc_prompt_kernel, tt=tt),
        grid=(n_seq, n_t),
        in_specs=[pl.BlockSpec((tt, D_B), rmap(P_UB // D_B)),
                  pl.BlockSpec((tt, D_B), rmap(P_GB // D_B)),
                  pl.BlockSpec((tt, D_B), rmap(P_UC // D_B)),
                  pl.BlockSpec((tt, D_B), rmap(P_VC // D_B)),
                  pl.BlockSpec((tt, D_B), rmap(P_GC // D_B)),
                  wspec, vspec, vspec, wspec, wspec],
        out_specs=pl.BlockSpec((tt, D_B + D_C), rmap(0)),
        out_shape=jax.ShapeDtypeStruct((n_seq * seq_len, D_B + D_C), BF16),
        scratch_shapes=[pltpu.VMEM((HALO, D_B), F32)],
        compiler_params=pltpu.CompilerParams(
            dimension_semantics=("parallel", "arbitrary"), vmem_limit_bytes=VMEM_LIMIT),
        name="bc_prompt",
    )(proj, proj, proj, proj, proj, pw, pscale, ln_g, wm, bm)


def _bc_sample_kernel(buf_ref, ub_ref, gb_ref, uc_ref, vc_ref, gc_ref, pw_ref, ps_ref, lng_ref,
                      wm_ref, bm_ref, o_ref, vn_ref, *, nb, t_len):
    u3 = ub_ref[...].astype(F32)
    ext = jnp.concatenate([buf_ref[...], u3], axis=1)
    per = HALO + t_len
    sums = _window_sums(ext.reshape(nb * per, D_B))
    u = u3.reshape(nb * t_len, D_B)
    pooled = []
    for g, win in enumerate(POOL_WINDOWS):
        ls = slice(g * POOL_GC, (g + 1) * POOL_GC)
        s3 = sums[g].reshape(nb, per, D_B)[:, HALO:, ls].reshape(nb * t_len, POOL_GC)
        cnt = float(min(PAST_LEN + 1, win))
        pooled.append(s3 / cnt - u[:, ls])
    rows = nb * t_len
    gb = gb_ref[...].astype(F32).reshape(rows, D_B)
    o_ref[:, :D_B] = _pool_gate(pooled, gb, pw_ref, ps_ref[...]).astype(o_ref.dtype)
    vn = _layer_norm_v(vc_ref[...].astype(F32).reshape(rows, D_C), lng_ref[...])
    vn_ref[...] = vn
    uc = uc_ref[...].astype(F32).reshape(rows, D_C)
    gc = gc_ref[...].astype(F32).reshape(rows, D_C)
    o_ref[:, D_B:] = _chunk_gate(vn, uc, gc, wm_ref, bm_ref).astype(o_ref.dtype)


def _bc_sample(proj3, seq0, n_seq, t_len, buf16, pw, pscale, ln_g, wm, bm):
    nb = CHUNK // t_len
    sb0 = seq0 // nb

    def rmap(col):
        return lambda i: (sb0 + i, 0, col)

    wspec = pl.BlockSpec((4, 128, 128), lambda i: (0, 0, 0))
    vspec = pl.BlockSpec((1, D_B), lambda i: (0, 0))
    rows = nb * t_len
    return pl.pallas_call(
        functools.partial(_bc_sample_kernel, nb=nb, t_len=t_len),
        grid=(n_seq // nb,),
        in_specs=[pl.BlockSpec((nb, HALO, D_B), lambda i: (i, 0, 0)),
                  pl.BlockSpec((nb, t_len, D_B), rmap(P_UB // D_B)),
                  pl.BlockSpec((nb, t_len, D_B), rmap(P_GB // D_B)),
                  pl.BlockSpec((nb, t_len, D_B), rmap(P_UC // D_B)),
                  pl.BlockSpec((nb, t_len, D_B), rmap(P_VC // D_B)),
                  pl.BlockSpec((nb, t_len, D_B), rmap(P_GC // D_B)),
                  wspec, vspec, vspec, wspec, wspec],
        out_specs=[pl.BlockSpec((rows, D_B + D_C), lambda i: (i, 0)),
                   pl.BlockSpec((rows, D_C), lambda i: (i, 0))],
        out_shape=[jax.ShapeDtypeStruct((n_seq * t_len, D_B + D_C), BF16),
                   jax.ShapeDtypeStruct((n_seq * t_len, D_C), F32)],
        compiler_params=pltpu.CompilerParams(
            dimension_semantics=("parallel",), vmem_limit_bytes=VMEM_LIMIT),
        name="bc_sample",
    )(buf16, proj3, proj3, proj3, proj3, proj3, pw, pscale, ln_g, wm, bm)


WPREP_ROWS = 256


def _wprep_kernel(w_ref, o_ref):
    o_ref[0, :, :3 * D_A] = w_ref[0, :, :3 * D_A].astype(BF16)
    o_ref[0, :, 3 * D_A:P_LORA] = w_ref[0, :, SHIFT_W:].astype(BF16)
    o_ref[0, :, P_LORA:] = w_ref[0, :, 3 * D_A:SHIFT_W].astype(BF16)


def _prep_w_in(w_in):
    spec = pl.BlockSpec((1, WPREP_ROWS, D_INP), lambda l, i: (l, i, 0))
    return pl.pallas_call(
        _wprep_kernel,
        grid=(DEPTH, D_MODEL // WPREP_ROWS),
        in_specs=[spec],
        out_specs=spec,
        out_shape=jax.ShapeDtypeStruct((DEPTH, D_MODEL, D_INP), BF16),
        compiler_params=pltpu.CompilerParams(
            dimension_semantics=("parallel", "parallel"), vmem_limit_bytes=VMEM_LIMIT),
        name="wprep",
    )(w_in)


def kernel(x_prompt, x_sample, state_shift, state_wkv, state_pool, norm_g, final_norm_g, w_in,
           shift_mu, w0, w_up, a0, a_up, k_k, k_a, r_k, lnx_g, lnx_b, pool_w, pool_scale,
           gmlp_ln_g, gmlp_ws, gmlp_b, w_out):
    bp, seq, _ = x_prompt.shape
    bs, dseq, _ = x_sample.shape
    n_p = bp * seq
    n_s = bs * dseq
    xp = x_prompt.reshape(n_p, D_MODEL)
    xs = x_sample.reshape(n_s, D_MODEL)
    hp = _norm_rows(xp, norm_g[0][None])
    hs = _norm_rows(xs, norm_g[0][None])

    w_in_p = _prep_w_in(w_in)
    w_out_h = w_out.astype(BF16)
    mu_l = jnp.pad(shift_mu[:, 3 * D_A:], ((0, 0), (0, D_A - 2 * LORA)))
    prm = jnp.stack([shift_mu[:, :D_A], shift_mu[:, D_A:2 * D_A], shift_mu[:, 2 * D_A:3 * D_A],
                     w0, a0, k_k, k_a, r_k.reshape(DEPTH, D_A), lnx_g, lnx_b, mu_l], axis=1)
    prm = jnp.pad(prm, ((0, 0), (0, 16 - prm.shape[1]), (0, 0)))
    w2 = jnp.concatenate([w_up, a_up], axis=1)
    pw = pool_w.astype(BF16)
    tril = jnp.tril(jnp.ones((CHUNK, CHUNK), F32))
    wm_p = (gmlp_ws * tril).astype(BF16)
    bm_p = jnp.broadcast_to(gmlp_b[:, :, :, None], (DEPTH, N_GROUPS_C, CHUNK, GC))
    nb_s = CHUNK // dseq
    eye_b = jnp.eye(nb_s, dtype=F32)
    ws_small = gmlp_ws[:, :, :dseq, :dseq] * tril[:dseq, :dseq]
    wm_s = jnp.einsum('ab,lgts->lgatbs', eye_b, ws_small).reshape(DEPTH, N_GROUPS_C, CHUNK, CHUNK)
    wm_s = wm_s.astype(BF16)
    bm_s = jnp.broadcast_to(jnp.tile(gmlp_b[:, :, :dseq], (1, 1, nb_s))[:, :, :, None],
                            (DEPTH, N_GROUPS_C, CHUNK, GC))

    ssh_main = state_shift[:, :, :3 * D_A].reshape(DEPTH, bs, 3, 1, D_A).transpose(0, 2, 1, 3, 4)
    ssh_lora = state_shift[:, :, 3 * D_A:].reshape(DEPTH, bs, 1, PAIR)
    wkv_s = state_wkv.reshape(DEPTH, bs, N_PAIRS, PAIR, HEAD_A)
    buf16 = jnp.pad(state_pool, ((0, 0), (0, 0), (HALO - POOL_BUF, 0), (0, 0)))

    p_shift, p_wkv, p_pool, s_shift, s_pool, s_v = [], [], [], [], [], []
    for l in range(DEPTH):
        final = l == DEPTH - 1
        g_out = final_norm_g[None] if final else norm_g[l + 1][None]
        bc_w = (pw[l], pool_scale[l][None], gmlp_ln_g[l][None])

        proj_p = _inproj_h(hp, w_in_p, l, 2048)
        ya_p, wk_p, shm_p, shl_p = _wkv_prompt(proj_p, bp, seq, prm, w2, l)
        cb_p = _bc_prompt(proj_p, bp, seq, *bc_w, wm_p[l], bm_p[l])
        xp = _outproj(ya_p, cb_p, w_out_h, l, xp, g_out, final)
        if not final:
            xp, hp = xp

        proj_s = _inproj_h(hs, w_in_p, l, 1024)
        ya_s, wkv_s, shm_s, shl_s = _wkv_sample(proj_s, bs, dseq, ROWS // dseq, ssh_main, ssh_lora,
                                                wkv_s, prm, w2, l)
        cb_s, vn_s = _bc_sample(proj_s.reshape(bs, dseq, D_INP), 0, bs, dseq, buf16[l], *bc_w,
                                wm_s[l], bm_s[l])
        xs = _outproj(ya_s, cb_s, w_out_h, l, xs, g_out, final)
        if not final:
            xs, hs = xs

        p_shift.append(jnp.concatenate([shm_p[0, :, 0], shm_p[1, :, 0], shm_p[2, :, 0], shl_p[:, 0]], axis=-1))
        s_shift.append(jnp.concatenate([shm_s[0, :, 0], shm_s[1, :, 0], shm_s[2, :, 0], shl_s[:, 0]], axis=-1))
        p_wkv.append(wk_p.reshape(bp, N_HEADS_A, HEAD_A, HEAD_A))
        p_pool.append(jnp.stack([proj_p[(b + 1) * seq - POOL_BUF:(b + 1) * seq, P_UB:P_UB + D_B]
                                 for b in range(bp)]).astype(F32))
        ub_s = proj_s[:, P_UB:P_UB + D_B].astype(F32).reshape(bs, dseq, D_B)
        s_pool.append(jnp.concatenate([state_pool[l], ub_s], axis=1)[:, -POOL_BUF:])
        s_v.append(vn_s.reshape(bs, dseq, D_C))

    y_prompt = xp.reshape(bp, seq, D_MODEL)
    y_sample = xs.reshape(bs, dseq, D_MODEL)
    s_wkv = wkv_s.reshape(DEPTH, bs, N_HEADS_A, HEAD_A, HEAD_A)
    return (y_prompt, y_sample, jnp.stack(p_shift), jnp.stack(p_wkv), jnp.stack(p_pool),
            jnp.stack(s_shift), s_wkv, jnp.stack(s_pool), jnp.stack(s_v))
```

```python
import functools

import jax
import jax.numpy as jnp
import numpy as np
from jax import lax
from jax.experimental import pallas as pl
from jax.experimental.pallas import tpu as pltpu

F32 = jnp.float32
BF16 = jnp.bfloat16

D_MODEL = 2048
DEPTH = 4
PAST_LEN = 16384
D_A = 1024
HEAD_A = 64
N_HEADS_A = 16
LORA = 64
D_B = 512
POOL_WINDOWS = (2, 4, 8, 16)
POOL_GC = 128
POOL_BUF = 15
D_C = 512
N_GROUPS_C = 4
GC = 128
CHUNK = 128
SHIFT_W = 3 * D_A + 2 * LORA
EPS = 1e-6
GN_EPS = HEAD_A * 1e-5
LN_EPS = 1e-5

P_R, P_K, P_V, P_GA = 0, 1024, 2048, 3072
P_UB, P_GB, P_UC, P_VC, P_GC = 4096, 4608, 5120, 5632, 6144
P_LORA = 6656
D_INP = 6784

PAIR = 128
N_PAIRS = D_A // PAIR
ROWS = 64

PROJ_DTYPE = BF16
VMEM_LIMIT = 52 * 1024 * 1024
HI = lax.Precision.HIGHEST


def _dot(a, b, prec=None):
    return jnp.dot(a, b, precision=prec, preferred_element_type=F32)


def _dot_nt(a, b, prec=None):
    return lax.dot_general(a, b, (((1,), (1,)), ((), ())), precision=prec,
                           preferred_element_type=F32)


def _dot_tn(a, b, prec=None):
    return lax.dot_general(a, b, (((0,), (0,)), ((), ())), precision=prec,
                           preferred_element_type=F32)


def _split(x, n):
    pieces = []
    rem = x
    for i in range(n):
        hi = rem.astype(BF16)
        pieces.append(hi)
        if i + 1 < n:
            rem = rem - hi.astype(F32)
    return pieces


_CONTRACT = {"nn": (1, 0), "nt": (1, 1), "tn": (0, 0)}


def _mm(a, b, mode="nn", pa=1, pb=1):
    ca, cb = _CONTRACT[mode]
    sa, sb = _split(a, pa), _split(b, pb)
    terms = [(i, j) for i in range(pa) for j in range(pb) if i + j < max(pa, pb)]
    lhs = jnp.concatenate([sa[i] for i, _ in terms], axis=ca) if len(terms) > 1 else sa[0]
    rhs = jnp.concatenate([sb[j] for _, j in terms], axis=cb) if len(terms) > 1 else sb[0]
    return lax.dot_general(lhs, rhs, (((ca,), (cb,)), ((), ())), preferred_element_type=F32)


PREC_LORA = (1, 1)
PREC_CUM = (1, 2)
PREC_SEG = (1, 1)
PREC_G = (1, 1)
PREC_INV = (1, 1)
PREC_X2 = (1, 1)
PREC_WU = (1, 1)
PREC_US = (1, 1)
PREC_UPD = (1, 1)
PREC_Y = (1, 1)


def _norm_kernel(x_ref, g_ref, o_ref):
    x = x_ref[...]
    ms = jnp.mean(x * x, axis=-1, keepdims=True)
    o_ref[...] = ((x * lax.rsqrt(ms + EPS)) * g_ref[...]).astype(BF16)


def _norm_rows(x, g, tm=512):
    m = x.shape[0]
    return pl.pallas_call(
        _norm_kernel,
        grid=(m // tm,),
        in_specs=[pl.BlockSpec((tm, D_MODEL), lambda i: (i, 0)),
                  pl.BlockSpec((1, D_MODEL), lambda i: (0, 0))],
        out_specs=pl.BlockSpec((tm, D_MODEL), lambda i: (i, 0)),
        out_shape=jax.ShapeDtypeStruct((m, D_MODEL), BF16),
        compiler_params=pltpu.CompilerParams(
            dimension_semantics=("parallel",), vmem_limit_bytes=VMEM_LIMIT),
        name="norm_rows",
    )(x, g)


def _inproj_h_kernel(h_ref, w_ref, o_ref):
    o_ref[...] = _dot(h_ref[...], w_ref[0]).astype(o_ref.dtype)


def _inproj_h(h, w, layer, tm, tn=768):
    m = h.shape[0]
    return pl.pallas_call(
        _inproj_h_kernel,
        grid=(m // tm, pl.cdiv(D_INP, tn)),
        in_specs=[pl.BlockSpec((tm, D_MODEL), lambda i, j: (i, 0)),
                  pl.BlockSpec((1, D_MODEL, tn), lambda i, j: (layer, 0, j))],
        out_specs=pl.BlockSpec((tm, tn), lambda i, j: (i, j)),
        out_shape=jax.ShapeDtypeStruct((m, D_INP), PROJ_DTYPE),
        compiler_params=pltpu.CompilerParams(
            dimension_semantics=("parallel", "arbitrary"), vmem_limit_bytes=VMEM_LIMIT),
        name="inproj_h",
    )(h, w)


def _outproj_kernel(ca_ref, cb_ref, wa_ref, wb_ref, x_ref, g_ref, *o_refs, final):
    y = _dot(ca_ref[...], wa_ref[0]) + _dot(cb_ref[...], wb_ref[0])
    out = x_ref[...] + y
    ms = jnp.mean(out * out, axis=-1, keepdims=True)
    normed = (out * lax.rsqrt(ms + EPS)) * g_ref[...]
    if final:
        o_refs[0][...] = normed
    else:
        o_refs[0][...] = out
        o_refs[1][...] = normed.astype(BF16)


def _outproj(cat_a, cat_b, w, layer, x, g, final, tm=512):
    m = x.shape[0]
    half = D_MODEL // 2
    row_spec = pl.BlockSpec((tm, D_MODEL), lambda i: (i, 0))
    f32_out = jax.ShapeDtypeStruct((m, D_MODEL), F32)
    return pl.pallas_call(
        functools.partial(_outproj_kernel, final=final),
        grid=(m // tm,),
        in_specs=[pl.BlockSpec((tm, half), lambda i: (i, 0)),
                  pl.BlockSpec((tm, half), lambda i: (i, 0)),
                  pl.BlockSpec((1, half, D_MODEL), lambda i: (layer, 0, 0)),
                  pl.BlockSpec((1, half, D_MODEL), lambda i: (layer, 1, 0)),
                  pl.BlockSpec((tm, D_MODEL), lambda i: (i, 0)),
                  pl.BlockSpec((1, D_MODEL), lambda i: (0, 0))],
        out_specs=row_spec if final else [row_spec, row_spec],
        out_shape=f32_out if final else [f32_out, jax.ShapeDtypeStruct((m, D_MODEL), BF16)],
        compiler_params=pltpu.CompilerParams(
            dimension_semantics=("parallel",), vmem_limit_bytes=VMEM_LIMIT),
        name="outproj",
    )(cat_a, cat_b, w, w, x, g)


PR_MU_R, PR_MU_K, PR_MU_V, PR_W0, PR_A0, PR_KK, PR_KA, PR_RK, PR_LNG, PR_LNB, PR_MU_L = range(11)


EXP_M_HALF = float(np.exp(-0.5))


NEG_LOG2E = -float(np.log2(np.e))


def _exp_neg(z):
    return jnp.exp2(z * NEG_LOG2E)


def _sigmoid(z):
    return 1.0 / (1.0 + _exp_neg(z))


def _silu(z):
    return z * _sigmoid(z)


def _run(gen):
    try:
        while True:
            next(gen)
    except StopIteration as e:
        return e.value


def _zip_run(main, prep, pattern):
    done = {}

    def step(gen, key):
        if key not in done:
            try:
                next(gen)
            except StopIteration as e:
                done[key] = e.value

    for ch in pattern:
        step(main if ch == "m" else prep, ch)
    while "m" not in done:
        step(main, "m")
    while "p" not in done:
        step(prep, "p")
    return done["m"], done["p"]


class _ScanConsts:
    def __init__(self, C, NB):
        rows = ROWS
        self.C, self.NB = C, NB
        row_id = lax.broadcasted_iota(jnp.int32, (rows, 1), 0)
        self.first = (row_id % C) == 0
        self.lo_half = lax.broadcasted_iota(jnp.int32, (rows, PAIR), 1) < HEAD_A
        ri = lax.broadcasted_iota(jnp.int32, (rows, rows), 0)
        ci = lax.broadcasted_iota(jnp.int32, (rows, rows), 1)
        self.tri = jnp.where(((ri // C) == (ci // C)) & (ri >= ci), 1.0, 0.0)
        pr = lax.broadcasted_iota(jnp.int32, (PAIR, PAIR), 0)
        pc = lax.broadcasted_iota(jnp.int32, (PAIR, PAIR), 1)
        self.same_head = (pr // HEAD_A) == (pc // HEAD_A)
        self.seg = jnp.where(self.same_head, 1.0, 0.0)
        rp = lax.broadcasted_iota(jnp.int32, (rows, PAIR), 0)
        cp = lax.broadcasted_iota(jnp.int32, (rows, PAIR), 1) % rows
        same_seq = (rp // C) == (cp // C)
        self.m_strict = same_seq & (rp > cp)
        m_incl = same_seq & (rp >= cp)
        self.m_incl2 = jnp.concatenate([m_incl, m_incl], axis=1)
        self.eye = jnp.where(rp == cp, 1.0, 0.0)


def _bmask(m, xb):
    return jnp.where(m, xb, jnp.zeros_like(xb))


_PP_SIDE = ("at", "rt", "kg", "bg", "vb")
_PP_STACK = ("ats", "khs", "bhs", "vs")
_PP_F32 = ("bon", "v", "sg")


def _stack(xb, lo_half):
    zb = jnp.zeros_like(xb)
    return jnp.concatenate([jnp.where(lo_half, xb, zb), jnp.where(lo_half, zb, xb)], axis=0)


def _prep_steps(raw, prev, prm, w2, cs):
    C, NB, rows = cs.C, cs.NB, ROWS
    r_raw, k_raw, v_raw, g_raw, l_raw = raw
    prev_r, prev_k, prev_v, prev_l = prev

    def prow(i):
        return prm[i:i + 1, :]

    def shift_mix(x, prev_rows, mu):
        rolled = pltpu.roll(x, 1, 0)
        parts = []
        for b, q in enumerate(prev_rows):
            head = jnp.where(cs.first[:8], jnp.broadcast_to(q, (8, x.shape[1])), rolled[b * C:b * C + 8])
            parts += [head] if C == 8 else [head, rolled[b * C + 8:(b + 1) * C]]
        shifted = jnp.concatenate(parts, axis=0)
        return x + (shifted - x) * mu

    xl = shift_mix(l_raw, prev_l, prow(PR_MU_L)[:, :PAIR])
    lora_w = _mm(jnp.where(cs.lo_half, jnp.tanh(xl), 0.0), w2, "nn", *PREC_LORA)
    lora_a = _mm(jnp.where(cs.lo_half, 0.0, xl), w2, "nn", *PREC_LORA)
    yield
    r = shift_mix(r_raw, prev_r, prow(PR_MU_R))
    k = shift_mix(k_raw, prev_k, prow(PR_MU_K))
    v = shift_mix(v_raw, prev_v, prow(PR_MU_V))
    dec = EXP_M_HALF * _sigmoid(prow(PR_W0) + lora_w)
    a = _sigmoid(prow(PR_A0) + lora_a)
    cum = _mm(cs.tri, -dec, "nn", *PREC_CUM)
    yield
    kk_raw = k * prow(PR_KK)
    k2 = k * (1.0 + (a - 1.0) * prow(PR_KA))
    rkr = r * k2 * prow(PR_RK)
    sls = [slice(p * PAIR, (p + 1) * PAIR) for p in range(N_PAIRS)]
    s0 = [_mm(jnp.concatenate([kk_raw[:, sl] * kk_raw[:, sl], rkr[:, sl]], axis=0), cs.seg, "nn", *PREC_SEG)
          for sl in sls]
    yield
    g_incl = jnp.exp(cum)
    g_excl = jnp.exp(cum + dec)
    g_inv = _exp_neg(cum)
    g_tot_rows = [jnp.exp(cum[(b + 1) * C - 1:(b + 1) * C, :]) for b in range(NB)]
    g_rest = g_inv * jnp.concatenate([jnp.broadcast_to(q, (C, D_A)) for q in g_tot_rows], axis=0)
    sg = _silu(g_raw)

    pp = {n: [] for n in _PP_SIDE + _PP_STACK + _PP_F32 + ("gtot",)}
    for sl, q in zip(sls, s0):
        kkp = kk_raw[:, sl] * lax.rsqrt(jnp.maximum(q[:rows], 1e-12))
        kka = kkp * a[:, sl]
        at = (-kkp * g_excl[:, sl]).astype(BF16)
        vb = v[:, sl].astype(BF16)
        pp["at"].append(at)
        pp["rt"].append((r[:, sl] * g_incl[:, sl]).astype(BF16))
        pp["kg"].append((k2[:, sl] * g_rest[:, sl]).astype(BF16))
        pp["bg"].append((kka * g_rest[:, sl]).astype(BF16))
        pp["vb"].append(vb)
        pp["ats"].append(_stack(at, cs.lo_half))
        pp["khs"].append(_stack((k2[:, sl] * g_inv[:, sl]).astype(BF16), cs.lo_half))
        pp["bhs"].append(_stack((kka * g_inv[:, sl]).astype(BF16), cs.lo_half))
        pp["vs"].append(_stack(vb, cs.lo_half))
        pp["bon"].append(q[rows:])
        pp["v"].append(v[:, sl])
        pp["sg"].append(sg[:, sl])
        pp["gtot"].append(jnp.concatenate([q[:, sl] for q in g_tot_rows], axis=0))
    return pp


def _scan_steps(pp, st_ref, o_ref, row0, prm, cs):
    C, NB, rows = cs.C, cs.NB, ROWS
    n_rounds = int(np.log2(C))
    pairs = range(N_PAIRS)

    def prow(i):
        return prm[i:i + 1, :]

    def seq_rows(x, b):
        if NB == 1:
            return x
        return x.astype(F32)[b * C:(b + 1) * C]

    def unseq_rows(pieces):
        return pieces[0] if NB == 1 else jnp.concatenate(pieces, axis=0)

    a_ak, a_r, pwb, t_inv = [], [], [], []
    for p in pairs:
        gm = _mm(jnp.concatenate([pp["at"][p], pp["rt"][p]], axis=0),
                 jnp.concatenate([pp["khs"][p], pp["bhs"][p]], axis=0), "nt", *PREC_G)
        gmb = gm.astype(BF16)
        a_ak.append(_bmask(cs.m_strict, gmb[:rows, :PAIR]))
        a_r.append(_bmask(cs.m_incl2, gmb[rows:, :]))
        pwb.append(_bmask(cs.m_strict, gmb[:rows, PAIR:]))
        t_inv.append(cs.eye + jnp.where(cs.m_strict, gm[:rows, PAIR:], 0.0))
    yield

    x2 = []
    for p in pairs:
        x2.append(_mm(a_ak[p], pp["vs"][p], "nn", *PREC_X2).astype(BF16))
        pwb[p] = _mm(pwb[p], _stack(pwb[p], cs.lo_half), "nn", *PREC_INV).astype(BF16)
    yield
    for rd in range(1, n_rounds):
        for p in pairs:
            tb = t_inv[p].astype(BF16)
            pws = _stack(pwb[p], cs.lo_half)
            if rd < n_rounds - 1:
                res = _mm(jnp.concatenate([pwb[p], tb], axis=0), pws, "nn", *PREC_INV)
                t_inv[p] = t_inv[p] + res[rows:]
                pwb[p] = res[:rows].astype(BF16)
            else:
                t_inv[p] = t_inv[p] + _mm(tb, pws, "nn", *PREC_INV)
        yield

    w_t, u_t = [], []
    for p in pairs:
        wu = _mm(t_inv[p], jnp.concatenate([pp["ats"][p], _stack(x2[p], cs.lo_half)], axis=1),
                 "nn", *PREC_WU)
        w_t.append(wu[:, :PAIR].astype(BF16))
        u_t.append(wu[:, PAIR:])
    yield

    s_old = [[st_ref[p * NB + b] for b in range(NB)] for p in pairs]
    u_b, rs = [], []
    for p in pairs:
        us = [_mm(jnp.concatenate([seq_rows(w_t[p], b), seq_rows(pp["rt"][p], b)], axis=0),
                  s_old[p][b], "nt", *PREC_US) for b in range(NB)]
        u_b.append([us[b][:C] + seq_rows(u_t[p], b) for b in range(NB)])
        rs.append(unseq_rows([us[b][C:] for b in range(NB)]))
    yield
    for p in pairs:
        for b in range(NB):
            upd = _mm(jnp.concatenate([seq_rows(pp["vb"][p], b), u_b[p][b]], axis=0),
                      jnp.concatenate([seq_rows(pp["kg"][p], b), seq_rows(pp["bg"][p], b)], axis=0),
                      "tn", *PREC_UPD)
            st_ref[p * NB + b] = s_old[p][b] * pp["gtot"][p][b:b + 1, :] + jnp.where(cs.same_head, upd, 0.0)
    yield
    y = []
    for p in pairs:
        u_s = _stack(unseq_rows(u_b[p]).astype(BF16), cs.lo_half)
        y.append(rs[p] + _mm(a_r[p], jnp.concatenate([pp["vs"][p], u_s], axis=0), "nn", *PREC_Y))
    yield
    yc = []
    for p in pairs:
        yc.append(y[p] - _mm(y[p], cs.seg, "nn", *PREC_SEG) * (1.0 / HEAD_A))
    yield
    for p in pairs:
        sl = slice(p * PAIR, (p + 1) * PAIR)
        yv = _mm(yc[p] * yc[p], cs.seg, "nn", *PREC_SEG) * (1.0 / HEAD_A)
        yn = yc[p] * lax.rsqrt(yv + GN_EPS) * prow(PR_LNG)[:, sl] + prow(PR_LNB)[:, sl]
        o_ref[row0:row0 + rows, sl] = ((yn + pp["bon"][p] * pp["v"][p]) * pp["sg"][p]).astype(o_ref.dtype)


def _write_state(st_ref, sout_ref, cs, b0=0):
    for b in range(cs.NB):
        for p in range(N_PAIRS):
            sm = jnp.where(cs.same_head, st_ref[p * cs.NB + b], 0.0)
            sout_ref[0, b0 + b, p] = sm[:, :HEAD_A] + sm[:, HEAD_A:]


_PIPE_PATTERN = "pmmpmmmmpmmp"


def _wkv_kernel(r_ref, k_ref, v_ref, g_ref, l_ref, sh_ref, shl_ref, s0_ref, prm_ref, w2_ref,
                o_ref, sout_ref, shm_out_ref, shl_out_ref, st_ref, *, C, NB, NBLK):
    cs = _ScanConsts(C, NB)
    prm = prm_ref[0]
    w2 = w2_ref[0]
    refs = (r_ref, k_ref, v_ref, g_ref, l_ref)
    for q in range(NBLK * NB):
        for p in range(N_PAIRS):
            s = s0_ref[0, q, p]
            s2 = jnp.concatenate([s, s], axis=1)
            st_ref[q // NB, p * NB + q % NB] = jnp.where(cs.same_head, s2, 0.0)

    def prep(blk):
        raw = tuple(q[blk * ROWS:(blk + 1) * ROWS, :].astype(F32) for q in refs)
        seqs = range(blk * NB, (blk + 1) * NB)
        prev = ([sh_ref[0, 0, q] for q in seqs], [sh_ref[0, 1, q] for q in seqs],
                [sh_ref[0, 2, q] for q in seqs], [shl_ref[0, q] for q in seqs])
        return _prep_steps(raw, prev, prm, w2, cs)

    pp = _run(prep(0))
    for blk in range(NBLK):
        scan = _scan_steps(pp, st_ref.at[blk], o_ref, blk * ROWS, prm, cs)
        if blk + 1 < NBLK:
            _, pp = _zip_run(scan, prep(blk + 1), _PIPE_PATTERN)
        else:
            _run(scan)

    for q in range(NBLK * NB):
        last = (q + 1) * C - 1
        shm_out_ref[0, q] = r_ref[last:last + 1, :].astype(F32)
        shm_out_ref[1, q] = k_ref[last:last + 1, :].astype(F32)
        shm_out_ref[2, q] = v_ref[last:last + 1, :].astype(F32)
        shl_out_ref[q] = l_ref[last:last + 1, :].astype(F32)
    for blk in range(NBLK):
        _write_state(st_ref.at[blk], sout_ref, cs, blk * NB)


def _wkv_sample(proj, n_seq, C, NB, sh_main, sh_lora, s0, prm, w2, layer, nblk=2):
    assert NB * C == ROWS
    nsq = NB * nblk
    rows = ROWS * nblk

    def rmap(col):
        return lambda i: (i, col)

    state_spec = pl.BlockSpec((1, nsq, N_PAIRS, PAIR, HEAD_A), lambda i: (layer, i, 0, 0, 0))
    return pl.pallas_call(
        functools.partial(_wkv_kernel, C=C, NB=NB, NBLK=nblk),
        grid=(n_seq // nsq,),
        in_specs=[pl.BlockSpec((rows, D_A), rmap(P_R // D_A)),
                  pl.BlockSpec((rows, D_A), rmap(P_K // D_A)),
                  pl.BlockSpec((rows, D_A), rmap(P_V // D_A)),
                  pl.BlockSpec((rows, D_A), rmap(P_GA // D_A)),
                  pl.BlockSpec((rows, PAIR), rmap(P_LORA // PAIR)),
                  pl.BlockSpec((1, 3, nsq, 1, D_A), lambda i: (layer, 0, i, 0, 0)),
                  pl.BlockSpec((1, nsq, 1, PAIR), lambda i: (layer, i, 0, 0)),
                  state_spec,
                  pl.BlockSpec((1, 16, D_A), lambda i: (layer, 0, 0)),
                  pl.BlockSpec((1, PAIR, D_A), lambda i: (layer, 0, 0))],
        out_specs=[pl.BlockSpec((rows, D_A), lambda i: (i, 0)),
                   state_spec,
                   pl.BlockSpec((3, nsq, 1, D_A), lambda i: (0, i, 0, 0)),
                   pl.BlockSpec((nsq, 1, PAIR), lambda i: (i, 0, 0))],
        out_shape=[jax.ShapeDtypeStruct((n_seq * C, D_A), BF16),
                   jax.ShapeDtypeStruct(s0.shape, F32),
                   jax.ShapeDtypeStruct((3, n_seq, 1, D_A), F32),
                   jax.ShapeDtypeStruct((n_seq, 1, PAIR), F32)],
        scratch_shapes=[pltpu.VMEM((nblk, N_PAIRS * NB, PAIR, PAIR), F32)],
        input_output_aliases={7: 1},
        compiler_params=pltpu.CompilerParams(
            dimension_semantics=("parallel",), vmem_limit_bytes=VMEM_LIMIT),
        name="wkv_c%d" % C,
    )(proj, proj, proj, proj, proj, sh_main, sh_lora, s0, prm, w2)


def _wkv_pipe_kernel(rc_ref, kc_ref, vc_ref, gc_ref, lc_ref, rn_ref, kn_ref, vn_ref, gn_ref, ln_ref,
                     prm_ref, w2_ref, o_ref, sout_ref, shm_out_ref, shl_out_ref,
                     st_ref, pa_ref, pb_ref, pf_ref, pg_ref, *, n_steps):
    k = pl.program_id(1)
    rows = ROWS
    cs = _ScanConsts(rows, 1)
    prm = prm_ref[0]
    w2 = w2_ref[0]

    def raw_rows(refs, lo):
        return tuple(q[lo:lo + rows, :].astype(F32) for q in refs)

    def prev_rows(refs, row):
        rr, kr, vr, _, lr = refs
        return tuple([q[row:row + 1, :].astype(F32)] for q in (rr, kr, vr, lr))

    def store_pp(pp):
        for p in range(N_PAIRS):
            for j, n in enumerate(_PP_SIDE):
                pa_ref[p, j] = pp[n][p]
            for j, n in enumerate(_PP_STACK):
                pb_ref[p, j] = pp[n][p]
            for j, n in enumerate(_PP_F32):
                pf_ref[p, j] = pp[n][p]
            pg_ref[p] = pp["gtot"][p]

    def load_pp():
        pp = {n: [pa_ref[p, j] for p in range(N_PAIRS)] for j, n in enumerate(_PP_SIDE)}
        pp.update({n: [pb_ref[p, j] for p in range(N_PAIRS)] for j, n in enumerate(_PP_STACK)})
        pp.update({n: [pf_ref[p, j] for p in range(N_PAIRS)] for j, n in enumerate(_PP_F32)})
        pp["gtot"] = [pg_ref[p] for p in range(N_PAIRS)]
        return pp

    cur = (rc_ref, kc_ref, vc_ref, gc_ref, lc_ref)
    nxt = (rn_ref, kn_ref, vn_ref, gn_ref, ln_ref)

    @pl.when(k == 0)
    def _():
        st_ref[...] = jnp.zeros_like(st_ref)
        zero = ([jnp.zeros((1, D_A), F32)], [jnp.zeros((1, D_A), F32)], [jnp.zeros((1, D_A), F32)],
                [jnp.zeros((1, PAIR), F32)])
        store_pp(_run(_prep_steps(raw_rows(cur, 0), zero, prm, w2, cs)))

    pp_a = load_pp()
    _, pp_b = _zip_run(_scan_steps(pp_a, st_ref, o_ref, 0, prm, cs),
                       _prep_steps(raw_rows(cur, rows), prev_rows(cur, rows - 1), prm, w2, cs),
                       _PIPE_PATTERN)
    _, pp_n = _zip_run(_scan_steps(pp_b, st_ref, o_ref, rows, prm, cs),
                       _prep_steps(raw_rows(nxt, 0), prev_rows(cur, 2 * rows - 1), prm, w2, cs),
                       _PIPE_PATTERN)
    store_pp(pp_n)

    last = 2 * rows - 1
    shm_out_ref[0, 0] = rc_ref[last:last + 1, :].astype(F32)
    shm_out_ref[1, 0] = kc_ref[last:last + 1, :].astype(F32)
    shm_out_ref[2, 0] = vc_ref[last:last + 1, :].astype(F32)
    shl_out_ref[0] = lc_ref[last:last + 1, :].astype(F32)

    @pl.when(k == n_steps - 1)
    def _():
        _write_state(st_ref, sout_ref, cs)


def _wkv_prompt(proj, n_seq, seq_len, prm, w2, layer):
    n_steps = seq_len // (2 * ROWS)
    n_chunks = seq_len // ROWS

    def cmap(col):
        return lambda i, k: (i * n_steps + k, col)

    def nmap(col):
        return lambda i, k: (i * n_chunks + jnp.minimum(2 * k + 2, n_chunks - 1), col)

    def specs(rows_, m):
        return [pl.BlockSpec((rows_, D_A), m(P_R // D_A)),
                pl.BlockSpec((rows_, D_A), m(P_K // D_A)),
                pl.BlockSpec((rows_, D_A), m(P_V // D_A)),
                pl.BlockSpec((rows_, D_A), m(P_GA // D_A)),
                pl.BlockSpec((rows_, PAIR), m(P_LORA // PAIR))]

    return pl.pallas_call(
        functools.partial(_wkv_pipe_kernel, n_steps=n_steps),
        grid=(n_seq, n_steps),
        in_specs=specs(2 * ROWS, cmap) + specs(ROWS, nmap) + [
            pl.BlockSpec((1, 16, D_A), lambda i, k: (layer, 0, 0)),
            pl.BlockSpec((1, PAIR, D_A), lambda i, k: (layer, 0, 0))],
        out_specs=[pl.BlockSpec((2 * ROWS, D_A), lambda i, k: (i * n_steps + k, 0)),
                   pl.BlockSpec((1, 1, N_PAIRS, PAIR, HEAD_A), lambda i, k: (0, i, 0, 0, 0)),
                   pl.BlockSpec((3, 1, 1, D_A), lambda i, k: (0, i, 0, 0)),
                   pl.BlockSpec((1, 1, PAIR), lambda i, k: (i, 0, 0))],
        out_shape=[jax.ShapeDtypeStruct((n_seq * seq_len, D_A), BF16),
                   jax.ShapeDtypeStruct((1, n_seq, N_PAIRS, PAIR, HEAD_A), F32),
                   jax.ShapeDtypeStruct((3, n_seq, 1, D_A), F32),
                   jax.ShapeDtypeStruct((n_seq, 1, PAIR), F32)],
        scratch_shapes=[pltpu.VMEM((N_PAIRS, PAIR, PAIR), F32),
                        pltpu.VMEM((N_PAIRS, len(_PP_SIDE), ROWS, PAIR), BF16),
                        pltpu.VMEM((N_PAIRS, len(_PP_STACK), 2 * ROWS, PAIR), BF16),
                        pltpu.VMEM((N_PAIRS, len(_PP_F32), ROWS, PAIR), F32),
                        pltpu.VMEM((N_PAIRS, 1, PAIR), F32)],
        compiler_params=pltpu.CompilerParams(
            dimension_semantics=("parallel", "arbitrary"), vmem_limit_bytes=VMEM_LIMIT),
        name="wkv_pipe",
    )(*([proj] * 10), prm, w2)


HALO = 16


def _window_sums(ext):
    w2 = ext + pltpu.roll(ext, 1, 0)
    w4 = w2 + pltpu.roll(w2, 2, 0)
    w8 = w4 + pltpu.roll(w4, 4, 0)
    w16 = w8 + pltpu.roll(w8, 8, 0)
    return (w2, w4, w8, w16)


def _pool_gate(pooled_groups, gb, pw_ref, pscale):
    mixed = [_dot(pg.astype(BF16), pw_ref[g]) for g, pg in enumerate(pooled_groups)]
    yb = jnp.concatenate(mixed, axis=1) * pscale
    return yb * _silu(gb)


def _layer_norm_v(vc, ln_g):
    vm = jnp.mean(vc, axis=-1, keepdims=True)
    d = vc - vm
    vv = jnp.mean(d * d, axis=-1, keepdims=True)
    return d * lax.rsqrt(vv + LN_EPS) * ln_g


def _chunk_gate(vn, uc, gc, wm_ref, bm_ref):
    n_rows = vn.shape[0]
    vnb = vn.astype(BF16)
    outs = []
    for j in range(n_rows // CHUNK):
        rs = slice(j * CHUNK, (j + 1) * CHUNK)
        mix = [_dot(wm_ref[g], vnb[rs, g * GC:(g + 1) * GC]) + bm_ref[g] for g in range(N_GROUPS_C)]
        outs.append(jnp.concatenate(mix, axis=1))
    mix = outs[0] if len(outs) == 1 else jnp.concatenate(outs, axis=0)
    return uc * mix * _silu(gc)


def _bc_prompt_kernel(ub_ref, gb_ref, uc_ref, vc_ref, gc_ref, pw_ref, ps_ref, lng_ref, wm_ref, bm_ref,
                      o_ref, halo_ref, *, tt):
    j = pl.program_id(1)

    @pl.when(j == 0)
    def _():
        halo_ref[...] = jnp.zeros_like(halo_ref)

    u = ub_ref[...].astype(F32)
    ext = jnp.concatenate([halo_ref[...], u], axis=0)
    halo_ref[...] = u[tt - HALO:, :]
    sums = _window_sums(ext)
    pos = j * tt + lax.broadcasted_iota(jnp.int32, (tt, 1), 0)
    pooled = []
    for g, win in enumerate(POOL_WINDOWS):
        ls = slice(g * POOL_GC, (g + 1) * POOL_GC)
        cnt = jnp.minimum(pos + 1, win).astype(F32)
        pooled.append(sums[g][HALO:, ls] / cnt - u[:, ls])
    gb = gb_ref[...].astype(F32)
    o_ref[:, :D_B] = _pool_gate(pooled, gb, pw_ref, ps_ref[...]).astype(o_ref.dtype)
    vn = _layer_norm_v(vc_ref[...].astype(F32), lng_ref[...])
    uc = uc_ref[...].astype(F32)
    gc = gc_ref[...].astype(F32)
    o_ref[:, D_B:] = _chunk_gate(vn, uc, gc, wm_ref, bm_ref).astype(o_ref.dtype)


def _bc_prompt(proj, n_seq, seq_len, pw, pscale, ln_g, wm, bm, tt=512):
    n_t = seq_len // tt

    def rmap(col):
        return lambda i, j: (i * n_t + j, col)

    wspec = pl.BlockSpec((4, 128, 128), lambda i, j: (0, 0, 0))
    vspec = pl.BlockSpec((1, D_B), lambda i, j: (0, 0))
    return pl.pallas_call(
        functools.partial(_bc_prompt_kernel, tt=tt),
        grid=(n_seq, n_t),
        in_specs=[pl.BlockSpec((tt, D_B), rmap(P_UB // D_B)),
                  pl.BlockSpec((tt, D_B), rmap(P_GB // D_B)),
                  pl.BlockSpec((tt, D_B), rmap(P_UC // D_B)),
                  pl.BlockSpec((tt, D_B), rmap(P_VC // D_B)),
                  pl.BlockSpec((tt, D_B), rmap(P_GC // D_B)),
                  wspec, vspec, vspec, wspec, wspec],
        out_specs=pl.BlockSpec((tt, D_B + D_C), rmap(0)),
        out_shape=jax.ShapeDtypeStruct((n_seq * seq_len, D_B + D_C), BF16),
        scratch_shapes=[pltpu.VMEM((HALO, D_B), F32)],
        compiler_params=pltpu.CompilerParams(
            dimension_semantics=("parallel", "arbitrary"), vmem_limit_bytes=VMEM_LIMIT),
        name="bc_prompt",
    )(proj, proj, proj, proj, proj, pw, pscale, ln_g, wm, bm)


def _bc_sample_kernel(buf_ref, ub_ref, gb_ref, uc_ref, vc_ref, gc_ref, pw_ref, ps_ref, lng_ref,
                      wm_ref, bm_ref, o_ref, vn_ref, *, nb, t_len):
    u3 = ub_ref[...].astype(F32)
    ext = jnp.concatenate([buf_ref[...], u3], axis=1)
    per = HALO + t_len
    sums = _window_sums(ext.reshape(nb * per, D_B))
    u = u3.reshape(nb * t_len, D_B)
    pooled = []
    for g, win in enumerate(POOL_WINDOWS):
        ls = slice(g * POOL_GC, (g + 1) * POOL_GC)
        s3 = sums[g].reshape(nb, per, D_B)[:, HALO:, ls].reshape(nb * t_len, POOL_GC)
        cnt = float(min(PAST_LEN + 1, win))
        pooled.append(s3 / cnt - u[:, ls])
    rows = nb * t_len
    gb = gb_ref[...].astype(F32).reshape(rows, D_B)
    o_ref[:, :D_B] = _pool_gate(pooled, gb, pw_ref, ps_ref[...]).astype(o_ref.dtype)
    vn = _layer_norm_v(vc_ref[...].astype(F32).reshape(rows, D_C), lng_ref[...])
    vn_ref[...] = vn
    uc = uc_ref[...].astype(F32).reshape(rows, D_C)
    gc = gc_ref[...].astype(F32).reshape(rows, D_C)
    o_ref[:, D_B:] = _chunk_gate(vn, uc, gc, wm_ref, bm_ref).astype(o_ref.dtype)


def _bc_sample(proj3, seq0, n_seq, t_len, buf16, pw, pscale, ln_g, wm, bm):
    nb = CHUNK // t_len
    sb0 = seq0 // nb

    def rmap(col):
        return lambda i: (sb0 + i, 0, col)

    wspec = pl.BlockSpec((4, 128, 128), lambda i: (0, 0, 0))
    vspec = pl.BlockSpec((1, D_B), lambda i: (0, 0))
    rows = nb * t_len
    return pl.pallas_call(
        functools.partial(_bc_sample_kernel, nb=nb, t_len=t_len),
        grid=(n_seq // nb,),
        in_specs=[pl.BlockSpec((nb, HALO, D_B), lambda i: (i, 0, 0)),
                  pl.BlockSpec((nb, t_len, D_B), rmap(P_UB // D_B)),
                  pl.BlockSpec((nb, t_len, D_B), rmap(P_GB // D_B)),
                  pl.BlockSpec((nb, t_len, D_B), rmap(P_UC // D_B)),
                  pl.BlockSpec((nb, t_len, D_B), rmap(P_VC // D_B)),
                  pl.BlockSpec((nb, t_len, D_B), rmap(P_GC // D_B)),
                  wspec, vspec, vspec, wspec, wspec],
        out_specs=[pl.BlockSpec((rows, D_B + D_C), lambda i: (i, 0)),
                   pl.BlockSpec((rows, D_C), lambda i: (i, 0))],
        out_shape=[jax.ShapeDtypeStruct((n_seq * t_len, D_B + D_C), BF16),
                   jax.ShapeDtypeStruct((n_seq * t_len, D_C), F32)],
        compiler_params=pltpu.CompilerParams(
            dimension_semantics=("parallel",), vmem_limit_bytes=VMEM_LIMIT),
        name="bc_sample",
    )(buf16, proj3, proj3, proj3, proj3, proj3, pw, pscale, ln_g, wm, bm)


WPREP_ROWS = 256


def _wprep_kernel(w_ref, o_ref):
    o_ref[0, :, :3 * D_A] = w_ref[0, :, :3 * D_A].astype(BF16)
    o_ref[0, :, 3 * D_A:P_LORA] = w_ref[0, :, SHIFT_W:].astype(BF16)
    o_ref[0, :, P_LORA:] = w_ref[0, :, 3 * D_A:SHIFT_W].astype(BF16)


def _prep_w_in(w_in):
    spec = pl.BlockSpec((1, WPREP_ROWS, D_INP), lambda l, i: (l, i, 0))
    return pl.pallas_call(
        _wprep_kernel,
        grid=(DEPTH, D_MODEL // WPREP_ROWS),
        in_specs=[spec],
        out_specs=spec,
        out_shape=jax.ShapeDtypeStruct((DEPTH, D_MODEL, D_INP), BF16),
        compiler_params=pltpu.CompilerParams(
            dimension_semantics=("parallel", "parallel"), vmem_limit_bytes=VMEM_LIMIT),
        name="wprep",
    )(w_in)


def kernel(x_prompt, x_sample, state_shift, state_wkv, state_pool, norm_g, final_norm_g, w_in,
           shift_mu, w0, w_up, a0, a_up, k_k, k_a, r_k, lnx_g, lnx_b, pool_w, pool_scale,
           gmlp_ln_g, gmlp_ws, gmlp_b, w_out):
    bp, seq, _ = x_prompt.shape
    bs, dseq, _ = x_sample.shape
    n_p = bp * seq
    n_s = bs * dseq
    xp = x_prompt.reshape(n_p, D_MODEL)
    xs = x_sample.reshape(n_s, D_MODEL)
    hp = _norm_rows(xp, norm_g[0][None])
    hs = _norm_rows(xs, norm_g[0][None])

    w_in_p = _prep_w_in(w_in)
    w_out_h = w_out.astype(BF16)
    mu_l = jnp.pad(shift_mu[:, 3 * D_A:], ((0, 0), (0, D_A - 2 * LORA)))
    prm = jnp.stack([shift_mu[:, :D_A], shift_mu[:, D_A:2 * D_A], shift_mu[:, 2 * D_A:3 * D_A],
                     w0, a0, k_k, k_a, r_k.reshape(DEPTH, D_A), lnx_g, lnx_b, mu_l], axis=1)
    prm = jnp.pad(prm, ((0, 0), (0, 16 - prm.shape[1]), (0, 0)))
    w2 = jnp.concatenate([w_up, a_up], axis=1)
    pw = pool_w.astype(BF16)
    tril = jnp.tril(jnp.ones((CHUNK, CHUNK), F32))
    wm_p = (gmlp_ws * tril).astype(BF16)
    bm_p = jnp.broadcast_to(gmlp_b[:, :, :, None], (DEPTH, N_GROUPS_C, CHUNK, GC))
    nb_s = CHUNK // dseq
    eye_b = jnp.eye(nb_s, dtype=F32)
    ws_small = gmlp_ws[:, :, :dseq, :dseq] * tril[:dseq, :dseq]
    wm_s = jnp.einsum('ab,lgts->lgatbs', eye_b, ws_small).reshape(DEPTH, N_GROUPS_C, CHUNK, CHUNK)
    wm_s = wm_s.astype(BF16)
    bm_s = jnp.broadcast_to(jnp.tile(gmlp_b[:, :, :dseq], (1, 1, nb_s))[:, :, :, None],
                            (DEPTH, N_GROUPS_C, CHUNK, GC))

    ssh_main = state_shift[:, :, :3 * D_A].reshape(DEPTH, bs, 3, 1, D_A).transpose(0, 2, 1, 3, 4)
    ssh_lora = state_shift[:, :, 3 * D_A:].reshape(DEPTH, bs, 1, PAIR)
    wkv_s = state_wkv.reshape(DEPTH, bs, N_PAIRS, PAIR, HEAD_A)
    buf16 = jnp.pad(state_pool, ((0, 0), (0, 0), (HALO - POOL_BUF, 0), (0, 0)))

    p_shift, p_wkv, p_pool, s_shift, s_pool, s_v = [], [], [], [], [], []
    for l in range(DEPTH):
        final = l == DEPTH - 1
        g_out = final_norm_g[None] if final else norm_g[l + 1][None]
        bc_w = (pw[l], pool_scale[l][None], gmlp_ln_g[l][None])

        proj_p = _inproj_h(hp, w_in_p, l, 2048)
        ya_p, wk_p, shm_p, shl_p = _wkv_prompt(proj_p, bp, seq, prm, w2, l)
        cb_p = _bc_prompt(proj_p, bp, seq, *bc_w, wm_p[l], bm_p[l])
        xp = _outproj(ya_p, cb_p, w_out_h, l, xp, g_out, final)
        if not final:
            xp, hp = xp

        proj_s = _inproj_h(hs, w_in_p, l, 1024)
        ya_s, wkv_s, shm_s, shl_s = _wkv_sample(proj_s, bs, dseq, ROWS // dseq, ssh_main, ssh_lora,
                                                wkv_s, prm, w2, l)
        cb_s, vn_s = _bc_sample(proj_s.reshape(bs, dseq, D_INP), 0, bs, dseq, buf16[l], *bc_w,
                                wm_s[l], bm_s[l])
        xs = _outproj(ya_s, cb_s, w_out_h, l, xs, g_out, final)
        if not final:
            xs, hs = xs

        p_shift.append(jnp.concatenate([shm_p[0, :, 0], shm_p[1, :, 0], shm_p[2, :, 0], shl_p[:, 0]], axis=-1))
        s_shift.append(jnp.concatenate([shm_s[0, :, 0], shm_s[1, :, 0], shm_s[2, :, 0], shl_s[:, 0]], axis=-1))
        p_wkv.append(wk_p.reshape(bp, N_HEADS_A, HEAD_A, HEAD_A))
        p_pool.append(jnp.stack([proj_p[(b + 1) * seq - POOL_BUF:(b + 1) * seq, P_UB:P_UB + D_B]
                                 for b in range(bp)]).astype(F32))
        ub_s = proj_s[:, P_UB:P_UB + D_B].astype(F32).reshape(bs, dseq, D_B)
        s_pool.append(jnp.concatenate([state_pool[l], ub_s], axis=1)[:, -POOL_BUF:])
        s_v.append(vn_s.reshape(bs, dseq, D_C))

    y_prompt = xp.reshape(bp, seq, D_MODEL)
    y_sample = xs.reshape(bs, dseq, D_MODEL)
    s_wkv = wkv_s.reshape(DEPTH, bs, N_HEADS_A, HEAD_A, HEAD_A)
    return (y_prompt, y_sample, jnp.stack(p_shift), jnp.stack(p_wkv), jnp.stack(p_pool),
            jnp.stack(s_shift), s_wkv, jnp.stack(s_pool), jnp.stack(s_v))
```

```python
import functools

import jax
import jax.numpy as jnp
import numpy as np
from jax import lax
from jax.experimental import pallas as pl
from jax.experimental.pallas import tpu as pltpu

F32 = jnp.float32
BF16 = jnp.bfloat16

D_MODEL = 2048
DEPTH = 4
PAST_LEN = 16384
D_A = 1024
HEAD_A = 64
N_HEADS_A = 16
LORA = 64
D_B = 512
POOL_WINDOWS = (2, 4, 8, 16)
POOL_GC = 128
POOL_BUF = 15
D_C = 512
N_GROUPS_C = 4
GC = 128
CHUNK = 128
SHIFT_W = 3 * D_A + 2 * LORA
EPS = 1e-6
GN_EPS = HEAD_A * 1e-5
LN_EPS = 1e-5

P_R, P_K, P_V, P_GA = 0, 1024, 2048, 3072
P_UB, P_GB, P_UC, P_VC, P_GC = 4096, 4608, 5120, 5632, 6144
P_LORA = 6656
D_INP = 6784

PAIR = 128
N_PAIRS = D_A // PAIR
ROWS = 64

PROJ_DTYPE = BF16
VMEM_LIMIT = 52 * 1024 * 1024


def _dot(a, b):
    return jnp.dot(a, b, preferred_element_type=F32)


def _split(x, n):
    pieces = []
    rem = x
    for i in range(n):
        hi = rem.astype(BF16)
        pieces.append(hi)
        if i + 1 < n:
            rem = rem - hi.astype(F32)
    return pieces


_CONTRACT = {"nn": (1, 0), "nt": (1, 1), "tn": (0, 0)}


def _mm(a, b, mode="nn", pa=1, pb=1):
    ca, cb = _CONTRACT[mode]
    sa, sb = _split(a, pa), _split(b, pb)
    terms = [(i, j) for i in range(pa) for j in range(pb) if i + j < max(pa, pb)]
    lhs = jnp.concatenate([sa[i] for i, _ in terms], axis=ca) if len(terms) > 1 else sa[0]
    rhs = jnp.concatenate([sb[j] for _, j in terms], axis=cb) if len(terms) > 1 else sb[0]
    return lax.dot_general(lhs, rhs, (((ca,), (cb,)), ((), ())), preferred_element_type=F32)


PREC_LORA = (1, 1)
PREC_CUM = (1, 2)
PREC_SEG = (1, 1)
PREC_G = (1, 1)
PREC_INV = (1, 1)
PREC_X2 = (1, 1)
PREC_WU = (1, 1)
PREC_US = (1, 1)
PREC_UPD = (1, 1)
PREC_Y = (1, 1)


def _norm_kernel(x_ref, g_ref, o_ref):
    x = x_ref[...]
    ms = jnp.mean(x * x, axis=-1, keepdims=True)
    o_ref[...] = ((x * lax.rsqrt(ms + EPS)) * g_ref[...]).astype(BF16)


def _norm_rows(x, g, tm=512):
    m = x.shape[0]
    return pl.pallas_call(
        _norm_kernel,
        grid=(m // tm,),
        in_specs=[pl.BlockSpec((tm, D_MODEL), lambda i: (i, 0)),
                  pl.BlockSpec((1, D_MODEL), lambda i: (0, 0))],
        out_specs=pl.BlockSpec((tm, D_MODEL), lambda i: (i, 0)),
        out_shape=jax.ShapeDtypeStruct((m, D_MODEL), BF16),
        compiler_params=pltpu.CompilerParams(
            dimension_semantics=("parallel",), vmem_limit_bytes=VMEM_LIMIT),
        name="norm_rows",
    )(x, g)


def _inproj_h_kernel(h_ref, w_ref, o_ref):
    o_ref[...] = _dot(h_ref[...], w_ref[0]).astype(o_ref.dtype)


def _inproj_h(h, w, layer, tm, tn=768):
    m = h.shape[0]
    return pl.pallas_call(
        _inproj_h_kernel,
        grid=(m // tm, pl.cdiv(D_INP, tn)),
        in_specs=[pl.BlockSpec((tm, D_MODEL), lambda i, j: (i, 0)),
                  pl.BlockSpec((1, D_MODEL, tn), lambda i, j: (layer, 0, j))],
        out_specs=pl.BlockSpec((tm, tn), lambda i, j: (i, j)),
        out_shape=jax.ShapeDtypeStruct((m, D_INP), PROJ_DTYPE),
        compiler_params=pltpu.CompilerParams(
            dimension_semantics=("parallel", "arbitrary"), vmem_limit_bytes=VMEM_LIMIT),
        name="inproj_h",
    )(h, w)


def _outproj_kernel(ca_ref, cb_ref, wa_ref, wb_ref, x_ref, g_ref, *o_refs, final):
    y = _dot(ca_ref[...], wa_ref[0]) + _dot(cb_ref[...], wb_ref[0])
    out = x_ref[...] + y
    ms = jnp.mean(out * out, axis=-1, keepdims=True)
    normed = (out * lax.rsqrt(ms + EPS)) * g_ref[...]
    if final:
        o_refs[0][...] = normed
    else:
        o_refs[0][...] = out
        o_refs[1][...] = normed.astype(BF16)


def _outproj(cat_a, cat_b, w, layer, x, g, final, tm=512):
    m = x.shape[0]
    half = D_MODEL // 2
    row_spec = pl.BlockSpec((tm, D_MODEL), lambda i: (i, 0))
    f32_out = jax.ShapeDtypeStruct((m, D_MODEL), F32)
    return pl.pallas_call(
        functools.partial(_outproj_kernel, final=final),
        grid=(m // tm,),
        in_specs=[pl.BlockSpec((tm, half), lambda i: (i, 0)),
                  pl.BlockSpec((tm, half), lambda i: (i, 0)),
                  pl.BlockSpec((1, half, D_MODEL), lambda i: (layer, 0, 0)),
                  pl.BlockSpec((1, half, D_MODEL), lambda i: (layer, 1, 0)),
                  pl.BlockSpec((tm, D_MODEL), lambda i: (i, 0)),
                  pl.BlockSpec((1, D_MODEL), lambda i: (0, 0))],
        out_specs=row_spec if final else [row_spec, row_spec],
        out_shape=f32_out if final else [f32_out, jax.ShapeDtypeStruct((m, D_MODEL), BF16)],
        compiler_params=pltpu.CompilerParams(
            dimension_semantics=("parallel",), vmem_limit_bytes=VMEM_LIMIT),
        name="outproj",
    )(cat_a, cat_b, w, w, x, g)


PR_MU_R, PR_MU_K, PR_MU_V, PR_W0, PR_A0, PR_KK, PR_KA, PR_RK, PR_LNG, PR_LNB, PR_MU_L = range(11)


EXP_M_HALF = float(np.exp(-0.5))


NEG_LOG2E = -float(np.log2(np.e))


def _exp_neg(z):
    return jnp.exp2(z * NEG_LOG2E)


def _sigmoid(z):
    return 1.0 / (1.0 + _exp_neg(z))


def _silu(z):
    return z * _sigmoid(z)


def _run(gen):
    try:
        while True:
            next(gen)
    except StopIteration as e:
        return e.value


def _zip_run(main, prep, pattern):
    done = {}

    def step(gen, key):
        if key not in done:
            try:
                next(gen)
            except StopIteration as e:
                done[key] = e.value

    for ch in pattern:
        step(main if ch == "m" else prep, ch)
    while "m" not in done:
        step(main, "m")
    while "p" not in done:
        step(prep, "p")
    return done["m"], done["p"]


class _ScanConsts:
    def __init__(self, C, NB):
        rows = ROWS
        self.C, self.NB = C, NB
        row_id = lax.broadcasted_iota(jnp.int32, (rows, 1), 0)
        self.first = (row_id % C) == 0
        self.lo_half = lax.broadcasted_iota(jnp.int32, (rows, PAIR), 1) < HEAD_A
        ri = lax.broadcasted_iota(jnp.int32, (rows, rows), 0)
        ci = lax.broadcasted_iota(jnp.int32, (rows, rows), 1)
        self.tri = jnp.where(((ri // C) == (ci // C)) & (ri >= ci), 1.0, 0.0)
        pr = lax.broadcasted_iota(jnp.int32, (PAIR, PAIR), 0)
        pc = lax.broadcasted_iota(jnp.int32, (PAIR, PAIR), 1)
        self.same_head = (pr // HEAD_A) == (pc // HEAD_A)
        self.seg = jnp.where(self.same_head, 1.0, 0.0)
        rp = lax.broadcasted_iota(jnp.int32, (rows, PAIR), 0)
        cp = lax.broadcasted_iota(jnp.int32, (rows, PAIR), 1) % rows
        same_seq = (rp // C) == (cp // C)
        self.m_strict = same_seq & (rp > cp)
        m_incl = same_seq & (rp >= cp)
        self.m_incl2 = jnp.concatenate([m_incl, m_incl], axis=1)
        self.eye = jnp.where(rp == cp, 1.0, 0.0)


def _bmask(m, xb):
    return jnp.where(m, xb, jnp.zeros_like(xb))


_PP_SIDE = ("at", "rt", "kg", "bg", "vb")
_PP_STACK = ("ats", "khs", "bhs", "vs")
_PP_F32 = ("bon", "v", "sg")


def _stack(xb, lo_half):
    zb = jnp.zeros_like(xb)
    return jnp.concatenate([jnp.where(lo_half, xb, zb), jnp.where(lo_half, zb, xb)], axis=0)


def _prep_steps(raw, prev, prm, w2, cs):
    C, NB, rows = cs.C, cs.NB, ROWS
    r_raw, k_raw, v_raw, g_raw, l_raw = raw
    prev_r, prev_k, prev_v, prev_l = prev

    def prow(i):
        return prm[i:i + 1, :]

    def shift_mix(x, prev_rows, mu):
        rolled = pltpu.roll(x, 1, 0)
        parts = []
        for b, q in enumerate(prev_rows):
            head = jnp.where(cs.first[:8], jnp.broadcast_to(q, (8, x.shape[1])), rolled[b * C:b * C + 8])
            parts += [head] if C == 8 else [head, rolled[b * C + 8:(b + 1) * C]]
        shifted = jnp.concatenate(parts, axis=0)
        return x + (shifted - x) * mu

    xl = shift_mix(l_raw, prev_l, prow(PR_MU_L)[:, :PAIR])
    lora_w = _mm(jnp.where(cs.lo_half, jnp.tanh(xl), 0.0), w2, "nn", *PREC_LORA)
    lora_a = _mm(jnp.where(cs.lo_half, 0.0, xl), w2, "nn", *PREC_LORA)
    yield
    r = shift_mix(r_raw, prev_r, prow(PR_MU_R))
    k = shift_mix(k_raw, prev_k, prow(PR_MU_K))
    v = shift_mix(v_raw, prev_v, prow(PR_MU_V))
    dec = EXP_M_HALF * _sigmoid(prow(PR_W0) + lora_w)
    a = _sigmoid(prow(PR_A0) + lora_a)
    cum = _mm(cs.tri, -dec, "nn", *PREC_CUM)
    yield
    kk_raw = k * prow(PR_KK)
    k2 = k * (1.0 + (a - 1.0) * prow(PR_KA))
    rkr = r * k2 * prow(PR_RK)
    sls = [slice(p * PAIR, (p + 1) * PAIR) for p in range(N_PAIRS)]
    s0 = [_mm(jnp.concatenate([kk_raw[:, sl] * kk_raw[:, sl], rkr[:, sl]], axis=0), cs.seg, "nn", *PREC_SEG)
          for sl in sls]
    yield
    g_incl = jnp.exp(cum)
    g_excl = jnp.exp(cum + dec)
    g_inv = _exp_neg(cum)
    g_tot_rows = [jnp.exp(cum[(b + 1) * C - 1:(b + 1) * C, :]) for b in range(NB)]
    g_rest = g_inv * jnp.concatenate([jnp.broadcast_to(q, (C, D_A)) for q in g_tot_rows], axis=0)
    sg = _silu(g_raw)

    pp = {n: [] for n in _PP_SIDE + _PP_STACK + _PP_F32 + ("gtot",)}
    for sl, q in zip(sls, s0):
        kkp = kk_raw[:, sl] * lax.rsqrt(jnp.maximum(q[:rows], 1e-12))
        kka = kkp * a[:, sl]
        at = (-kkp * g_excl[:, sl]).astype(BF16)
        vb = v[:, sl].astype(BF16)
        pp["at"].append(at)
        pp["rt"].append((r[:, sl] * g_incl[:, sl]).astype(BF16))
        pp["kg"].append((k2[:, sl] * g_rest[:, sl]).astype(BF16))
        pp["bg"].append((kka * g_rest[:, sl]).astype(BF16))
        pp["vb"].append(vb)
        pp["ats"].append(_stack(at, cs.lo_half))
        pp["khs"].append(_stack((k2[:, sl] * g_inv[:, sl]).astype(BF16), cs.lo_half))
        pp["bhs"].append(_stack((kka * g_inv[:, sl]).astype(BF16), cs.lo_half))
        pp["vs"].append(_stack(vb, cs.lo_half))
        pp["bon"].append(q[rows:])
        pp["v"].append(v[:, sl])
        pp["sg"].append(sg[:, sl])
        pp["gtot"].append(jnp.concatenate([q[:, sl] for q in g_tot_rows], axis=0))
    return pp


def _scan_steps(pp, st_ref, o_ref, row0, prm, cs):
    C, NB, rows = cs.C, cs.NB, ROWS
    n_rounds = int(np.log2(C))
    pairs = range(N_PAIRS)

    def prow(i):
        return prm[i:i + 1, :]

    def seq_rows(x, b):
        if NB == 1:
            return x
        return x.astype(F32)[b * C:(b + 1) * C]

    def unseq_rows(pieces):
        return pieces[0] if NB == 1 else jnp.concatenate(pieces, axis=0)

    a_ak, a_r, pwb, t_inv = [], [], [], []
    for p in pairs:
        gm = _mm(jnp.concatenate([pp["at"][p], pp["rt"][p]], axis=0),
                 jnp.concatenate([pp["khs"][p], pp["bhs"][p]], axis=0), "nt", *PREC_G)
        gmb = gm.astype(BF16)
        a_ak.append(_bmask(cs.m_strict, gmb[:rows, :PAIR]))
        a_r.append(_bmask(cs.m_incl2, gmb[rows:, :]))
        pwb.append(_bmask(cs.m_strict, gmb[:rows, PAIR:]))
        t_inv.append(cs.eye + jnp.where(cs.m_strict, gm[:rows, PAIR:], 0.0))
    yield

    x2 = []
    for p in pairs:
        x2.append(_mm(a_ak[p], pp["vs"][p], "nn", *PREC_X2).astype(BF16))
        pwb[p] = _mm(pwb[p], _stack(pwb[p], cs.lo_half), "nn", *PREC_INV).astype(BF16)
    yield
    for rd in range(1, n_rounds):
        for p in pairs:
            tb = t_inv[p].astype(BF16)
            pws = _stack(pwb[p], cs.lo_half)
            if rd < n_rounds - 1:
                res = _mm(jnp.concatenate([pwb[p], tb], axis=0), pws, "nn", *PREC_INV)
                t_inv[p] = t_inv[p] + res[rows:]
                pwb[p] = res[:rows].astype(BF16)
            else:
                t_inv[p] = t_inv[p] + _mm(tb, pws, "nn", *PREC_INV)
        yield

    w_t, u_t = [], []
    for p in pairs:
        wu = _mm(t_inv[p], jnp.concatenate([pp["ats"][p], _stack(x2[p], cs.lo_half)], axis=1),
                 "nn", *PREC_WU)
        w_t.append(wu[:, :PAIR].astype(BF16))
        u_t.append(wu[:, PAIR:])
    yield

    s_old = [[st_ref[p * NB + b] for b in range(NB)] for p in pairs]
    u_b, rs = [], []
    for p in pairs:
        us = [_mm(jnp.concatenate([seq_rows(w_t[p], b), seq_rows(pp["rt"][p], b)], axis=0),
                  s_old[p][b], "nt", *PREC_US) for b in range(NB)]
        u_b.append([us[b][:C] + seq_rows(u_t[p], b) for b in range(NB)])
        rs.append(unseq_rows([us[b][C:] for b in range(NB)]))
    yield
    for p in pairs:
        for b in range(NB):
            upd = _mm(jnp.concatenate([seq_rows(pp["vb"][p], b), u_b[p][b]], axis=0),
                      jnp.concatenate([seq_rows(pp["kg"][p], b), seq_rows(pp["bg"][p], b)], axis=0),
                      "tn", *PREC_UPD)
            st_ref[p * NB + b] = s_old[p][b] * pp["gtot"][p][b:b + 1, :] + jnp.where(cs.same_head, upd, 0.0)
    yield
    y = []
    for p in pairs:
        u_s = _stack(unseq_rows(u_b[p]).astype(BF16), cs.lo_half)
        y.append(rs[p] + _mm(a_r[p], jnp.concatenate([pp["vs"][p], u_s], axis=0), "nn", *PREC_Y))
    yield
    yc = []
    for p in pairs:
        yc.append(y[p] - _mm(y[p], cs.seg, "nn", *PREC_SEG) * (1.0 / HEAD_A))
    yield
    for p in pairs:
        sl = slice(p * PAIR, (p + 1) * PAIR)
        yv = _mm(yc[p] * yc[p], cs.seg, "nn", *PREC_SEG) * (1.0 / HEAD_A)
        yn = yc[p] * lax.rsqrt(yv + GN_EPS) * prow(PR_LNG)[:, sl] + prow(PR_LNB)[:, sl]
        o_ref[row0:row0 + rows, sl] = ((yn + pp["bon"][p] * pp["v"][p]) * pp["sg"][p]).astype(o_ref.dtype)


def _write_state(st_ref, sout_ref, cs, b0=0):
    for b in range(cs.NB):
        for p in range(N_PAIRS):
            sm = jnp.where(cs.same_head, st_ref[p * cs.NB + b], 0.0)
            sout_ref[0, b0 + b, p] = sm[:, :HEAD_A] + sm[:, HEAD_A:]


_PIPE_PATTERN = "pmmpmmmmpmmp"


def _wkv_kernel(r_ref, k_ref, v_ref, g_ref, l_ref, sh_ref, shl_ref, s0_ref, prm_ref, w2_ref,
                o_ref, sout_ref, shm_out_ref, shl_out_ref, st_ref, *, C, NB, NBLK):
    cs = _ScanConsts(C, NB)
    prm = prm_ref[0]
    w2 = w2_ref[0]
    refs = (r_ref, k_ref, v_ref, g_ref, l_ref)
    for q in range(NBLK * NB):
        for p in range(N_PAIRS):
            s = s0_ref[0, q, p]
            s2 = jnp.concatenate([s, s], axis=1)
            st_ref[q // NB, p * NB + q % NB] = jnp.where(cs.same_head, s2, 0.0)

    def prep(blk):
        raw = tuple(q[blk * ROWS:(blk + 1) * ROWS, :].astype(F32) for q in refs)
        seqs = range(blk * NB, (blk + 1) * NB)
        prev = ([sh_ref[0, 0, q] for q in seqs], [sh_ref[0, 1, q] for q in seqs],
                [sh_ref[0, 2, q] for q in seqs], [shl_ref[0, q] for q in seqs])
        return _prep_steps(raw, prev, prm, w2, cs)

    pp = _run(prep(0))
    for blk in range(NBLK):
        scan = _scan_steps(pp, st_ref.at[blk], o_ref, blk * ROWS, prm, cs)
        if blk + 1 < NBLK:
            _, pp = _zip_run(scan, prep(blk + 1), _PIPE_PATTERN)
        else:
            _run(scan)

    for q in range(NBLK * NB):
        last = (q + 1) * C - 1
        shm_out_ref[0, q] = r_ref[last:last + 1, :].astype(F32)
        shm_out_ref[1, q] = k_ref[last:last + 1, :].astype(F32)
        shm_out_ref[2, q] = v_ref[last:last + 1, :].astype(F32)
        shl_out_ref[q] = l_ref[last:last + 1, :].astype(F32)
    for blk in range(NBLK):
        _write_state(st_ref.at[blk], sout_ref, cs, blk * NB)


def _wkv_sample(proj, n_seq, C, NB, sh_main, sh_lora, s0, prm, w2, layer, nblk=2):
    assert NB * C == ROWS
    nsq = NB * nblk
    rows = ROWS * nblk

    def rmap(col):
        return lambda i: (i, col)

    state_spec = pl.BlockSpec((1, nsq, N_PAIRS, PAIR, HEAD_A), lambda i: (layer, i, 0, 0, 0))
    return pl.pallas_call(
        functools.partial(_wkv_kernel, C=C, NB=NB, NBLK=nblk),
        grid=(n_seq // nsq,),
        in_specs=[pl.BlockSpec((rows, D_A), rmap(P_R // D_A)),
                  pl.BlockSpec((rows, D_A), rmap(P_K // D_A)),
                  pl.BlockSpec((rows, D_A), rmap(P_V // D_A)),
                  pl.BlockSpec((rows, D_A), rmap(P_GA // D_A)),
                  pl.BlockSpec((rows, PAIR), rmap(P_LORA // PAIR)),
                  pl.BlockSpec((1, 3, nsq, 1, D_A), lambda i: (layer, 0, i, 0, 0)),
                  pl.BlockSpec((1, nsq, 1, PAIR), lambda i: (layer, i, 0, 0)),
                  state_spec,
                  pl.BlockSpec((1, 16, D_A), lambda i: (layer, 0, 0)),
                  pl.BlockSpec((1, PAIR, D_A), lambda i: (layer, 0, 0))],
        out_specs=[pl.BlockSpec((rows, D_A), lambda i: (i, 0)),
                   state_spec,
                   pl.BlockSpec((3, nsq, 1, D_A), lambda i: (0, i, 0, 0)),
                   pl.BlockSpec((nsq, 1, PAIR), lambda i: (i, 0, 0))],
        out_shape=[jax.ShapeDtypeStruct((n_seq * C, D_A), BF16),
                   jax.ShapeDtypeStruct(s0.shape, F32),
                   jax.ShapeDtypeStruct((3, n_seq, 1, D_A), F32),
                   jax.ShapeDtypeStruct((n_seq, 1, PAIR), F32)],
        scratch_shapes=[pltpu.VMEM((nblk, N_PAIRS * NB, PAIR, PAIR), F32)],
        input_output_aliases={7: 1},
        compiler_params=pltpu.CompilerParams(
            dimension_semantics=("parallel",), vmem_limit_bytes=VMEM_LIMIT),
        name="wkv_c%d" % C,
    )(proj, proj, proj, proj, proj, sh_main, sh_lora, s0, prm, w2)


def _wkv_pipe_kernel(rc_ref, kc_ref, vc_ref, gc_ref, lc_ref, rn_ref, kn_ref, vn_ref, gn_ref, ln_ref,
                     prm_ref, w2_ref, o_ref, sout_ref, shm_out_ref, shl_out_ref,
                     st_ref, pa_ref, pb_ref, pf_ref, pg_ref, *, n_steps):
    k = pl.program_id(1)
    rows = ROWS
    cs = _ScanConsts(rows, 1)
    prm = prm_ref[0]
    w2 = w2_ref[0]

    def raw_rows(refs, lo):
        return tuple(q[lo:lo + rows, :].astype(F32) for q in refs)

    def prev_rows(refs, row):
        rr, kr, vr, _, lr = refs
        return tuple([q[row:row + 1, :].astype(F32)] for q in (rr, kr, vr, lr))

    def store_pp(pp):
        for p in range(N_PAIRS):
            for j, n in enumerate(_PP_SIDE):
                pa_ref[p, j] = pp[n][p]
            for j, n in enumerate(_PP_STACK):
                pb_ref[p, j] = pp[n][p]
            for j, n in enumerate(_PP_F32):
                pf_ref[p, j] = pp[n][p]
            pg_ref[p] = pp["gtot"][p]

    def load_pp():
        pp = {n: [pa_ref[p, j] for p in range(N_PAIRS)] for j, n in enumerate(_PP_SIDE)}
        pp.update({n: [pb_ref[p, j] for p in range(N_PAIRS)] for j, n in enumerate(_PP_STACK)})
        pp.update({n: [pf_ref[p, j] for p in range(N_PAIRS)] for j, n in enumerate(_PP_F32)})
        pp["gtot"] = [pg_ref[p] for p in range(N_PAIRS)]
        return pp

    cur = (rc_ref, kc_ref, vc_ref, gc_ref, lc_ref)
    nxt = (rn_ref, kn_ref, vn_ref, gn_ref, ln_ref)

    @pl.when(k == 0)
    def _():
        st_ref[...] = jnp.zeros_like(st_ref)
        zero = ([jnp.zeros((1, D_A), F32)], [jnp.zeros((1, D_A), F32)], [jnp.zeros((1, D_A), F32)],
                [jnp.zeros((1, PAIR), F32)])
        store_pp(_run(_prep_steps(raw_rows(cur, 0), zero, prm, w2, cs)))

    pp_a = load_pp()
    _, pp_b = _zip_run(_scan_steps(pp_a, st_ref, o_ref, 0, prm, cs),
                       _prep_steps(raw_rows(cur, rows), prev_rows(cur, rows - 1), prm, w2, cs),
                       _PIPE_PATTERN)
    _, pp_n = _zip_run(_scan_steps(pp_b, st_ref, o_ref, rows, prm, cs),
                       _prep_steps(raw_rows(nxt, 0), prev_rows(cur, 2 * rows - 1), prm, w2, cs),
                       _PIPE_PATTERN)
    store_pp(pp_n)

    last = 2 * rows - 1
    shm_out_ref[0, 0] = rc_ref[last:last + 1, :].astype(F32)
    shm_out_ref[1, 0] = kc_ref[last:last + 1, :].astype(F32)
    shm_out_ref[2, 0] = vc_ref[last:last + 1, :].astype(F32)
    shl_out_ref[0] = lc_ref[last:last + 1, :].astype(F32)

    @pl.when(k == n_steps - 1)
    def _():
        _write_state(st_ref, sout_ref, cs)


def _wkv_prompt(proj, n_seq, seq_len, prm, w2, layer):
    n_steps = seq_len // (2 * ROWS)
    n_chunks = seq_len // ROWS

    def cmap(col):
        return lambda i, k: (i * n_steps + k, col)

    def nmap(col):
        return lambda i, k: (i * n_chunks + jnp.minimum(2 * k + 2, n_chunks - 1), col)

    def specs(rows_, m):
        return [pl.BlockSpec((rows_, D_A), m(P_R // D_A)),
                pl.BlockSpec((rows_, D_A), m(P_K // D_A)),
                pl.BlockSpec((rows_, D_A), m(P_V // D_A)),
                pl.BlockSpec((rows_, D_A), m(P_GA // D_A)),
                pl.BlockSpec((rows_, PAIR), m(P_LORA // PAIR))]

    return pl.pallas_call(
        functools.partial(_wkv_pipe_kernel, n_steps=n_steps),
        grid=(n_seq, n_steps),
        in_specs=specs(2 * ROWS, cmap) + specs(ROWS, nmap) + [
            pl.BlockSpec((1, 16, D_A), lambda i, k: (layer, 0, 0)),
            pl.BlockSpec((1, PAIR, D_A), lambda i, k: (layer, 0, 0))],
        out_specs=[pl.BlockSpec((2 * ROWS, D_A), lambda i, k: (i * n_steps + k, 0)),
                   pl.BlockSpec((1, 1, N_PAIRS, PAIR, HEAD_A), lambda i, k: (0, i, 0, 0, 0)),
                   pl.BlockSpec((3, 1, 1, D_A), lambda i, k: (0, i, 0, 0)),
                   pl.BlockSpec((1, 1, PAIR), lambda i, k: (i, 0, 0))],
        out_shape=[jax.ShapeDtypeStruct((n_seq * seq_len, D_A), BF16),
                   jax.ShapeDtypeStruct((1, n_seq, N_PAIRS, PAIR, HEAD_A), F32),
                   jax.ShapeDtypeStruct((3, n_seq, 1, D_A), F32),
                   jax.ShapeDtypeStruct((n_seq, 1, PAIR), F32)],
        scratch_shapes=[pltpu.VMEM((N_PAIRS, PAIR, PAIR), F32),
                        pltpu.VMEM((N_PAIRS, len(_PP_SIDE), ROWS, PAIR), BF16),
                        pltpu.VMEM((N_PAIRS, len(_PP_STACK), 2 * ROWS, PAIR), BF16),
                        pltpu.VMEM((N_PAIRS, len(_PP_F32), ROWS, PAIR), F32),
                        pltpu.VMEM((N_PAIRS, 1, PAIR), F32)],
        compiler_params=pltpu.CompilerParams(
            dimension_semantics=("parallel", "arbitrary"), vmem_limit_bytes=VMEM_LIMIT),
        name="wkv_pipe",
    )(*([proj] * 10), prm, w2)


HALO = 16


def _window_sums(ext):
    w2 = ext + pltpu.roll(ext, 1, 0)
    w4 = w2 + pltpu.roll(w2, 2, 0)
    w8 = w4 + pltpu.roll(w4, 4, 0)
    w16 = w8 + pltpu.roll(w8, 8, 0)
    return (w2, w4, w8, w16)


def _pool_gate(pooled_groups, gb, pw_ref, pscale):
    mixed = [_dot(pg.astype(BF16), pw_ref[g]) for g, pg in enumerate(pooled_groups)]
    yb = jnp.concatenate(mixed, axis=1) * pscale
    return yb * _silu(gb)


def _layer_norm_v(vc, ln_g):
    vm = jnp.mean(vc, axis=-1, keepdims=True)
    d = vc - vm
    vv = jnp.mean(d * d, axis=-1, keepdims=True)
    return d * lax.rsqrt(vv + LN_EPS) * ln_g


def _chunk_gate(vn, uc, gc, wm_ref, bm_ref):
    n_rows = vn.shape[0]
    vnb = vn.astype(BF16)
    outs = []
    for j in range(n_rows // CHUNK):
        rs = slice(j * CHUNK, (j + 1) * CHUNK)
        mix = [_dot(wm_ref[g], vnb[rs, g * GC:(g + 1) * GC]) + bm_ref[g] for g in range(N_GROUPS_C)]
        outs.append(jnp.concatenate(mix, axis=1))
    mix = outs[0] if len(outs) == 1 else jnp.concatenate(outs, axis=0)
    return uc * mix * _silu(gc)


def _bc_prompt_kernel(ub_ref, gb_ref, uc_ref, vc_ref, gc_ref, pw_ref, ps_ref, lng_ref, wm_ref, bm_ref,
                      o_ref, halo_ref, *, tt):
    j = pl.program_id(1)

    @pl.when(j == 0)
    def _():
        halo_ref[...] = jnp.zeros_like(halo_ref)

    u = ub_ref[...].astype(F32)
    ext = jnp.concatenate([halo_ref[...], u], axis=0)
    halo_ref[...] = u[tt - HALO:, :]
    sums = _window_sums(ext)
    pos = j * tt + lax.broadcasted_iota(jnp.int32, (tt, 1), 0)
    pooled = []
    for g, win in enumerate(POOL_WINDOWS):
        ls = slice(g * POOL_GC, (g + 1) * POOL_GC)
        cnt = jnp.minimum(pos + 1, win).astype(F32)
        pooled.append(sums[g][HALO:, ls] / cnt - u[:, ls])
    gb = gb_ref[...].astype(F32)
    o_ref[:, :D_B] = _pool_gate(pooled, gb, pw_ref, ps_ref[...]).astype(o_ref.dtype)
    vn = _layer_norm_v(vc_ref[...].astype(F32), lng_ref[...])
    uc = uc_ref[...].astype(F32)
    gc = gc_ref[...].astype(F32)
    o_ref[:, D_B:] = _chunk_gate(vn, uc, gc, wm_ref, bm_ref).astype(o_ref.dtype)


def _bc_prompt(proj, n_seq, seq_len, pw, pscale, ln_g, wm, bm, tt=512):
    n_t = seq_len // tt

    def rmap(col):
        return lambda i, j: (i * n_t + j, col)

    wspec = pl.BlockSpec((4, 128, 128), lambda i, j: (0, 0, 0))
    vspec = pl.BlockSpec((1, D_B), lambda i, j: (0, 0))
    return pl.pallas_call(
        functools.partial(_bc_prompt_kernel, tt=tt),
        grid=(n_seq, n_t),
        in_specs=[pl.BlockSpec((tt, D_B), rmap(P_UB // D_B)),
                  pl.BlockSpec((tt, D_B), rmap(P_GB // D_B)),
                  pl.BlockSpec((tt, D_B), rmap(P_UC // D_B)),
                  pl.BlockSpec((tt, D_B), rmap(P_VC // D_B)),
                  pl.BlockSpec((tt, D_B), rmap(P_GC // D_B)),
                  wspec, vspec, vspec, wspec, wspec],
        out_specs=pl.BlockSpec((tt, D_B + D_C), rmap(0)),
        out_shape=jax.ShapeDtypeStruct((n_seq * seq_len, D_B + D_C), BF16),
        scratch_shapes=[pltpu.VMEM((HALO, D_B), F32)],
        compiler_params=pltpu.CompilerParams(
            dimension_semantics=("parallel", "arbitrary"), vmem_limit_bytes=VMEM_LIMIT),
        name="bc_prompt",
    )(proj, proj, proj, proj, proj, pw, pscale, ln_g, wm, bm)


def _bc_sample_kernel(buf_ref, ub_ref, gb_ref, uc_ref, vc_ref, gc_ref, pw_ref, ps_ref, lng_ref,
                      wm_ref, bm_ref, o_ref, vn_ref, *, nb, t_len):
    u3 = ub_ref[...].astype(F32)
    ext = jnp.concatenate([buf_ref[...], u3], axis=1)
    per = HALO + t_len
    sums = _window_sums(ext.reshape(nb * per, D_B))
    u = u3.reshape(nb * t_len, D_B)
    pooled = []
    for g, win in enumerate(POOL_WINDOWS):
        ls = slice(g * POOL_GC, (g + 1) * POOL_GC)
        s3 = sums[g].reshape(nb, per, D_B)[:, HALO:, ls].reshape(nb * t_len, POOL_GC)
        cnt = float(min(PAST_LEN + 1, win))
        pooled.append(s3 / cnt - u[:, ls])
    rows = nb * t_len
    gb = gb_ref[...].astype(F32).reshape(rows, D_B)
    o_ref[:, :D_B] = _pool_gate(pooled, gb, pw_ref, ps_ref[...]).astype(o_ref.dtype)
    vn = _layer_norm_v(vc_ref[...].astype(F32).reshape(rows, D_C), lng_ref[...])
    vn_ref[...] = vn
    uc = uc_ref[...].astype(F32).reshape(rows, D_C)
    gc = gc_ref[...].astype(F32).reshape(rows, D_C)
    o_ref[:, D_B:] = _chunk_gate(vn, uc, gc, wm_ref, bm_ref).astype(o_ref.dtype)


def _bc_sample(proj3, seq0, n_seq, t_len, buf16, pw, pscale, ln_g, wm, bm):
    nb = CHUNK // t_len
    sb0 = seq0 // nb

    def rmap(col):
        return lambda i: (sb0 + i, 0, col)

    wspec = pl.BlockSpec((4, 128, 128), lambda i: (0, 0, 0))
    vspec = pl.BlockSpec((1, D_B), lambda i: (0, 0))
    rows = nb * t_len
    return pl.pallas_call(
        functools.partial(_bc_sample_kernel, nb=nb, t_len=t_len),
        grid=(n_seq // nb,),
        in_specs=[pl.BlockSpec((nb, HALO, D_B), lambda i: (i, 0, 0)),
                  pl.BlockSpec((nb, t_len, D_B), rmap(P_UB // D_B)),
                  pl.BlockSpec((nb, t_len, D_B), rmap(P_GB // D_B)),
                  pl.BlockSpec((nb, t_len, D_B), rmap(P_UC // D_B)),
                  pl.BlockSpec((nb, t_len, D_B), rmap(P_VC // D_B)),
                  pl.BlockSpec((nb, t_len, D_B), rmap(P_GC // D_B)),
                  wspec, vspec, vspec, wspec, wspec],
        out_specs=[pl.BlockSpec((rows, D_B + D_C), lambda i: (i, 0)),
                   pl.BlockSpec((rows, D_C), lambda i: (i, 0))],
        out_shape=[jax.ShapeDtypeStruct((n_seq * t_len, D_B + D_C), BF16),
                   jax.ShapeDtypeStruct((n_seq * t_len, D_C), F32)],
        compiler_params=pltpu.CompilerParams(
            dimension_semantics=("parallel",), vmem_limit_bytes=VMEM_LIMIT),
        name="bc_sample",
    )(buf16, proj3, proj3, proj3, proj3, proj3, pw, pscale, ln_g, wm, bm)


WPREP_ROWS = 256


def _wprep_kernel(w_ref, o_ref):
    o_ref[0, :, :3 * D_A] = w_ref[0, :, :3 * D_A].astype(BF16)
    o_ref[0, :, 3 * D_A:P_LORA] = w_ref[0, :, SHIFT_W:].astype(BF16)
    o_ref[0, :, P_LORA:] = w_ref[0, :, 3 * D_A:SHIFT_W].astype(BF16)


def _prep_w_in(w_in):
    spec = pl.BlockSpec((1, WPREP_ROWS, D_INP), lambda l, i: (l, i, 0))
    return pl.pallas_call(
        _wprep_kernel,
        grid=(DEPTH, D_MODEL // WPREP_ROWS),
        in_specs=[spec],
        out_specs=spec,
        out_shape=jax.ShapeDtypeStruct((DEPTH, D_MODEL, D_INP), BF16),
        compiler_params=pltpu.CompilerParams(
            dimension_semantics=("parallel", "parallel"), vmem_limit_bytes=VMEM_LIMIT),
        name="wprep",
    )(w_in)


def kernel(x_prompt, x_sample, state_shift, state_wkv, state_pool, norm_g, final_norm_g, w_in,
           shift_mu, w0, w_up, a0, a_up, k_k, k_a, r_k, lnx_g, lnx_b, pool_w, pool_scale,
           gmlp_ln_g, gmlp_ws, gmlp_b, w_out):
    bp, seq, _ = x_prompt.shape
    bs, dseq, _ = x_sample.shape
    n_p = bp * seq
    n_s = bs * dseq
    xp = x_prompt.reshape(n_p, D_MODEL)
    xs = x_sample.reshape(n_s, D_MODEL)
    hp = _norm_rows(xp, norm_g[0][None])
    hs = _norm_rows(xs, norm_g[0][None])

    w_in_p = _prep_w_in(w_in)
    w_out_h = w_out.astype(BF16)
    mu_l = jnp.pad(shift_mu[:, 3 * D_A:], ((0, 0), (0, D_A - 2 * LORA)))
    prm = jnp.stack([shift_mu[:, :D_A], shift_mu[:, D_A:2 * D_A], shift_mu[:, 2 * D_A:3 * D_A],
                     w0, a0, k_k, k_a, r_k.reshape(DEPTH, D_A), lnx_g, lnx_b, mu_l], axis=1)
    prm = jnp.pad(prm, ((0, 0), (0, 16 - prm.shape[1]), (0, 0)))
    w2 = jnp.concatenate([w_up, a_up], axis=1)
    pw = pool_w.astype(BF16)
    tril = jnp.tril(jnp.ones((CHUNK, CHUNK), F32))
    wm_p = (gmlp_ws * tril).astype(BF16)
    bm_p = jnp.broadcast_to(gmlp_b[:, :, :, None], (DEPTH, N_GROUPS_C, CHUNK, GC))
    nb_s = CHUNK // dseq
    eye_b = jnp.eye(nb_s, dtype=F32)
    ws_small = gmlp_ws[:, :, :dseq, :dseq] * tril[:dseq, :dseq]
    wm_s = jnp.einsum('ab,lgts->lgatbs', eye_b, ws_small).reshape(DEPTH, N_GROUPS_C, CHUNK, CHUNK)
    wm_s = wm_s.astype(BF16)
    bm_s = jnp.broadcast_to(jnp.tile(gmlp_b[:, :, :dseq], (1, 1, nb_s))[:, :, :, None],
                            (DEPTH, N_GROUPS_C, CHUNK, GC))

    ssh_main = state_shift[:, :, :3 * D_A].reshape(DEPTH, bs, 3, 1, D_A).transpose(0, 2, 1, 3, 4)
    ssh_lora = state_shift[:, :, 3 * D_A:].reshape(DEPTH, bs, 1, PAIR)
    wkv_s = state_wkv.reshape(DEPTH, bs, N_PAIRS, PAIR, HEAD_A)
    buf16 = jnp.pad(state_pool, ((0, 0), (0, 0), (HALO - POOL_BUF, 0), (0, 0)))

    p_shift, p_wkv, p_pool, s_shift, s_pool, s_v = [], [], [], [], [], []
    for l in range(DEPTH):
        final = l == DEPTH - 1
        g_out = final_norm_g[None] if final else norm_g[l + 1][None]
        bc_w = (pw[l], pool_scale[l][None], gmlp_ln_g[l][None])

        proj_p = _inproj_h(hp, w_in_p, l, 2048)
        ya_p, wk_p, shm_p, shl_p = _wkv_prompt(proj_p, bp, seq, prm, w2, l)
        cb_p = _bc_prompt(proj_p, bp, seq, *bc_w, wm_p[l], bm_p[l])
        xp = _outproj(ya_p, cb_p, w_out_h, l, xp, g_out, final)
        if not final:
            xp, hp = xp

        proj_s = _inproj_h(hs, w_in_p, l, 1024)
        ya_s, wkv_s, shm_s, shl_s = _wkv_sample(proj_s, bs, dseq, ROWS // dseq, ssh_main, ssh_lora,
                                                wkv_s, prm, w2, l)
        cb_s, vn_s = _bc_sample(proj_s.reshape(bs, dseq, D_INP), 0, bs, dseq, buf16[l], *bc_w,
                                wm_s[l], bm_s[l])
        xs = _outproj(ya_s, cb_s, w_out_h, l, xs, g_out, final)
        if not final:
            xs, hs = xs

        p_shift.append(jnp.concatenate([shm_p[0, :, 0], shm_p[1, :, 0], shm_p[2, :, 0], shl_p[:, 0]], axis=-1))
        s_shift.append(jnp.concatenate([shm_s[0, :, 0], shm_s[1, :, 0], shm_s[2, :, 0], shl_s[:, 0]], axis=-1))
        p_wkv.append(wk_p.reshape(bp, N_HEADS_A, HEAD_A, HEAD_A))
        p_pool.append(jnp.stack([proj_p[(b + 1) * seq - POOL_BUF:(b + 1) * seq, P_UB:P_UB + D_B]
                                 for b in range(bp)]).astype(F32))
        ub_s = proj_s[:, P_UB:P_UB + D_B].astype(F32).reshape(bs, dseq, D_B)
        s_pool.append(jnp.concatenate([state_pool[l], ub_s], axis=1)[:, -POOL_BUF:])
        s_v.append(vn_s.reshape(bs, dseq, D_C))

    y_prompt = xp.reshape(bp, seq, D_MODEL)
    y_sample = xs.reshape(bs, dseq, D_MODEL)
    s_wkv = wkv_s.reshape(DEPTH, bs, N_HEADS_A, HEAD_A, HEAD_A)
    return (y_prompt, y_sample, jnp.stack(p_shift), jnp.stack(p_wkv), jnp.stack(p_pool),
            jnp.stack(s_shift), s_wkv, jnp.stack(s_pool), jnp.stack(s_v))
```

```python
import functools

import jax
import jax.numpy as jnp
import numpy as np
from jax import lax
from jax.experimental import pallas as pl
from jax.experimental.pallas import tpu as pltpu

F32 = jnp.float32
BF16 = jnp.bfloat16

D_MODEL = 2048
DEPTH = 4
PAST_LEN = 16384
D_A = 1024
HEAD_A = 64
N_HEADS_A = 16
LORA = 64
D_B = 512
POOL_WINDOWS = (2, 4, 8, 16)
POOL_GC = 128
POOL_BUF = 15
D_C = 512
N_GROUPS_C = 4
GC = 128
CHUNK = 128
SHIFT_W = 3 * D_A + 2 * LORA
EPS = 1e-6
GN_EPS = HEAD_A * 1e-5
LN_EPS = 1e-5

P_R, P_K, P_V, P_GA = 0, 1024, 2048, 3072
P_UB, P_GB, P_UC, P_VC, P_GC = 4096, 4608, 5120, 5632, 6144
P_LORA = 6656
D_INP = 6784

PAIR = 128
N_PAIRS = D_A // PAIR
ROWS = 64

PROJ_DTYPE = BF16
VMEM_LIMIT = 52 * 1024 * 1024


def _dot(a, b):
    return jnp.dot(a, b, preferred_element_type=F32)


def _split(x, n):
    pieces = []
    rem = x
    for i in range(n):
        hi = rem.astype(BF16)
        pieces.append(hi)
        if i + 1 < n:
            rem = rem - hi.astype(F32)
    return pieces


_CONTRACT = {"nn": (1, 0), "nt": (1, 1), "tn": (0, 0)}


def _mm(a, b, mode="nn", pa=1, pb=1):
    ca, cb = _CONTRACT[mode]
    sa, sb = _split(a, pa), _split(b, pb)
    terms = [(i, j) for i in range(pa) for j in range(pb) if i + j < max(pa, pb)]
    lhs = jnp.concatenate([sa[i] for i, _ in terms], axis=ca) if len(terms) > 1 else sa[0]
    rhs = jnp.concatenate([sb[j] for _, j in terms], axis=cb) if len(terms) > 1 else sb[0]
    return lax.dot_general(lhs, rhs, (((ca,), (cb,)), ((), ())), preferred_element_type=F32)


PREC_LORA = (1, 1)
PREC_CUM = (1, 2)
PREC_SEG = (1, 1)
PREC_G = (1, 1)
PREC_INV = (1, 1)
PREC_X2 = (1, 1)
PREC_WU = (1, 1)
PREC_US = (1, 1)
PREC_UPD = (1, 1)
PREC_Y = (1, 1)


def _norm_kernel(x_ref, g_ref, o_ref):
    x = x_ref[...]
    ms = jnp.mean(x * x, axis=-1, keepdims=True)
    o_ref[...] = ((x * lax.rsqrt(ms + EPS)) * g_ref[...]).astype(BF16)


def _norm_rows(x, g, tm=512):
    m = x.shape[0]
    return pl.pallas_call(
        _norm_kernel,
        grid=(m // tm,),
        in_specs=[pl.BlockSpec((tm, D_MODEL), lambda i: (i, 0)),
                  pl.BlockSpec((1, D_MODEL), lambda i: (0, 0))],
        out_specs=pl.BlockSpec((tm, D_MODEL), lambda i: (i, 0)),
        out_shape=jax.ShapeDtypeStruct((m, D_MODEL), BF16),
        compiler_params=pltpu.CompilerParams(
            dimension_semantics=("parallel",), vmem_limit_bytes=VMEM_LIMIT),
        name="norm_rows",
    )(x, g)


def _inproj_h_kernel(h_ref, w_ref, o_ref):
    o_ref[...] = _dot(h_ref[...], w_ref[0]).astype(o_ref.dtype)


def _inproj_h(h, w, layer, tm, tn=768):
    m = h.shape[0]
    return pl.pallas_call(
        _inproj_h_kernel,
        grid=(m // tm, pl.cdiv(D_INP, tn)),
        in_specs=[pl.BlockSpec((tm, D_MODEL), lambda i, j: (i, 0)),
                  pl.BlockSpec((1, D_MODEL, tn), lambda i, j: (layer, 0, j))],
        out_specs=pl.BlockSpec((tm, tn), lambda i, j: (i, j)),
        out_shape=jax.ShapeDtypeStruct((m, D_INP), PROJ_DTYPE),
        compiler_params=pltpu.CompilerParams(
            dimension_semantics=("parallel", "arbitrary"), vmem_limit_bytes=VMEM_LIMIT),
        name="inproj_h",
    )(h, w)


def _outproj_kernel(ca_ref, cb_ref, wa_ref, wb_ref, x_ref, g_ref, *o_refs, final):
    y = _dot(ca_ref[...], wa_ref[0]) + _dot(cb_ref[...], wb_ref[0])
    out = x_ref[...] + y
    ms = jnp.mean(out * out, axis=-1, keepdims=True)
    normed = (out * lax.rsqrt(ms + EPS)) * g_ref[...]
    if final:
        o_refs[0][...] = normed
    else:
        o_refs[0][...] = out
        o_refs[1][...] = normed.astype(BF16)


def _outproj(cat_a, cat_b, w, layer, x, g, final, tm=512):
    m = x.shape[0]
    half = D_MODEL // 2
    row_spec = pl.BlockSpec((tm, D_MODEL), lambda i: (i, 0))
    f32_out = jax.ShapeDtypeStruct((m, D_MODEL), F32)
    return pl.pallas_call(
        functools.partial(_outproj_kernel, final=final),
        grid=(m // tm,),
        in_specs=[pl.BlockSpec((tm, half), lambda i: (i, 0)),
                  pl.BlockSpec((tm, half), lambda i: (i, 0)),
                  pl.BlockSpec((1, half, D_MODEL), lambda i: (layer, 0, 0), pipeline_mode=pl.Buffered(1)),
                  pl.BlockSpec((1, half, D_MODEL), lambda i: (layer, 1, 0), pipeline_mode=pl.Buffered(1)),
                  pl.BlockSpec((tm, D_MODEL), lambda i: (i, 0)),
                  pl.BlockSpec((1, D_MODEL), lambda i: (0, 0))],
        out_specs=row_spec if final else [row_spec, row_spec],
        out_shape=f32_out if final else [f32_out, jax.ShapeDtypeStruct((m, D_MODEL), BF16)],
        compiler_params=pltpu.CompilerParams(
            dimension_semantics=("parallel",), vmem_limit_bytes=VMEM_LIMIT),
        name="outproj",
    )(cat_a, cat_b, w, w, x, g)


PR_MU_R, PR_MU_K, PR_MU_V, PR_W0, PR_A0, PR_KK, PR_KA, PR_RK, PR_LNG, PR_LNB, PR_MU_L = range(11)


EXP_M_HALF = float(np.exp(-0.5))


NEG_LOG2E = -float(np.log2(np.e))


def _exp_neg(z):
    return jnp.exp2(z * NEG_LOG2E)


def _sigmoid(z):
    return 1.0 / (1.0 + _exp_neg(z))


def _silu(z):
    return z * _sigmoid(z)


def _run(gen):
    try:
        while True:
            next(gen)
    except StopIteration as e:
        return e.value


def _zip_run(main, prep, pattern):
    done = {}

    def step(gen, key):
        if key not in done:
            try:
                next(gen)
            except StopIteration as e:
                done[key] = e.value

    for ch in pattern:
        step(main if ch == "m" else prep, ch)
    while "m" not in done:
        step(main, "m")
    while "p" not in done:
        step(prep, "p")
    return done["m"], done["p"]


class _ScanConsts:
    def __init__(self, C, NB):
        rows = ROWS
        self.C, self.NB = C, NB
        row_id = lax.broadcasted_iota(jnp.int32, (rows, 1), 0)
        self.first = (row_id % C) == 0
        self.lo_half = lax.broadcasted_iota(jnp.int32, (rows, PAIR), 1) < HEAD_A
        ri = lax.broadcasted_iota(jnp.int32, (rows, rows), 0)
        ci = lax.broadcasted_iota(jnp.int32, (rows, rows), 1)
        self.tri = jnp.where(((ri // C) == (ci // C)) & (ri >= ci), 1.0, 0.0)
        pr = lax.broadcasted_iota(jnp.int32, (PAIR, PAIR), 0)
        pc = lax.broadcasted_iota(jnp.int32, (PAIR, PAIR), 1)
        self.same_head = (pr // HEAD_A) == (pc // HEAD_A)
        self.seg = jnp.where(self.same_head, 1.0, 0.0)
        rp = lax.broadcasted_iota(jnp.int32, (rows, PAIR), 0)
        cp = lax.broadcasted_iota(jnp.int32, (rows, PAIR), 1) % rows
        same_seq = (rp // C) == (cp // C)
        self.m_strict = same_seq & (rp > cp)
        m_incl = same_seq & (rp >= cp)
        self.m_incl2 = jnp.concatenate([m_incl, m_incl], axis=1)
        self.eye = jnp.where(rp == cp, 1.0, 0.0)


def _bmask(m, xb):
    return jnp.where(m, xb, jnp.zeros_like(xb))


_PP_SIDE = ("at", "rt", "kg", "bg", "vb")
_PP_STACK = ("ats", "khs", "bhs", "vs")
_PP_F32 = ("bon", "v", "sg")


def _stack(xb, lo_half):
    zb = jnp.zeros_like(xb)
    return jnp.concatenate([jnp.where(lo_half, xb, zb), jnp.where(lo_half, zb, xb)], axis=0)


def _prep_steps(raw, prev, prm, w2, cs):
    C, NB, rows = cs.C, cs.NB, ROWS
    r_raw, k_raw, v_raw, g_raw, l_raw = raw
    prev_r, prev_k, prev_v, prev_l = prev

    def prow(i):
        return prm[i:i + 1, :]

    def shift_mix(x, prev_rows, mu):
        rolled = pltpu.roll(x, 1, 0)
        parts = []
        for b, q in enumerate(prev_rows):
            head = jnp.where(cs.first[:8], jnp.broadcast_to(q, (8, x.shape[1])), rolled[b * C:b * C + 8])
            parts += [head] if C == 8 else [head, rolled[b * C + 8:(b + 1) * C]]
        shifted = jnp.concatenate(parts, axis=0)
        return x + (shifted - x) * mu

    xl = shift_mix(l_raw, prev_l, prow(PR_MU_L)[:, :PAIR])
    lora_w = _mm(jnp.where(cs.lo_half, jnp.tanh(xl), 0.0), w2, "nn", *PREC_LORA)
    lora_a = _mm(jnp.where(cs.lo_half, 0.0, xl), w2, "nn", *PREC_LORA)
    yield
    r = shift_mix(r_raw, prev_r, prow(PR_MU_R))
    k = shift_mix(k_raw, prev_k, prow(PR_MU_K))
    v = shift_mix(v_raw, prev_v, prow(PR_MU_V))
    dec = EXP_M_HALF * _sigmoid(prow(PR_W0) + lora_w)
    a = _sigmoid(prow(PR_A0) + lora_a)
    cum = _mm(cs.tri, -dec, "nn", *PREC_CUM)
    yield
    kk_raw = k * prow(PR_KK)
    k2 = k * (1.0 + (a - 1.0) * prow(PR_KA))
    rkr = r * k2 * prow(PR_RK)
    sls = [slice(p * PAIR, (p + 1) * PAIR) for p in range(N_PAIRS)]
    s0 = [_mm(jnp.concatenate([kk_raw[:, sl] * kk_raw[:, sl], rkr[:, sl]], axis=0), cs.seg, "nn", *PREC_SEG)
          for sl in sls]
    yield
    g_incl = jnp.exp(cum)
    g_excl = jnp.exp(cum + dec)
    g_inv = _exp_neg(cum)
    g_tot_rows = [jnp.exp(cum[(b + 1) * C - 1:(b + 1) * C, :]) for b in range(NB)]
    g_rest = g_inv * jnp.concatenate([jnp.broadcast_to(q, (C, D_A)) for q in g_tot_rows], axis=0)
    sg = _silu(g_raw)

    pp = {n: [] for n in _PP_SIDE + _PP_STACK + _PP_F32 + ("gtot",)}
    for sl, q in zip(sls, s0):
        kkp = kk_raw[:, sl] * lax.rsqrt(jnp.maximum(q[:rows], 1e-12))
        kka = kkp * a[:, sl]
        at = (-kkp * g_excl[:, sl]).astype(BF16)
        vb = v[:, sl].astype(BF16)
        pp["at"].append(at)
        pp["rt"].append((r[:, sl] * g_incl[:, sl]).astype(BF16))
        pp["kg"].append((k2[:, sl] * g_rest[:, sl]).astype(BF16))
        pp["bg"].append((kka * g_rest[:, sl]).astype(BF16))
        pp["vb"].append(vb)
        pp["ats"].append(_stack(at, cs.lo_half))
        pp["khs"].append(_stack((k2[:, sl] * g_inv[:, sl]).astype(BF16), cs.lo_half))
        pp["bhs"].append(_stack((kka * g_inv[:, sl]).astype(BF16), cs.lo_half))
        pp["vs"].append(_stack(vb, cs.lo_half))
        pp["bon"].append(q[rows:])
        pp["v"].append(v[:, sl])
        pp["sg"].append(sg[:, sl])
        pp["gtot"].append(jnp.concatenate([q[:, sl] for q in g_tot_rows], axis=0))
    return pp


def _scan_steps(pp, st_ref, o_ref, row0, prm, cs):
    C, NB, rows = cs.C, cs.NB, ROWS
    n_rounds = int(np.log2(C))
    pairs = range(N_PAIRS)

    def prow(i):
        return prm[i:i + 1, :]

    def seq_rows(x, b):
        if NB == 1:
            return x
        return x.astype(F32)[b * C:(b + 1) * C]

    def unseq_rows(pieces):
        return pieces[0] if NB == 1 else jnp.concatenate(pieces, axis=0)

    a_ak, a_r, pwb, t_inv = [], [], [], []
    for p in pairs:
        gm = _mm(jnp.concatenate([pp["at"][p], pp["rt"][p]], axis=0),
                 jnp.concatenate([pp["khs"][p], pp["bhs"][p]], axis=0), "nt", *PREC_G)
        gmb = gm.astype(BF16)
        a_ak.append(_bmask(cs.m_strict, gmb[:rows, :PAIR]))
        a_r.append(_bmask(cs.m_incl2, gmb[rows:, :]))
        pwb.append(_bmask(cs.m_strict, gmb[:rows, PAIR:]))
        t_inv.append(cs.eye + jnp.where(cs.m_strict, gm[:rows, PAIR:], 0.0))
    yield

    x2 = []
    for p in pairs:
        x2.append(_mm(a_ak[p], pp["vs"][p], "nn", *PREC_X2).astype(BF16))
        pwb[p] = _mm(pwb[p], _stack(pwb[p], cs.lo_half), "nn", *PREC_INV).astype(BF16)
    yield
    for rd in range(1, n_rounds):
        for p in pairs:
            tb = t_inv[p].astype(BF16)
            pws = _stack(pwb[p], cs.lo_half)
            if rd < n_rounds - 1:
                res = _mm(jnp.concatenate([pwb[p], tb], axis=0), pws, "nn", *PREC_INV)
                t_inv[p] = t_inv[p] + res[rows:]
                pwb[p] = res[:rows].astype(BF16)
            else:
                t_inv[p] = t_inv[p] + _mm(tb, pws, "nn", *PREC_INV)
        yield

    w_t, u_t = [], []
    for p in pairs:
        wu = _mm(t_inv[p], jnp.concatenate([pp["ats"][p], _stack(x2[p], cs.lo_half)], axis=1),
                 "nn", *PREC_WU)
        w_t.append(wu[:, :PAIR].astype(BF16))
        u_t.append(wu[:, PAIR:])
    yield

    s_old = [[st_ref[p * NB + b] for b in range(NB)] for p in pairs]
    u_b, rs = [], []
    for p in pairs:
        us = [_mm(jnp.concatenate([seq_rows(w_t[p], b), seq_rows(pp["rt"][p], b)], axis=0),
                  s_old[p][b], "nt", *PREC_US) for b in range(NB)]
        u_b.append([us[b][:C] + seq_rows(u_t[p], b) for b in range(NB)])
        rs.append(unseq_rows([us[b][C:] for b in range(NB)]))
    yield
    for p in pairs:
        for b in range(NB):
            upd = _mm(jnp.concatenate([seq_rows(pp["vb"][p], b), u_b[p][b]], axis=0),
                      jnp.concatenate([seq_rows(pp["kg"][p], b), seq_rows(pp["bg"][p], b)], axis=0),
                      "tn", *PREC_UPD)
            st_ref[p * NB + b] = s_old[p][b] * pp["gtot"][p][b:b + 1, :] + jnp.where(cs.same_head, upd, 0.0)
    yield
    y = []
    for p in pairs:
        u_s = _stack(unseq_rows(u_b[p]).astype(BF16), cs.lo_half)
        y.append(rs[p] + _mm(a_r[p], jnp.concatenate([pp["vs"][p], u_s], axis=0), "nn", *PREC_Y))
    yield
    yc = []
    for p in pairs:
        yc.append(y[p] - _mm(y[p], cs.seg, "nn", *PREC_SEG) * (1.0 / HEAD_A))
    yield
    for p in pairs:
        sl = slice(p * PAIR, (p + 1) * PAIR)
        yv = _mm(yc[p] * yc[p], cs.seg, "nn", *PREC_SEG) * (1.0 / HEAD_A)
        yn = yc[p] * lax.rsqrt(yv + GN_EPS) * prow(PR_LNG)[:, sl] + prow(PR_LNB)[:, sl]
        o_ref[row0:row0 + rows, sl] = ((yn + pp["bon"][p] * pp["v"][p]) * pp["sg"][p]).astype(o_ref.dtype)


def _write_state(st_ref, sout_ref, cs, b0=0):
    for b in range(cs.NB):
        for p in range(N_PAIRS):
            sm = jnp.where(cs.same_head, st_ref[p * cs.NB + b], 0.0)
            sout_ref[0, b0 + b, p] = sm[:, :HEAD_A] + sm[:, HEAD_A:]


_PIPE_PATTERN = "pmmpmmmmpmmp"


def _wkv_kernel(r_ref, k_ref, v_ref, g_ref, l_ref, sh_ref, shl_ref, s0_ref, prm_ref, w2_ref,
                o_ref, sout_ref, shm_out_ref, shl_out_ref, st_ref, *, C, NB, NBLK):
    cs = _ScanConsts(C, NB)
    prm = prm_ref[0]
    w2 = w2_ref[0]
    refs = (r_ref, k_ref, v_ref, g_ref, l_ref)
    for q in range(NBLK * NB):
        for p in range(N_PAIRS):
            s = s0_ref[0, q, p]
            s2 = jnp.concatenate([s, s], axis=1)
            st_ref[q // NB, p * NB + q % NB] = jnp.where(cs.same_head, s2, 0.0)

    def prep(blk):
        raw = tuple(q[blk * ROWS:(blk + 1) * ROWS, :].astype(F32) for q in refs)
        seqs = range(blk * NB, (blk + 1) * NB)
        prev = ([sh_ref[0, 0, q] for q in seqs], [sh_ref[0, 1, q] for q in seqs],
                [sh_ref[0, 2, q] for q in seqs], [shl_ref[0, q] for q in seqs])
        return _prep_steps(raw, prev, prm, w2, cs)

    pp = _run(prep(0))
    for blk in range(NBLK):
        scan = _scan_steps(pp, st_ref.at[blk], o_ref, blk * ROWS, prm, cs)
        if blk + 1 < NBLK:
            _, pp = _zip_run(scan, prep(blk + 1), _PIPE_PATTERN)
        else:
            _run(scan)

    for q in range(NBLK * NB):
        last = (q + 1) * C - 1
        shm_out_ref[0, q] = r_ref[last:last + 1, :].astype(F32)
        shm_out_ref[1, q] = k_ref[last:last + 1, :].astype(F32)
        shm_out_ref[2, q] = v_ref[last:last + 1, :].astype(F32)
        shl_out_ref[q] = l_ref[last:last + 1, :].astype(F32)
    for blk in range(NBLK):
        _write_state(st_ref.at[blk], sout_ref, cs, blk * NB)


def _wkv_sample(proj, n_seq, C, NB, sh_main, sh_lora, s0, prm, w2, layer, nblk=2):
    assert NB * C == ROWS
    nsq = NB * nblk
    rows = ROWS * nblk

    def rmap(col):
        return lambda i: (i, col)

    state_spec = pl.BlockSpec((1, nsq, N_PAIRS, PAIR, HEAD_A), lambda i: (layer, i, 0, 0, 0))
    return pl.pallas_call(
        functools.partial(_wkv_kernel, C=C, NB=NB, NBLK=nblk),
        grid=(n_seq // nsq,),
        in_specs=[pl.BlockSpec((rows, D_A), rmap(P_R // D_A)),
                  pl.BlockSpec((rows, D_A), rmap(P_K // D_A)),
                  pl.BlockSpec((rows, D_A), rmap(P_V // D_A)),
                  pl.BlockSpec((rows, D_A), rmap(P_GA // D_A)),
                  pl.BlockSpec((rows, PAIR), rmap(P_LORA // PAIR)),
                  pl.BlockSpec((1, 3, nsq, 1, D_A), lambda i: (layer, 0, i, 0, 0)),
                  pl.BlockSpec((1, nsq, 1, PAIR), lambda i: (layer, i, 0, 0)),
                  state_spec,
                  pl.BlockSpec((1, 16, D_A), lambda i: (layer, 0, 0)),
                  pl.BlockSpec((1, PAIR, D_A), lambda i: (layer, 0, 0))],
        out_specs=[pl.BlockSpec((rows, D_A), lambda i: (i, 0)),
                   state_spec,
                   pl.BlockSpec((3, nsq, 1, D_A), lambda i: (0, i, 0, 0)),
                   pl.BlockSpec((nsq, 1, PAIR), lambda i: (i, 0, 0))],
        out_shape=[jax.ShapeDtypeStruct((n_seq * C, D_A), BF16),
                   jax.ShapeDtypeStruct(s0.shape, F32),
                   jax.ShapeDtypeStruct((3, n_seq, 1, D_A), F32),
                   jax.ShapeDtypeStruct((n_seq, 1, PAIR), F32)],
        scratch_shapes=[pltpu.VMEM((nblk, N_PAIRS * NB, PAIR, PAIR), F32)],
        input_output_aliases={7: 1},
        compiler_params=pltpu.CompilerParams(
            dimension_semantics=("parallel",), vmem_limit_bytes=VMEM_LIMIT),
        name="wkv_c%d" % C,
    )(proj, proj, proj, proj, proj, sh_main, sh_lora, s0, prm, w2)


def _wkv_pipe_kernel(rc_ref, kc_ref, vc_ref, gc_ref, lc_ref, rn_ref, kn_ref, vn_ref, gn_ref, ln_ref,
                     prm_ref, w2_ref, o_ref, sout_ref, shm_out_ref, shl_out_ref,
                     st_ref, pa_ref, pb_ref, pf_ref, pg_ref, *, n_steps):
    k = pl.program_id(1)
    rows = ROWS
    cs = _ScanConsts(rows, 1)
    prm = prm_ref[0]
    w2 = w2_ref[0]

    def raw_rows(refs, lo):
        return tuple(q[lo:lo + rows, :].astype(F32) for q in refs)

    def prev_rows(refs, row):
        rr, kr, vr, _, lr = refs
        return tuple([q[row:row + 1, :].astype(F32)] for q in (rr, kr, vr, lr))

    def store_pp(pp):
        for p in range(N_PAIRS):
            for j, n in enumerate(_PP_SIDE):
                pa_ref[p, j] = pp[n][p]
            for j, n in enumerate(_PP_STACK):
                pb_ref[p, j] = pp[n][p]
            for j, n in enumerate(_PP_F32):
                pf_ref[p, j] = pp[n][p]
            pg_ref[p] = pp["gtot"][p]

    def load_pp():
        pp = {n: [pa_ref[p, j] for p in range(N_PAIRS)] for j, n in enumerate(_PP_SIDE)}
        pp.update({n: [pb_ref[p, j] for p in range(N_PAIRS)] for j, n in enumerate(_PP_STACK)})
        pp.update({n: [pf_ref[p, j] for p in range(N_PAIRS)] for j, n in enumerate(_PP_F32)})
        pp["gtot"] = [pg_ref[p] for p in range(N_PAIRS)]
        return pp

    cur = (rc_ref, kc_ref, vc_ref, gc_ref, lc_ref)
    nxt = (rn_ref, kn_ref, vn_ref, gn_ref, ln_ref)

    @pl.when(k == 0)
    def _():
        st_ref[...] = jnp.zeros_like(st_ref)
        zero = ([jnp.zeros((1, D_A), F32)], [jnp.zeros((1, D_A), F32)], [jnp.zeros((1, D_A), F32)],
                [jnp.zeros((1, PAIR), F32)])
        store_pp(_run(_prep_steps(raw_rows(cur, 0), zero, prm, w2, cs)))

    pp_a = load_pp()
    _, pp_b = _zip_run(_scan_steps(pp_a, st_ref, o_ref, 0, prm, cs),
                       _prep_steps(raw_rows(cur, rows), prev_rows(cur, rows - 1), prm, w2, cs),
                       _PIPE_PATTERN)
    _, pp_n = _zip_run(_scan_steps(pp_b, st_ref, o_ref, rows, prm, cs),
                       _prep_steps(raw_rows(nxt, 0), prev_rows(cur, 2 * rows - 1), prm, w2, cs),
                       _PIPE_PATTERN)
    store_pp(pp_n)

    last = 2 * rows - 1
    shm_out_ref[0, 0] = rc_ref[last:last + 1, :].astype(F32)
    shm_out_ref[1, 0] = kc_ref[last:last + 1, :].astype(F32)
    shm_out_ref[2, 0] = vc_ref[last:last + 1, :].astype(F32)
    shl_out_ref[0] = lc_ref[last:last + 1, :].astype(F32)

    @pl.when(k == n_steps - 1)
    def _():
        _write_state(st_ref, sout_ref, cs)


def _wkv_prompt(proj, n_seq, seq_len, prm, w2, layer):
    n_steps = seq_len // (2 * ROWS)
    n_chunks = seq_len // ROWS

    def cmap(col):
        return lambda i, k: (i * n_steps + k, col)

    def nmap(col):
        return lambda i, k: (i * n_chunks + jnp.minimum(2 * k + 2, n_chunks - 1), col)

    def specs(rows_, m):
        return [pl.BlockSpec((rows_, D_A), m(P_R // D_A)),
                pl.BlockSpec((rows_, D_A), m(P_K // D_A)),
                pl.BlockSpec((rows_, D_A), m(P_V // D_A)),
                pl.BlockSpec((rows_, D_A), m(P_GA // D_A)),
                pl.BlockSpec((rows_, PAIR), m(P_LORA // PAIR))]

    return pl.pallas_call(
        functools.partial(_wkv_pipe_kernel, n_steps=n_steps),
        grid=(n_seq, n_steps),
        in_specs=specs(2 * ROWS, cmap) + specs(ROWS, nmap) + [
            pl.BlockSpec((1, 16, D_A), lambda i, k: (layer, 0, 0)),
            pl.BlockSpec((1, PAIR, D_A), lambda i, k: (layer, 0, 0))],
        out_specs=[pl.BlockSpec((2 * ROWS, D_A), lambda i, k: (i * n_steps + k, 0)),
                   pl.BlockSpec((1, 1, N_PAIRS, PAIR, HEAD_A), lambda i, k: (0, i, 0, 0, 0)),
                   pl.BlockSpec((3, 1, 1, D_A), lambda i, k: (0, i, 0, 0)),
                   pl.BlockSpec((1, 1, PAIR), lambda i, k: (i, 0, 0))],
        out_shape=[jax.ShapeDtypeStruct((n_seq * seq_len, D_A), BF16),
                   jax.ShapeDtypeStruct((1, n_seq, N_PAIRS, PAIR, HEAD_A), F32),
                   jax.ShapeDtypeStruct((3, n_seq, 1, D_A), F32),
                   jax.ShapeDtypeStruct((n_seq, 1, PAIR), F32)],
        scratch_shapes=[pltpu.VMEM((N_PAIRS, PAIR, PAIR), F32),
                        pltpu.VMEM((N_PAIRS, len(_PP_SIDE), ROWS, PAIR), BF16),
                        pltpu.VMEM((N_PAIRS, len(_PP_STACK), 2 * ROWS, PAIR), BF16),
                        pltpu.VMEM((N_PAIRS, len(_PP_F32), ROWS, PAIR), F32),
                        pltpu.VMEM((N_PAIRS, 1, PAIR), F32)],
        compiler_params=pltpu.CompilerParams(
            dimension_semantics=("parallel", "arbitrary"), vmem_limit_bytes=VMEM_LIMIT),
        name="wkv_pipe",
    )(*([proj] * 10), prm, w2)


HALO = 16


def _window_sums(ext):
    w2 = ext + pltpu.roll(ext, 1, 0)
    w4 = w2 + pltpu.roll(w2, 2, 0)
    w8 = w4 + pltpu.roll(w4, 4, 0)
    w16 = w8 + pltpu.roll(w8, 8, 0)
    return (w2, w4, w8, w16)


def _pool_gate(pooled_groups, gb, pw_ref, pscale):
    mixed = [_dot(pg.astype(BF16), pw_ref[g]) for g, pg in enumerate(pooled_groups)]
    yb = jnp.concatenate(mixed, axis=1) * pscale
    return yb * _silu(gb)


def _layer_norm_v(vc, ln_g):
    vm = jnp.mean(vc, axis=-1, keepdims=True)
    d = vc - vm
    vv = jnp.mean(d * d, axis=-1, keepdims=True)
    return d * lax.rsqrt(vv + LN_EPS) * ln_g


def _chunk_gate(vn, uc, gc, wm_ref, bm_ref):
    n_rows = vn.shape[0]
    vnb = vn.astype(BF16)
    outs = []
    for j in range(n_rows // CHUNK):
        rs = slice(j * CHUNK, (j + 1) * CHUNK)
        mix = [_dot(wm_ref[g], vnb[rs, g * GC:(g + 1) * GC]) + bm_ref[g] for g in range(N_GROUPS_C)]
        outs.append(jnp.concatenate(mix, axis=1))
    mix = outs[0] if len(outs) == 1 else jnp.concatenate(outs, axis=0)
    return uc * mix * _silu(gc)


def _bc_prompt_kernel(ub_ref, gb_ref, uc_ref, vc_ref, gc_ref, pw_ref, ps_ref, lng_ref, wm_ref, bm_ref,
                      o_ref, halo_ref, *, tt):
    j = pl.program_id(1)

    @pl.when(j == 0)
    def _():
        halo_ref[...] = jnp.zeros_like(halo_ref)

    u = ub_ref[...].astype(F32)
    ext = jnp.concatenate([halo_ref[...], u], axis=0)
    halo_ref[...] = u[tt - HALO:, :]
    sums = _window_sums(ext)
    pos = j * tt + lax.broadcasted_iota(jnp.int32, (tt, 1), 0)
    pooled = []
    for g, win in enumerate(POOL_WINDOWS):
        ls = slice(g * POOL_GC, (g + 1) * POOL_GC)
        cnt = jnp.minimum(pos + 1, win).astype(F32)
        pooled.append(sums[g][HALO:, ls] / cnt - u[:, ls])
    gb = gb_ref[...].astype(F32)
    o_ref[:, :D_B] = _pool_gate(pooled, gb, pw_ref, ps_ref[...]).astype(o_ref.dtype)
    vn = _layer_norm_v(vc_ref[...].astype(F32), lng_ref[...])
    uc = uc_ref[...].astype(F32)
    gc = gc_ref[...].astype(F32)
    o_ref[:, D_B:] = _chunk_gate(vn, uc, gc, wm_ref, bm_ref).astype(o_ref.dtype)


def _bc_prompt(proj, n_seq, seq_len, pw, pscale, ln_g, wm, bm, tt=512):
    n_t = seq_len // tt

    def rmap(col):
        return lambda i, j: (i * n_t + j, col)

    wspec = pl.BlockSpec((4, 128, 128), lambda i, j: (0, 0, 0))
    vspec = pl.BlockSpec((1, D_B), lambda i, j: (0, 0))
    return pl.pallas_call(
        functools.partial(_bc_prompt_kernel, tt=tt),
        grid=(n_seq, n_t),
        in_specs=[pl.BlockSpec((tt, D_B), rmap(P_UB // D_B)),
                  pl.BlockSpec((tt, D_B), rmap(P_GB // D_B)),
                  pl.BlockSpec((tt, D_B), rmap(P_UC // D_B)),
                  pl.BlockSpec((tt, D_B), rmap(P_VC // D_B)),
                  pl.BlockSpec((tt, D_B), rmap(P_GC // D_B)),
                  wspec, vspec, vspec, wspec, wspec],
        out_specs=pl.BlockSpec((tt, D_B + D_C), rmap(0)),
        out_shape=jax.ShapeDtypeStruct((n_seq * seq_len, D_B + D_C), BF16),
        scratch_shapes=[pltpu.VMEM((HALO, D_B), F32)],
        compiler_params=pltpu.CompilerParams(
            dimension_semantics=("parallel", "arbitrary"), vmem_limit_bytes=VMEM_LIMIT),
        name="bc_prompt",
    )(proj, proj, proj, proj, proj, pw, pscale, ln_g, wm, bm)


def _bc_sample_kernel(buf_ref, ub_ref, gb_ref, uc_ref, vc_ref, gc_ref, pw_ref, ps_ref, lng_ref,
                      wm_ref, bm_ref, o_ref, vn_ref, *, nb, t_len):
    u3 = ub_ref[...].astype(F32)
    ext = jnp.concatenate([buf_ref[...], u3], axis=1)
    per = HALO + t_len
    sums = _window_sums(ext.reshape(nb * per, D_B))
    u = u3.reshape(nb * t_len, D_B)
    pooled = []
    for g, win in enumerate(POOL_WINDOWS):
        ls = slice(g * POOL_GC, (g + 1) * POOL_GC)
        s3 = sums[g].reshape(nb, per, D_B)[:, HALO:, ls].reshape(nb * t_len, POOL_GC)
        cnt = float(min(PAST_LEN + 1, win))
        pooled.append(s3 / cnt - u[:, ls])
    rows = nb * t_len
    gb = gb_ref[...].astype(F32).reshape(rows, D_B)
    o_ref[:, :D_B] = _pool_gate(pooled, gb, pw_ref, ps_ref[...]).astype(o_ref.dtype)
    vn = _layer_norm_v(vc_ref[...].astype(F32).reshape(rows, D_C), lng_ref[...])
    vn_ref[...] = vn
    uc = uc_ref[...].astype(F32).reshape(rows, D_C)
    gc = gc_ref[...].astype(F32).reshape(rows, D_C)
    o_ref[:, D_B:] = _chunk_gate(vn, uc, gc, wm_ref, bm_ref).astype(o_ref.dtype)


def _bc_sample(proj3, seq0, n_seq, t_len, buf16, pw, pscale, ln_g, wm, bm):
    nb = CHUNK // t_len
    sb0 = seq0 // nb

    def rmap(col):
        return lambda i: (sb0 + i, 0, col)

    wspec = pl.BlockSpec((4, 128, 128), lambda i: (0, 0, 0))
    vspec = pl.BlockSpec((1, D_B), lambda i: (0, 0))
    rows = nb * t_len
    return pl.pallas_call(
        functools.partial(_bc_sample_kernel, nb=nb, t_len=t_len),
        grid=(n_seq // nb,),
        in_specs=[pl.BlockSpec((nb, HALO, D_B), lambda i: (i, 0, 0)),
                  pl.BlockSpec((nb, t_len, D_B), rmap(P_UB // D_B)),
                  pl.BlockSpec((nb, t_len, D_B), rmap(P_GB // D_B)),
                  pl.BlockSpec((nb, t_len, D_B), rmap(P_UC // D_B)),
                  pl.BlockSpec((nb, t_len, D_B), rmap(P_VC // D_B)),
                  pl.BlockSpec((nb, t_len, D_B), rmap(P_GC // D_B)),
                  wspec, vspec, vspec, wspec, wspec],
        out_specs=[pl.BlockSpec((rows, D_B + D_C), lambda i: (i, 0)),
                   pl.BlockSpec((rows, D_C), lambda i: (i, 0))],
        out_shape=[jax.ShapeDtypeStruct((n_seq * t_len, D_B + D_C), BF16),
                   jax.ShapeDtypeStruct((n_seq * t_len, D_C), F32)],
        compiler_params=pltpu.CompilerParams(
            dimension_semantics=("parallel",), vmem_limit_bytes=VMEM_LIMIT),
        name="bc_sample",
    )(buf16, proj3, proj3, proj3, proj3, proj3, pw, pscale, ln_g, wm, bm)


WPREP_ROWS = 256


def _wprep_kernel(w_ref, o_ref):
    o_ref[0, :, :3 * D_A] = w_ref[0, :, :3 * D_A].astype(BF16)
    o_ref[0, :, 3 * D_A:P_LORA] = w_ref[0, :, SHIFT_W:].astype(BF16)
    o_ref[0, :, P_LORA:] = w_ref[0, :, 3 * D_A:SHIFT_W].astype(BF16)


def _prep_w_in(w_in):
    spec = pl.BlockSpec((1, WPREP_ROWS, D_INP), lambda l, i: (l, i, 0))
    return pl.pallas_call(
        _wprep_kernel,
        grid=(DEPTH, D_MODEL // WPREP_ROWS),
        in_specs=[spec],
        out_specs=spec,
        out_shape=jax.ShapeDtypeStruct((DEPTH, D_MODEL, D_INP), BF16),
        compiler_params=pltpu.CompilerParams(
            dimension_semantics=("parallel", "parallel"), vmem_limit_bytes=VMEM_LIMIT),
        name="wprep",
    )(w_in)


def kernel(x_prompt, x_sample, state_shift, state_wkv, state_pool, norm_g, final_norm_g, w_in,
           shift_mu, w0, w_up, a0, a_up, k_k, k_a, r_k, lnx_g, lnx_b, pool_w, pool_scale,
           gmlp_ln_g, gmlp_ws, gmlp_b, w_out):
    bp, seq, _ = x_prompt.shape
    bs, dseq, _ = x_sample.shape
    n_p = bp * seq
    n_s = bs * dseq
    xp = x_prompt.reshape(n_p, D_MODEL)
    xs = x_sample.reshape(n_s, D_MODEL)
    hp = _norm_rows(xp, norm_g[0][None])
    hs = _norm_rows(xs, norm_g[0][None])

    w_in_p = _prep_w_in(w_in)
    w_out_h = w_out.astype(BF16)
    mu_l = jnp.pad(shift_mu[:, 3 * D_A:], ((0, 0), (0, D_A - 2 * LORA)))
    prm = jnp.stack([shift_mu[:, :D_A], shift_mu[:, D_A:2 * D_A], shift_mu[:, 2 * D_A:3 * D_A],
                     w0, a0, k_k, k_a, r_k.reshape(DEPTH, D_A), lnx_g, lnx_b, mu_l], axis=1)
    prm = jnp.pad(prm, ((0, 0), (0, 16 - prm.shape[1]), (0, 0)))
    w2 = jnp.concatenate([w_up, a_up], axis=1)
    pw = pool_w.astype(BF16)
    tril = jnp.tril(jnp.ones((CHUNK, CHUNK), F32))
    wm_p = (gmlp_ws * tril).astype(BF16)
    bm_p = jnp.broadcast_to(gmlp_b[:, :, :, None], (DEPTH, N_GROUPS_C, CHUNK, GC))
    nb_s = CHUNK // dseq
    eye_b = jnp.eye(nb_s, dtype=F32)
    ws_small = gmlp_ws[:, :, :dseq, :dseq] * tril[:dseq, :dseq]
    wm_s = jnp.einsum('ab,lgts->lgatbs', eye_b, ws_small).reshape(DEPTH, N_GROUPS_C, CHUNK, CHUNK)
    wm_s = wm_s.astype(BF16)
    bm_s = jnp.broadcast_to(jnp.tile(gmlp_b[:, :, :dseq], (1, 1, nb_s))[:, :, :, None],
                            (DEPTH, N_GROUPS_C, CHUNK, GC))

    ssh_main = state_shift[:, :, :3 * D_A].reshape(DEPTH, bs, 3, 1, D_A).transpose(0, 2, 1, 3, 4)
    ssh_lora = state_shift[:, :, 3 * D_A:].reshape(DEPTH, bs, 1, PAIR)
    wkv_s = state_wkv.reshape(DEPTH, bs, N_PAIRS, PAIR, HEAD_A)
    buf16 = jnp.pad(state_pool, ((0, 0), (0, 0), (HALO - POOL_BUF, 0), (0, 0)))

    p_shift, p_wkv, p_pool, s_shift, s_pool, s_v = [], [], [], [], [], []
    for l in range(DEPTH):
        final = l == DEPTH - 1
        g_out = final_norm_g[None] if final else norm_g[l + 1][None]
        bc_w = (pw[l], pool_scale[l][None], gmlp_ln_g[l][None])

        proj_p = _inproj_h(hp, w_in_p, l, 2048)
        ya_p, wk_p, shm_p, shl_p = _wkv_prompt(proj_p, bp, seq, prm, w2, l)
        cb_p = _bc_prompt(proj_p, bp, seq, *bc_w, wm_p[l], bm_p[l])
        xp = _outproj(ya_p, cb_p, w_out_h, l, xp, g_out, final)
        if not final:
            xp, hp = xp

        proj_s = _inproj_h(hs, w_in_p, l, 1024)
        ya_s, wkv_s, shm_s, shl_s = _wkv_sample(proj_s, bs, dseq, ROWS // dseq, ssh_main, ssh_lora,
                                                wkv_s, prm, w2, l)
        cb_s, vn_s = _bc_sample(proj_s.reshape(bs, dseq, D_INP), 0, bs, dseq, buf16[l], *bc_w,
                                wm_s[l], bm_s[l])
        xs = _outproj(ya_s, cb_s, w_out_h, l, xs, g_out, final)
        if not final:
            xs, hs = xs

        p_shift.append(jnp.concatenate([shm_p[0, :, 0], shm_p[1, :, 0], shm_p[2, :, 0], shl_p[:, 0]], axis=-1))
        s_shift.append(jnp.concatenate([shm_s[0, :, 0], shm_s[1, :, 0], shm_s[2, :, 0], shl_s[:, 0]], axis=-1))
        p_wkv.append(wk_p.reshape(bp, N_HEADS_A, HEAD_A, HEAD_A))
        p_pool.append(jnp.stack([proj_p[(b + 1) * seq - POOL_BUF:(b + 1) * seq, P_UB:P_UB + D_B]
                                 for b in range(bp)]).astype(F32))
        ub_s = proj_s[:, P_UB:P_UB + D_B].astype(F32).reshape(bs, dseq, D_B)
        s_pool.append(jnp.concatenate([state_pool[l], ub_s], axis=1)[:, -POOL_BUF:])
        s_v.append(vn_s.reshape(bs, dseq, D_C))

    y_prompt = xp.reshape(bp, seq, D_MODEL)
    y_sample = xs.reshape(bs, dseq, D_MODEL)
    s_wkv = wkv_s.reshape(DEPTH, bs, N_HEADS_A, HEAD_A, HEAD_A)
    return (y_prompt, y_sample, jnp.stack(p_shift), jnp.stack(p_wkv), jnp.stack(p_pool),
            jnp.stack(s_shift), s_wkv, jnp.stack(s_pool), jnp.stack(s_v))
```
